```python
import math
import jax
import jax.numpy as jnp
from jax import lax
import numpy as np

D_MODEL = 1024
BATCH = 2
SEQ = 8192
DEPTH = 2

GRID_W = 64
CTX_LEN = 256
HEAD_DIM = 64
RMS_EPS = 1e-6
N_MOD = 6

NA_HEADS = 4
NA_WIN_R = 8
NA_WIN_C = 16
NA_COL_BLOCK = 16
NA_BAND_C = NA_COL_BLOCK + NA_WIN_C
NA_WIDTH = NA_HEADS * HEAD_DIM

GQA_Q_HEADS = 8
GQA_KV_HEADS = 2
GQA_Q_WIDTH = GQA_Q_HEADS * HEAD_DIM
GQA_KV_WIDTH = GQA_KV_HEADS * HEAD_DIM
Q_BLOCK = 128
ROPE_THETA = 10000.0

HY_WIDTH = 256
HY_ORDER = 2
HY_EMB_DIM = 33
HY_FILTER_HIDDEN = 64
HY_SHORT_CONV = 3
HY_DECAY_TARGET = 1e-2
HY_FAST_DECAY_PCT = 0.3
HY_SLOW_DECAY_PCT = 1.5

N_BRANCH = 3
SPLIT_SIZES = (NA_WIDTH, NA_WIDTH, NA_WIDTH, GQA_Q_WIDTH, GQA_KV_WIDTH, GQA_KV_WIDTH, 3 * HY_WIDTH, N_BRANCH * D_MODEL)
IN_COLS = sum(SPLIT_SIZES)

N_EXPERTS = 16
N_GROUPS = 4
EXPERTS_PER_GROUP = N_EXPERTS // N_GROUPS
TOP_K = 2
EXPERT_FF = 704

kernel_name = 'hybrid_na_gqa_hyena_moe_dit_block'


def rms_norm(x, w):
    x32 = x.astype(jnp.float32)
    y = x32 * lax.rsqrt(jnp.mean(x32 * x32, axis=-1, keepdims=True) + RMS_EPS)
    return (y * w.astype(jnp.float32)).astype(x.dtype)


def modulate(h, shift, scale):
    return h * (1 + scale) + shift


def rope_2d_tables(n_tokens, dim):
    t = jnp.arange(n_tokens, dtype=jnp.int32)
    rows = (t // GRID_W).astype(jnp.float32)
    cols = (t % GRID_W).astype(jnp.float32)
    d_axis = dim // 2
    inv = ROPE_THETA ** (-jnp.arange(0, d_axis, 2, dtype=jnp.float32) / d_axis)
    ang = jnp.concatenate([rows[:, None] * inv, cols[:, None] * inv], axis=-1)
    return jnp.cos(ang), jnp.sin(ang)


def apply_rope(x, cos, sin):
    x1, x2 = jnp.split(x, 2, axis=-1)
    cs = cos[None, :, None, :].astype(x.dtype)
    sn = sin[None, :, None, :].astype(x.dtype)
    return jnp.concatenate([x1 * cs - x2 * sn, x1 * sn + x2 * cs], axis=-1)


def full_attention(q, k, v):
    b, lq, hq, dh = q.shape
    hkv = k.shape[2]
    qg = q.reshape(b, lq, hkv, hq // hkv, dh)
    s = jnp.einsum('bqkgd,bskd->bkgqs', qg, k).astype(jnp.float32) * (dh ** -0.5)
    p = jax.nn.softmax(s, axis=-1).astype(v.dtype)
    o = jnp.einsum('bkgqs,bskd->bqkgd', p, v)
    return o.reshape(b, lq, hq * dh)


def blocked_attention(q, k, v):
    b, n, hq, dh = q.shape
    nb = n // Q_BLOCK
    qb = q.reshape(b, nb, Q_BLOCK, hq, dh).transpose(1, 0, 2, 3, 4)
    o = lax.map(lambda qi: full_attention(qi, k, v), qb)
    return o.transpose(1, 0, 2, 3).reshape(b, n, hq * dh)


def neighbourhood_attention(q, k, v, k_ctx, v_ctx, rpb):
    b, n, h, dh = q.shape
    rows = n // GRID_W
    win_r = min(NA_WIN_R, rows)
    n_cb = GRID_W // NA_COL_BLOCK
    r = jnp.arange(rows)
    key_rows = jnp.clip(r - win_r // 2, 0, rows - win_r)[:, None] + jnp.arange(win_r)
    j = jnp.arange(n_cb)
    key_cols = jnp.clip(j * NA_COL_BLOCK - NA_WIN_C // 2, 0, GRID_W - NA_BAND_C)[:, None] + jnp.arange(NA_BAND_C)
    q_cols = j[:, None] * NA_COL_BLOCK + jnp.arange(NA_COL_BLOCK)
    win_c0 = jnp.clip(q_cols - NA_WIN_C // 2, 0, GRID_W - NA_WIN_C)[..., None]
    kc = key_cols[:, None, :]
    col_ok = (kc >= win_c0) & (kc < win_c0 + NA_WIN_C)
    d_row = key_rows - r[:, None] + NA_WIN_R - 1
    d_col = jnp.clip(kc - q_cols[..., None] + NA_WIN_C - 1, 0, 2 * NA_WIN_C - 2)
    bias = rpb[:, d_row[:, None, None, :, None], d_col[None, :, :, None, :]].astype(jnp.float32)
    bias = jnp.where(col_ok[None, None, :, :, None, :], bias, -1e30)
    qg = q.reshape(b, rows, n_cb, NA_COL_BLOCK, h, dh)
    kg = k.reshape(b, rows, GRID_W, h, dh)
    vg = v.reshape(b, rows, GRID_W, h, dh)
    ri = key_rows[:, None, :, None]
    ci = key_cols[None, :, None, :]
    k_band = kg[:, ri, ci]
    v_band = vg[:, ri, ci]
    scale = dh ** -0.5
    s_loc = jnp.einsum('brjqhd,brjwkhd->bhrjqwk', qg, k_band).astype(jnp.float32) * scale + bias[None]
    s_ctx = jnp.einsum('brjqhd,bshd->bhrjqs', qg, k_ctx).astype(jnp.float32) * scale
    n_loc = win_r * NA_BAND_C
    s = jnp.concatenate([s_loc.reshape(s_loc.shape[:5] + (n_loc,)), s_ctx], axis=-1)
    p = jax.nn.softmax(s, axis=-1).astype(v.dtype)
    p_loc = p[..., :n_loc].reshape(s_loc.shape)
    o = (jnp.einsum('bhrjqwk,brjwkhd->brjqhd', p_loc, v_band)
         + jnp.einsum('bhrjqs,bshd->brjqhd', p[..., n_loc:], v_ctx))
    return o.reshape(b, n, h * dh)


def short_conv(x, w, bias):
    pad = HY_SHORT_CONV // 2
    y = lax.conv_general_dilated(x, w[:, None, :].astype(x.dtype), window_strides=(1,), padding=[(pad, pad)],
                                 dimension_numbers=('NWC', 'WIO', 'NWC'), feature_group_count=x.shape[-1])
    return y + bias


def hyena_filters(length, w1, b1, w2, b2, w3, freq):
    t = jnp.linspace(0.0, 1.0, length, dtype=jnp.float32)[:, None]
    bands = (HY_EMB_DIM - 1) // 2
    f = jnp.linspace(1e-4, bands - 1, bands, dtype=jnp.float32)
    w = 2.0 * math.pi * jnp.arange(length, dtype=jnp.float32)[:, None] / length
    z = jnp.concatenate([t, jnp.cos(f * w), -jnp.sin(f * w)], axis=-1)
    hid = jnp.sin(freq * (z @ w1 + b1))
    hid = jnp.sin(freq * (hid @ w2 + b2))
    filt = (hid @ w3).astype(jnp.float32).reshape(length, 2, HY_ORDER * HY_WIDTH)
    deltas = jnp.linspace(math.log(HY_DECAY_TARGET) / HY_SLOW_DECAY_PCT, math.log(HY_DECAY_TARGET) / HY_FAST_DECAY_PCT,
                          HY_ORDER * HY_WIDTH, dtype=jnp.float32)
    decay = jnp.exp(-t * jnp.abs(deltas))
    return (filt * decay[:, None, :]).reshape(length, 2, HY_ORDER, HY_WIDTH)


def bidir_long_conv(u, h_fwd, h_bwd, bias_d):
    length, ch = h_fwd.shape
    filt = jnp.concatenate([h_fwd, jnp.zeros((1, ch), jnp.float32), h_bwd[1:][::-1]], axis=0)
    u32 = u.astype(jnp.float32)
    uf = jnp.fft.rfft(u32, n=2 * length, axis=1)
    kf = jnp.fft.rfft(filt, n=2 * length, axis=0)
    y = jnp.fft.irfft(uf * kf[None], n=2 * length, axis=1)[:, :length]
    return (y + u32 * bias_d.astype(jnp.float32)).astype(u.dtype)


def hyena(u, conv_w, conv_b, filt_params, bias_d):
    z = short_conv(u, conv_w, conv_b)
    v, x1, x2 = jnp.split(z, 3, axis=-1)
    filt = hyena_filters(u.shape[1], *filt_params)
    y = v
    for o, gate in enumerate((x1, x2)):
        y = gate * bidir_long_conv(y, filt[:, 0, o], filt[:, 1, o], bias_d[o])
    return y


def merge_branches(ya, yb, yc, gate_cols, b_gate, w_br_a, w_br_b, w_br_c, w_out):
    g = jax.nn.sigmoid((gate_cols + b_gate).astype(jnp.float32)).astype(ya.dtype)
    ga, gb, gc = jnp.split(g, N_BRANCH, axis=-1)
    m = ga * (ya @ w_br_a) + gb * (yb @ w_br_b) + gc * (yc @ w_br_c)
    return m @ w_out


def mixer_block(h_ctx, h_lat, cos, sin, w_in, b_gate, na_rpb, q_norm_w, k_norm_w, hy_conv_w, hy_conv_b,
                filt_params, hy_bias_d, w_br_a, w_br_b, w_br_c, w_out, with_ctx):
    b, n, _ = h_lat.shape
    lc = h_ctx.shape[1]
    p = jnp.concatenate([h_ctx, h_lat], axis=1) @ w_in
    a_q, a_k, a_v, b_q, b_k, b_v, c_u, gate_cols = jnp.split(p, list(np.cumsum(SPLIT_SIZES)[:-1]), axis=-1)
    heads = lambda t, nh: t.reshape(t.shape[0], t.shape[1], nh, HEAD_DIM)
    aq, ak, av = heads(a_q, NA_HEADS), heads(a_k, NA_HEADS), heads(a_v, NA_HEADS)
    ya = neighbourhood_attention(aq[:, lc:], ak[:, lc:], av[:, lc:], ak[:, :lc], av[:, :lc], na_rpb)
    bq = rms_norm(heads(b_q, GQA_Q_HEADS), q_norm_w)
    bk = rms_norm(heads(b_k, GQA_KV_HEADS), k_norm_w)
    bv = heads(b_v, GQA_KV_HEADS)
    k_all = jnp.concatenate([bk[:, :lc], apply_rope(bk[:, lc:], cos, sin)], axis=1)
    yb = blocked_attention(apply_rope(bq[:, lc:], cos, sin), k_all, bv)
    yc = hyena(c_u[:, lc:], hy_conv_w, hy_conv_b, filt_params, hy_bias_d)
    y_lat = merge_branches(ya, yb, yc, gate_cols[:, lc:], b_gate, w_br_a, w_br_b, w_br_c, w_out)
    if not with_ctx:
        return None, y_lat
    ya_c = full_attention(aq[:, :lc], ak[:, :lc], av[:, :lc])
    yb_c = full_attention(bq[:, :lc], bk[:, :lc], bv[:, :lc])
    yc_c = hyena(c_u[:, :lc], hy_conv_w, hy_conv_b, filt_params, hy_bias_d)
    y_ctx = merge_branches(ya_c, yb_c, yc_c, gate_cols[:, :lc], b_gate, w_br_a, w_br_b, w_br_c, w_out)
    return y_ctx, y_lat


def moe(h, router_w, router_b, w_gate, w_up, w_down):
    t = h.shape[0]
    scores = jax.nn.sigmoid((h @ router_w).astype(jnp.float32))
    sel = (scores + router_b.astype(jnp.float32)).reshape(t, N_GROUPS, EXPERTS_PER_GROUP)
    group_score = lax.top_k(sel, 2)[0].sum(-1)
    g_idx = jnp.argmax(group_score, axis=-1)
    in_group = jnp.take_along_axis(sel, g_idx[:, None, None], axis=1)[:, 0]
    _, local = lax.top_k(in_group, TOP_K)
    expert_idx = g_idx[:, None] * EXPERTS_PER_GROUP + local
    wts = jnp.take_along_axis(scores, expert_idx, axis=-1)
    wts = wts / jnp.sum(wts, axis=-1, keepdims=True)
    combine = jnp.sum(jax.nn.one_hot(expert_idx, N_EXPERTS, dtype=jnp.float32) * wts[..., None], axis=1).astype(h.dtype)
    out = jnp.zeros_like(h)
    for e in range(N_EXPERTS):
        ye = (jax.nn.silu(h @ w_gate[e]) * (h @ w_up[e])) @ w_down[e]
        out = out + combine[:, e:e + 1] * ye
    return out


def setup_inputs(seed: int = 0) -> dict:
    key = jax.random.key(seed)
    ks = iter(jax.random.split(key, 40))
    nrm = lambda shape, s: jax.random.normal(next(ks), shape, jnp.float32) * s
    d = D_MODEL
    nl = DEPTH
    hid = HY_FILTER_HIDDEN
    return {
        'x': nrm((BATCH, SEQ, d), 1.0),
        'c': nrm((BATCH, d), 1.0),
        'ctx': nrm((BATCH, CTX_LEN, d), 1.0),
        'c_ctx': nrm((d,), 1.0),
        'ada_w': nrm((nl, d, N_MOD * d), 0.5 * d ** -0.5),
        'ada_b': nrm((nl, N_MOD * d), 0.02),
        'norm1_w': 1.0 + nrm((nl, d), 0.05),
        'norm2_w': 1.0 + nrm((nl, d), 0.05),
        'w_in': nrm((nl, d, IN_COLS), d ** -0.5),
        'b_gate': nrm((nl, N_BRANCH * d), 0.02),
        'na_rpb': nrm((nl, NA_HEADS, 2 * NA_WIN_R - 1, 2 * NA_WIN_C - 1), 0.02),
        'q_norm_w': 1.0 + nrm((nl, HEAD_DIM), 0.05),
        'k_norm_w': 1.0 + nrm((nl, HEAD_DIM), 0.05),
        'hy_conv_w': nrm((nl, HY_SHORT_CONV, 3 * HY_WIDTH), HY_SHORT_CONV ** -0.5),
        'hy_conv_b': nrm((nl, 3 * HY_WIDTH), 0.02),
        'hy_f_w1': nrm((nl, HY_EMB_DIM, hid), HY_EMB_DIM ** -0.5),
        'hy_f_b1': nrm((nl, hid), 0.1),
        'hy_f_w2': nrm((nl, hid, hid), hid ** -0.5),
        'hy_f_b2': nrm((nl, hid), 0.1),
        'hy_f_w3': nrm((nl, hid, 2 * HY_ORDER * HY_WIDTH), 0.005),
        'hy_sin_freq': 1.0 + nrm((nl, hid), 0.05),
        'hy_bias_d': nrm((nl, HY_ORDER, HY_WIDTH), 0.1),
        'w_br_a': nrm((nl, NA_WIDTH, d), NA_WIDTH ** -0.5),
        'w_br_b': nrm((nl, GQA_Q_WIDTH, d), GQA_Q_WIDTH ** -0.5),
        'w_br_c': nrm((nl, HY_WIDTH, d), HY_WIDTH ** -0.5),
        'w_out': nrm((nl, d, d), d ** -0.5),
        'router_w': nrm((d, N_EXPERTS), d ** -0.5),
        'router_b': nrm((N_EXPERTS,), 0.01),
        'exp_w_gate': nrm((nl, N_EXPERTS, d, EXPERT_FF), d ** -0.5),
        'exp_w_up': nrm((nl, N_EXPERTS, d, EXPERT_FF), d ** -0.5),
        'exp_w_down': nrm((nl, N_EXPERTS, EXPERT_FF, d), EXPERT_FF ** -0.5),
        'final_norm_w': 1.0 + nrm((d,), 0.05),
    }


def reference(x, c, ctx, c_ctx, ada_w, ada_b, norm1_w, norm2_w, w_in, b_gate, na_rpb, q_norm_w, k_norm_w,
              hy_conv_w, hy_conv_b, hy_f_w1, hy_f_b1, hy_f_w2, hy_f_b2, hy_f_w3, hy_sin_freq, hy_bias_d,
              w_br_a, w_br_b, w_br_c, w_out, router_w, router_b, exp_w_gate, exp_w_up, exp_w_down, final_norm_w):
    b, n, d = x.shape
    lc = ctx.shape[1]
    cos, sin = rope_2d_tables(n, HEAD_DIM)
    ctx_h = ctx
    for layer in range(DEPTH):
        last = layer == DEPTH - 1
        mod_lat = [m[:, None, :] for m in jnp.split(jax.nn.silu(c) @ ada_w[layer] + ada_b[layer], N_MOD, axis=-1)]
        mod_ctx = [m[None, None, :] for m in jnp.split(jax.nn.silu(c_ctx) @ ada_w[layer] + ada_b[layer], N_MOD, axis=-1)]
        h_lat = modulate(rms_norm(x, norm1_w[layer]), mod_lat[0], mod_lat[1])
        h_ctx = modulate(rms_norm(ctx_h, norm1_w[layer]), mod_ctx[0], mod_ctx[1])
        filt_params = (hy_f_w1[layer], hy_f_b1[layer], hy_f_w2[layer], hy_f_b2[layer], hy_f_w3[layer], hy_sin_freq[layer])
        y_ctx, y_lat = mixer_block(h_ctx, h_lat, cos, sin, w_in[layer], b_gate[layer], na_rpb[layer], q_norm_w[layer],
                                   k_norm_w[layer], hy_conv_w[layer], hy_conv_b[layer], filt_params, hy_bias_d[layer],
                                   w_br_a[layer], w_br_b[layer], w_br_c[layer], w_out[layer], not last)
        x = x + mod_lat[2] * y_lat
        h_lat = modulate(rms_norm(x, norm2_w[layer]), mod_lat[3], mod_lat[4])
        if last:
            y = moe(h_lat.reshape(b * n, d), router_w, router_b, exp_w_gate[layer], exp_w_up[layer], exp_w_down[layer])
            x = x + mod_lat[5] * y.reshape(b, n, d)
        else:
            ctx_h = ctx_h + mod_ctx[2] * y_ctx
            h_ctx = modulate(rms_norm(ctx_h, norm2_w[layer]), mod_ctx[3], mod_ctx[4])
            tokens = jnp.concatenate([h_ctx, h_lat], axis=1).reshape(b * (lc + n), d)
            y = moe(tokens, router_w, router_b, exp_w_gate[layer], exp_w_up[layer], exp_w_down[layer]).reshape(b, lc + n, d)
            ctx_h = ctx_h + mod_ctx[5] * y[:, :lc]
            x = x + mod_lat[5] * y[:, lc:]
    return rms_norm(x, final_norm_w)
```

```python
import functools
import math

import jax
import jax.numpy as jnp
import numpy as np
from jax import lax
from jax.experimental import pallas as pl
from jax.experimental.pallas import tpu as pltpu

F32 = jnp.float32
BF16 = jnp.bfloat16
I32 = jnp.int32
HI = lax.Precision.HIGHEST

GRID_W = 64
HEAD_DIM = 64
RMS_EPS = 1e-6
N_MOD = 6
NA_HEADS = 4
NA_WIN_R = 8
NA_WIN_C = 16
GQA_Q_HEADS = 8
GQA_KV_HEADS = 2
ROPE_THETA = 10000.0
HY_WIDTH = 256
HY_ORDER = 2
HY_EMB_DIM = 33
HY_DECAY_TARGET = 1e-2
HY_FAST_DECAY_PCT = 0.3
HY_SLOW_DECAY_PCT = 1.5
N_BRANCH = 3
N_EXPERTS = 16
N_GROUPS = 4
EXPERTS_PER_GROUP = 4
LANE = 128
SUBLANE = 8
DFT_N2 = 128
VMEM_LIMIT = 56 * 1024 * 1024


def _cp(sem, vmem=VMEM_LIMIT):
    return pltpu.CompilerParams(dimension_semantics=sem, vmem_limit_bytes=vmem)


def _dot(a, b, prec=None):
    return jnp.dot(a, b, preferred_element_type=F32, precision=prec)


def _dot_nt(a, b, prec=None):
    return lax.dot_general(a, b, (((1,), (1,)), ((), ())), preferred_element_type=F32, precision=prec)


def _silu(x):
    return x * jax.nn.sigmoid(x)


def _ada_body(c_ref, w_ref, b_ref, o_ref):
    o_ref[...] = _dot(_silu(c_ref[...]), w_ref[...], HI) + b_ref[...]


def _ada_mod(cvec, ada_w, ada_b):
    rows, d = cvec.shape
    cols = ada_w.shape[1]
    tn = 1536
    return pl.pallas_call(
        _ada_body,
        grid=(cols // tn,),
        in_specs=[pl.BlockSpec((rows, d), lambda j: (0, 0)),
                  pl.BlockSpec((d, tn), lambda j: (0, j)),
                  pl.BlockSpec((1, tn), lambda j: (0, j))],
        out_specs=pl.BlockSpec((rows, tn), lambda j: (0, j)),
        out_shape=jax.ShapeDtypeStruct((rows, cols), F32),
        compiler_params=_cp(("arbitrary",)),
        name="ada_mod",
    )(cvec, ada_w, ada_b.reshape(1, cols))


def _head_rms(x, ones_bd, w_row):
    x2 = x * x
    hi = x2.astype(BF16)
    lo = (x2 - hi.astype(F32)).astype(BF16)
    ss = _dot(hi, ones_bd) + _dot(lo, ones_bd)
    return x * lax.rsqrt(ss * (1.0 / HEAD_DIM) + RMS_EPS) * w_row


def _rope(x, cos_t, sin_t):
    n = x.shape[1]
    half = HEAD_DIM // 2
    lane = lax.broadcasted_iota(I32, x.shape, 1)
    first = (lane % HEAD_DIM) < half
    swapped = jnp.where(first, pltpu.roll(x, n - half, 1), pltpu.roll(x, half, 1))
    return x * cos_t + swapped * sin_t


def _inproj_body(x_ref, mod_ref, nw_ref, w_ref, bg_ref, qw_ref, kw_ref, cos_ref, sin_ref, bdq_ref, bdk_ref, dup_ref,
                 aq_ref, ak_ref, av_ref, bq_ref, bk_ref, bv_ref, cu_ref, g_ref):
    x = x_ref[0]
    mod = mod_ref[0]
    y = x * lax.rsqrt(jnp.mean(x * x, axis=-1, keepdims=True) + RMS_EPS) * nw_ref[...]
    h = (y * (1.0 + mod[1:2]) + mod[0:1]).astype(BF16)
    scale = HEAD_DIM ** -0.5
    na = NA_HEADS * HEAD_DIM
    qg = GQA_Q_HEADS * HEAD_DIM
    kg = GQA_KV_HEADS * HEAD_DIM
    o = 0
    pa = _dot(h, w_ref[:, o:o + 3 * na])
    aq_ref[0] = (pa[:, :na] * scale).astype(BF16)
    ak_ref[0] = pa[:, na:2 * na].astype(BF16)
    av_ref[0] = pa[:, 2 * na:].astype(BF16)
    o += 3 * na
    cos2 = cos_ref[...]
    sin2 = sin_ref[...]
    pq = _dot(h, w_ref[:, o:o + qg])
    qn = _head_rms(pq, bdq_ref[...], qw_ref[...])
    reps = qg // LANE
    qr = _rope(qn, jnp.concatenate([cos2] * reps, axis=1), jnp.concatenate([sin2] * reps, axis=1))
    bq_ref[0] = (qr * scale).astype(BF16)
    o += qg
    pkv = _dot(h, w_ref[:, o:o + 2 * kg])
    kn = _head_rms(pkv[:, :kg], bdk_ref[...], kw_ref[...])
    kr = _rope(kn, cos2, sin2).astype(BF16)
    dup = dup_ref[...]
    bk_ref[0] = _dot(kr, dup).astype(BF16)
    bv_ref[0] = _dot(pkv[:, kg:].astype(BF16), dup).astype(BF16)
    o += 2 * kg
    cu_ref[0] = _dot(h, w_ref[:, o:o + 3 * HY_WIDTH])
    o += 3 * HY_WIDTH
    g_ref[0] = jax.nn.sigmoid(_dot(h, w_ref[:, o:]) + bg_ref[...]).astype(BF16)


def _norm_inproj(xc, mod, norm_w, w_in_bf, b_gate, q_norm_w, k_norm_w, cos_t, sin_t, lc):
    b, t, d = xc.shape
    tm = 256
    na = NA_HEADS * HEAD_DIM
    qg = GQA_Q_HEADS * HEAD_DIM
    kg = GQA_KV_HEADS * HEAD_DIM
    ng = N_BRANCH * d
    nb = b
    bdq = jnp.asarray(np.kron(np.eye(GQA_Q_HEADS), np.ones((HEAD_DIM, HEAD_DIM))), BF16)
    bdk = jnp.asarray(np.kron(np.eye(GQA_KV_HEADS), np.ones((HEAD_DIM, HEAD_DIM))), BF16)
    dup_np = np.zeros((kg, 2 * kg), np.float32)
    for g in range(GQA_KV_HEADS):
        for r in range(2):
            dup_np[g * HEAD_DIM:(g + 1) * HEAD_DIM, (2 * g + r) * HEAD_DIM:(2 * g + r + 1) * HEAD_DIM] = np.eye(HEAD_DIM)
    dup = jnp.asarray(dup_np, BF16)
    qw = jnp.tile(q_norm_w.astype(F32), GQA_Q_HEADS).reshape(1, qg)
    kw = jnp.tile(k_norm_w.astype(F32), GQA_KV_HEADS).reshape(1, kg)
    ctx_tiles = lc // tm
    full = lambda shape: pl.BlockSpec(shape, lambda bi, ti: (0,) * len(shape))
    tok = lambda width: pl.BlockSpec((1, tm, width), lambda bi, ti: (bi, ti, 0))
    outs = [(na, BF16), (na, BF16), (na, BF16), (qg, BF16), (2 * kg, BF16), (2 * kg, BF16), (3 * HY_WIDTH, F32), (ng, BF16)]
    return pl.pallas_call(
        _inproj_body,
        grid=(b, t // tm),
        in_specs=[tok(d),
                  pl.BlockSpec((1, N_MOD, d), lambda bi, ti: (jnp.where(ti < ctx_tiles, nb, bi), 0, 0)),
                  full((1, d)), full(w_in_bf.shape), full((1, ng)), full((1, qg)), full((1, kg)),
                  pl.BlockSpec((tm, LANE), lambda bi, ti: (ti, 0)),
                  pl.BlockSpec((tm, LANE), lambda bi, ti: (ti, 0)),
                  full(bdq.shape), full(bdk.shape), full(dup.shape)],
        out_specs=[tok(w) for w, _ in outs],
        out_shape=[jax.ShapeDtypeStruct((b, t, w), dt) for w, dt in outs],
        compiler_params=_cp(("arbitrary", "arbitrary")),
        name="norm_inproj",
    )(xc, mod, norm_w.reshape(1, d), w_in_bf, b_gate.reshape(1, ng), qw, kw, cos_t, sin_t, bdq, bdk, dup)


def _rope_tables(lc, n):
    tpos = jnp.arange(n, dtype=I32)
    rows = (tpos // GRID_W).astype(F32)
    cols = (tpos % GRID_W).astype(F32)
    d_axis = HEAD_DIM // 2
    inv = ROPE_THETA ** (-jnp.arange(0, d_axis, 2, dtype=F32) / d_axis)
    ang = jnp.concatenate([rows[:, None] * inv, cols[:, None] * inv], axis=-1)
    cos, sin = jnp.cos(ang), jnp.sin(ang)
    cos_h = jnp.concatenate([cos, cos], axis=-1)
    sin_h = jnp.concatenate([-sin, sin], axis=-1)
    cos_h = jnp.concatenate([jnp.ones((lc, HEAD_DIM), F32), cos_h], axis=0)
    sin_h = jnp.concatenate([jnp.zeros((lc, HEAD_DIM), F32), sin_h], axis=0)
    return jnp.concatenate([cos_h, cos_h], axis=-1), jnp.concatenate([sin_h, sin_h], axis=-1)


NA_ROWS_PER_STEP = 4


def _na_bias_table(rpb):
    qc = np.arange(GRID_W)[:, None]
    kc = np.arange(GRID_W)[None, :]
    win_c0 = np.clip(qc - NA_WIN_C // 2, 0, GRID_W - NA_WIN_C)
    col_ok = (kc >= win_c0) & (kc < win_c0 + NA_WIN_C)
    d_col = np.clip(kc - qc + NA_WIN_C - 1, 0, 2 * NA_WIN_C - 2)
    base = np.arange(NA_WIN_R)[:, None] + np.arange(NA_WIN_R)[None, :]
    tab = rpb.astype(F32)[:, base][:, :, :, d_col]
    tab = jnp.where(jnp.asarray(col_ok)[None, None, None], tab, -1e30)
    tab = tab.transpose(1, 0, 3, 2, 4)
    return tab.reshape(NA_WIN_R, NA_HEADS * GRID_W, NA_WIN_R * GRID_W)


def _na_body(q_ref, k_ref, v_ref, bias_ref, o_ref, *, lc, n_rows):
    i = pl.program_id(1)
    nq = NA_HEADS * GRID_W
    lane_q = lax.broadcasted_iota(I32, (GRID_W, NA_HEADS * HEAD_DIM), 1) // HEAD_DIM
    lane_o = lane_q
    kctx = k_ref[0, 0:lc, :]
    vctx = v_ref[0, 0:lc, :]
    for j in range(NA_ROWS_PER_STEP):
        r = i * NA_ROWS_PER_STEP + j
        kr0 = jnp.clip(r - NA_WIN_R // 2, 0, n_rows - NA_WIN_R)
        variant = kr0 - r + NA_WIN_R - 1
        kstart = pl.multiple_of(lc + kr0 * GRID_W, GRID_W)
        kwin = k_ref[0, pl.ds(kstart, NA_WIN_R * GRID_W), :]
        vwin = v_ref[0, pl.ds(kstart, NA_WIN_R * GRID_W), :]
        q = q_ref[0, j * GRID_W:(j + 1) * GRID_W, :]
        qm = jnp.concatenate([jnp.where(lane_q == h, q, jnp.zeros_like(q)) for h in range(NA_HEADS)], axis=0)
        s_loc = _dot_nt(qm, kwin) + bias_ref[variant]
        s_ctx = _dot_nt(qm, kctx)
        m = jnp.maximum(jnp.max(s_loc, axis=-1, keepdims=True), jnp.max(s_ctx, axis=-1, keepdims=True))
        p_loc = jnp.exp(s_loc - m)
        p_ctx = jnp.exp(s_ctx - m)
        l = jnp.sum(p_loc, axis=-1, keepdims=True) + jnp.sum(p_ctx, axis=-1, keepdims=True)
        o = (_dot(p_loc.astype(BF16), vwin) + _dot(p_ctx.astype(BF16), vctx)) / l
        out = jnp.zeros((GRID_W, NA_HEADS * HEAD_DIM), F32)
        for h in range(NA_HEADS):
            out = jnp.where(lane_o == h, o[h * GRID_W:(h + 1) * GRID_W], out)
        o_ref[0, j * GRID_W:(j + 1) * GRID_W, :] = out.astype(o_ref.dtype)
    del nq


def _na_attention(aq, ak, av, bias_tab, lc):
    b, t, w = aq.shape
    n = t - lc
    n_rows = n // GRID_W
    assert n % GRID_W == 0 and n_rows >= NA_WIN_R and n_rows % NA_ROWS_PER_STEP == 0
    tq = NA_ROWS_PER_STEP * GRID_W
    assert lc % tq == 0
    off = lc // tq
    return pl.pallas_call(
        functools.partial(_na_body, lc=lc, n_rows=n_rows),
        grid=(b, n_rows // NA_ROWS_PER_STEP),
        in_specs=[pl.BlockSpec((1, tq, w), lambda bi, i: (bi, i + off, 0)),
                  pl.BlockSpec((1, t, w), lambda bi, i: (bi, 0, 0)),
                  pl.BlockSpec((1, t, w), lambda bi, i: (bi, 0, 0)),
                  pl.BlockSpec(bias_tab.shape, lambda bi, i: (0, 0, 0))],
        out_specs=pl.BlockSpec((1, tq, w), lambda bi, i: (bi, i, 0)),
        out_shape=jax.ShapeDtypeStruct((b, n, w), BF16),
        compiler_params=_cp(("arbitrary", "arbitrary")),
        name="na_attention",
    )(aq, ak, av, bias_tab)


def _pair_attn_body(q_ref, k_ref, v_ref, o_ref, *, kc):
    q = q_ref[0]
    tq = q.shape[0]
    tk = k_ref.shape[1]
    lane = lax.broadcasted_iota(I32, q.shape, 1)
    lo = lane < HEAD_DIM
    zero = jnp.zeros_like(q)
    qq = jnp.concatenate([jnp.where(lo, q, zero), jnp.where(lo, zero, q)], axis=0)
    n_chunks = tk // kc

    def step(c, carry):
        m, l, acc = carry
        start = 0 if n_chunks == 1 else pl.multiple_of(c * kc, kc)
        kk = k_ref[0, pl.ds(start, kc), :]
        vv = v_ref[0, pl.ds(start, kc), :]
        s = _dot_nt(qq, kk)
        m_new = jnp.maximum(m, jnp.max(s, axis=-1, keepdims=True))
        alpha = jnp.exp(m - m_new)
        p = jnp.exp(s - m_new)
        l = alpha * l + jnp.sum(p, axis=-1, keepdims=True)
        acc = alpha * acc + _dot(p.astype(BF16), vv)
        return m_new, l, acc

    init = (jnp.full((2 * tq, 1), -jnp.inf, F32), jnp.zeros((2 * tq, 1), F32), jnp.zeros((2 * tq, LANE), F32))
    if n_chunks == 1:
        m, l, acc = step(0, init)
    else:
        m, l, acc = lax.fori_loop(0, n_chunks, step, init)
    o = acc / l
    o_ref[0] = jnp.where(lo, o[:tq], o[tq:]).astype(o_ref.dtype)


def _pair_attention(q, k, v, *, q_row0, n_q, kv_of_pair, n_pairs):
    b = q.shape[0]
    tk = k.shape[1]
    tq = 256
    assert q_row0 % tq == 0 and n_q % tq == 0
    kc = 768 if tk % 768 == 0 else tk
    off = q_row0 // tq
    return pl.pallas_call(
        functools.partial(_pair_attn_body, kc=kc),
        grid=(b, n_pairs, n_q // tq),
        in_specs=[pl.BlockSpec((1, tq, LANE), lambda bi, p, i: (bi, i + off, p)),
                  pl.BlockSpec((1, tk, LANE), lambda bi, p, i: (bi, 0, kv_of_pair(p))),
                  pl.BlockSpec((1, tk, LANE), lambda bi, p, i: (bi, 0, kv_of_pair(p)))],
        out_specs=pl.BlockSpec((1, tq, LANE), lambda bi, p, i: (bi, i, p)),
        out_shape=jax.ShapeDtypeStruct((b, n_q, n_pairs * LANE), BF16),
        compiler_params=_cp(("arbitrary", "arbitrary", "arbitrary")),
        name="pair_attention",
    )(q, k, v)


def _filter_body(w1t_ref, w1c_ref, w1s_ref, b1_ref, w2_ref, b2_ref, w3_ref, fr_ref, o_ref, *, length, rows, transposed):
    g = pl.program_id(0)
    n_total = 2 * length
    bands = (HY_EMB_DIM - 1) // 2
    f_row = 1e-4 + lax.broadcasted_iota(I32, (1, bands), 1).astype(F32) * ((bands - 1 - 1e-4) / (bands - 1))
    nch = HY_ORDER * HY_WIDTH
    d_lo = math.log(HY_DECAY_TARGET) / HY_SLOW_DECAY_PCT
    d_hi = math.log(HY_DECAY_TARGET) / HY_FAST_DECAY_PCT
    deltas = d_lo + lax.broadcasted_iota(I32, (1, nch), 1).astype(F32) * ((d_hi - d_lo) / (nch - 1))
    freq = fr_ref[...]
    sub = DFT_N2 if transposed else rows
    for j in range(rows // sub):
        tt = g * rows + j * sub + lax.broadcasted_iota(I32, (sub, 1), 0)
        pos = jnp.where(tt < length, tt, n_total - tt).astype(F32)
        tn = pos * (1.0 / (length - 1))
        w = pos * (2.0 * math.pi / length)
        arg = w * f_row
        pre = tn * w1t_ref[...] + _dot(jnp.cos(arg), w1c_ref[...], HI) - _dot(jnp.sin(arg), w1s_ref[...], HI)
        h1 = jnp.sin(freq * (pre + b1_ref[...]))
        h2 = jnp.sin(freq * (_dot(h1, w2_ref[...], HI) + b2_ref[...]))
        filt = _dot(h2, w3_ref[0], HI) * jnp.exp(-tn * jnp.abs(deltas))
        filt = jnp.where(tt == length, 0.0, filt)
        if transposed:
            o_ref[:, j, :] = filt
        else:
            o_ref[...] = filt


def _hyena_filter(length, w1, b1, w2, b2, w3, freq, *, transposed):
    hid = w1.shape[1]
    nch = HY_ORDER * HY_WIDTH
    n_total = 2 * length
    bands = (HY_EMB_DIM - 1) // 2
    rows = SUBLANE * DFT_N2 if transposed else length
    n_steps = n_total // rows
    assert n_total % rows == 0 and n_steps % 2 == 0
    w3d = w3.reshape(hid, 2, nch).transpose(1, 0, 2)
    full = lambda shape: pl.BlockSpec(shape, lambda g: (0,) * len(shape))
    if transposed:
        out_spec = pl.BlockSpec((DFT_N2, SUBLANE, nch), lambda g: (0, g, 0))
        out_shape = jax.ShapeDtypeStruct((DFT_N2, n_total // DFT_N2, nch), F32)
    else:
        out_spec = pl.BlockSpec((rows, nch), lambda g: (g, 0))
        out_shape = jax.ShapeDtypeStruct((n_total, nch), F32)
    return pl.pallas_call(
        functools.partial(_filter_body, length=length, rows=rows, transposed=transposed),
        grid=(n_steps,),
        in_specs=[full((1, hid)), full((bands, hid)), full((bands, hid)), full((1, hid)), full((hid, hid)), full((1, hid)),
                  pl.BlockSpec((1, hid, nch), lambda g: (g // (n_steps // 2), 0, 0)), full((1, hid))],
        out_specs=out_spec,
        out_shape=out_shape,
        compiler_params=_cp(("arbitrary",)),
        name="hyena_filter",
    )(w1[0:1], w1[1:1 + bands], w1[1 + bands:], b1.reshape(1, hid), w2, b2.reshape(1, hid), w3d, freq.reshape(1, hid))


def _dft_tables(n1, n1_used):
    n = n1 * DFT_N2
    n2 = np.arange(DFT_N2)[:, None, None]
    k1 = np.arange(n1)[None, :, None]
    nn1 = np.arange(n1_used)[None, None, :]
    ang = 2.0 * np.pi * ((k1 * (DFT_N2 * nn1 + n2)) % n) / n
    g_fwd = np.concatenate([np.cos(ang), -np.sin(ang)], axis=1)
    g_inv = np.transpose(g_fwd, (0, 2, 1)) / n
    kk = np.arange(DFT_N2)
    a2 = 2.0 * np.pi * ((kk[:, None] * kk[None, :]) % DFT_N2) / DFT_N2
    fr, fi = np.cos(a2), -np.sin(a2)
    f2 = np.block([[fr, -fi], [fi, fr]])
    f2c = np.block([[fr, fi], [-fi, fr]])
    cast = lambda a: jnp.asarray(a.astype(np.float32)).astype(BF16)
    return cast(g_fwd), cast(g_inv), cast(f2), cast(f2c)


def _stage1_body(x_ref, g_ref, o_ref, *, n1):
    for j in range(SUBLANE):
        res = _dot(g_ref[j], x_ref[0, j].astype(BF16))
        o_ref[0, 0, :, j, :] = res[:n1]
        o_ref[0, 1, :, j, :] = res[n1:]


def _dft_stage1(x_t, g_fwd):
    bx, _, n1u, c = x_t.shape
    n1 = g_fwd.shape[1] // 2
    return pl.pallas_call(
        functools.partial(_stage1_body, n1=n1),
        grid=(bx, DFT_N2 // SUBLANE),
        in_specs=[pl.BlockSpec((1, SUBLANE, n1u, c), lambda bi, g: (bi, g, 0, 0)),
                  pl.BlockSpec((SUBLANE, 2 * n1, n1u), lambda bi, g: (g, 0, 0))],
        out_specs=pl.BlockSpec((1, 2, n1, SUBLANE, c), lambda bi, g: (bi, 0, 0, g, 0)),
        out_shape=jax.ShapeDtypeStruct((bx, 2, n1, DFT_N2, c), F32),
        compiler_params=_cp(("arbitrary", "arbitrary")),
        name="dft_stage1",
    )(x_t, g_fwd)


def _stage2_spec_body(a_ref, f2_ref, o_ref):
    for j in range(SUBLANE):
        slab = jnp.concatenate([a_ref[0, 0, j], a_ref[0, 1, j]], axis=0).astype(BF16)
        o_ref[j] = _dot(f2_ref[...], slab)


def _dft_stage2_spectrum(a, f2):
    _, _, n1, _, c = a.shape
    return pl.pallas_call(
        _stage2_spec_body,
        grid=(n1 // SUBLANE,),
        in_specs=[pl.BlockSpec((1, 2, SUBLANE, DFT_N2, c), lambda g: (0, 0, g, 0, 0)),
                  pl.BlockSpec(f2.shape, lambda g: (0, 0))],
        out_specs=pl.BlockSpec((SUBLANE, 2 * DFT_N2, c), lambda g: (g, 0, 0)),
        out_shape=jax.ShapeDtypeStruct((n1, 2 * DFT_N2, c), F32),
        compiler_params=_cp(("arbitrary",)),
        name="dft_stage2_spectrum",
    )(a, f2)


def _stage2_conv_body(a_ref, k_ref, f2_ref, f2c_ref, o_ref):
    h = DFT_N2
    for j in range(SUBLANE):
        slab = jnp.concatenate([a_ref[0, 0, j], a_ref[0, 1, j]], axis=0).astype(BF16)
        x = _dot(f2_ref[...], slab)
        kf = k_ref[j]
        xr, xi, kr, ki = x[:h], x[h:], kf[:h], kf[h:]
        y = jnp.concatenate([xr * kr - xi * ki, xr * ki + xi * kr], axis=0).astype(BF16)
        bv = _dot(f2c_ref[...], y)
        o_ref[0, 0, :, j, :] = bv[:h]
        o_ref[0, 1, :, j, :] = bv[h:]


def _dft_stage2_conv(a, kspec, order, f2, f2c):
    bx, _, n1, _, c = a.shape
    return pl.pallas_call(
        _stage2_conv_body,
        grid=(bx, n1 // SUBLANE),
        in_specs=[pl.BlockSpec((1, 2, SUBLANE, DFT_N2, c), lambda bi, g: (bi, 0, g, 0, 0)),
                  pl.BlockSpec((SUBLANE, 2 * DFT_N2, c), lambda bi, g: (g, 0, order)),
                  pl.BlockSpec(f2.shape, lambda bi, g: (0, 0)),
                  pl.BlockSpec(f2c.shape, lambda bi, g: (0, 0))],
        out_specs=pl.BlockSpec((1, 2, DFT_N2, SUBLANE, c), lambda bi, g: (bi, 0, 0, g, 0)),
        out_shape=jax.ShapeDtypeStruct((bx, 2, DFT_N2, n1, c), F32),
        compiler_params=_cp(("arbitrary", "arbitrary")),
        name="dft_stage2_conv",
    )(a, kspec, f2, f2c)


def _inverse_body(b_ref, g_ref, u_ref, gate_ref, bias_ref, o_ref, *, token_order):
    for j in range(SUBLANE):
        slab = jnp.concatenate([b_ref[0, 0, j], b_ref[0, 1, j]], axis=0).astype(BF16)
        y = _dot(g_ref[j], slab)
        out = gate_ref[0, j] * (y + u_ref[0, j] * bias_ref[...])
        if token_order:
            o_ref[0, :, j, :] = out
        else:
            o_ref[0, j] = out


def _dft_inverse(bv, g_inv, u_t, gate_t, bias, *, token_order):
    bx, _, _, n1, c = bv.shape
    n1u = g_inv.shape[1]
    t_spec = pl.BlockSpec((1, SUBLANE, n1u, c), lambda bi, g: (bi, g, 0, 0))
    if token_order:
        out_spec = pl.BlockSpec((1, n1u, SUBLANE, c), lambda bi, g: (bi, 0, g, 0))
        out_shape = jax.ShapeDtypeStruct((bx, n1u, DFT_N2, c), F32)
    else:
        out_spec = t_spec
        out_shape = jax.ShapeDtypeStruct((bx, DFT_N2, n1u, c), F32)
    return pl.pallas_call(
        functools.partial(_inverse_body, token_order=token_order),
        grid=(bx, DFT_N2 // SUBLANE),
        in_specs=[pl.BlockSpec((1, 2, SUBLANE, n1, c), lambda bi, g: (bi, 0, g, 0, 0)),
                  pl.BlockSpec((SUBLANE, n1u, 2 * n1), lambda bi, g: (g, 0, 0)),
                  t_spec, t_spec,
                  pl.BlockSpec((1, c), lambda bi, g: (0, 0))],
        out_specs=out_spec,
        out_shape=out_shape,
        compiler_params=_cp(("arbitrary", "arbitrary")),
        name="dft_inverse",
    )(bv, g_inv, u_t, gate_t, bias.reshape(1, c))


def _shortconv_body(prev_ref, cur_ref, next_ref, w_ref, b_ref, v_ref, x1_ref, x2_ref, *, n_steps):
    i = pl.program_id(1)
    u = cur_ref[0]
    rows = u.shape[0]
    prev_row = jnp.where(i > 0, prev_ref[0, SUBLANE - 1:SUBLANE, :], 0.0)
    next_row = jnp.where(i < n_steps - 1, next_ref[0, 0:1, :], 0.0)
    ridx = lax.broadcasted_iota(I32, u.shape, 0)
    up = jnp.where(ridx == 0, prev_row, pltpu.roll(u, 1, 0))
    dn = jnp.where(ridx == rows - 1, next_row, pltpu.roll(u, rows - 1, 0))
    z = up * w_ref[0:1] + u * w_ref[1:2] + dn * w_ref[2:3] + b_ref[...]
    cw = HY_WIDTH
    for j in range(rows // DFT_N2):
        zj = z[j * DFT_N2:(j + 1) * DFT_N2]
        v_ref[0, :, j, :] = zj[:, :cw]
        x1_ref[0, :, j, :] = zj[:, cw:2 * cw]
        x2_ref[0, :, j, :] = zj[:, 2 * cw:]


def _short_conv_t(cu, conv_w, conv_b):
    b, length, c3 = cu.shape
    rows = SUBLANE * DFT_N2
    assert length % rows == 0
    n_steps = length // rows
    rb = rows // SUBLANE
    n1u = length // DFT_N2
    out_spec = pl.BlockSpec((1, DFT_N2, SUBLANE, HY_WIDTH), lambda bi, i: (bi, 0, i, 0))
    out_shape = jax.ShapeDtypeStruct((b, DFT_N2, n1u, HY_WIDTH), F32)
    return pl.pallas_call(
        functools.partial(_shortconv_body, n_steps=n_steps),
        grid=(b, n_steps),
        in_specs=[pl.BlockSpec((1, SUBLANE, c3), lambda bi, i: (bi, jnp.maximum(i * rb - 1, 0), 0)),
                  pl.BlockSpec((1, rows, c3), lambda bi, i: (bi, i, 0)),
                  pl.BlockSpec((1, SUBLANE, c3), lambda bi, i: (bi, jnp.minimum((i + 1) * rb, n_steps * rb - 1), 0)),
                  pl.BlockSpec((3, c3), lambda bi, i: (0, 0)),
                  pl.BlockSpec((1, c3), lambda bi, i: (0, 0))],
        out_specs=[out_spec] * 3,
        out_shape=[out_shape] * 3,
        compiler_params=_cp(("arbitrary", "arbitrary")),
        name="hyena_short_conv",
    )(cu, cu, cu, conv_w, conv_b.reshape(1, c3))


def _hyena_long(cu_lat, conv_w, conv_b, filt_params, bias_d):
    b, length, _ = cu_lat.shape
    n1u = length // DFT_N2
    n1 = 2 * n1u
    g_sig, g_inv, f2, f2c = _dft_tables(n1, n1u)
    g_full = _dft_tables(n1, n1)[0]
    filt_t = _hyena_filter(length, *filt_params, transposed=True)
    kspec = _dft_stage2_spectrum(_dft_stage1(filt_t[None], g_full), f2)
    v_t, x1_t, x2_t = _short_conv_t(cu_lat, conv_w, conv_b)
    y = v_t
    for o, gate in enumerate((x1_t, x2_t)):
        a = _dft_stage1(y, g_sig)
        bv = _dft_stage2_conv(a, kspec, o, f2, f2c)
        y = _dft_inverse(bv, g_inv, y, gate, bias_d[o], token_order=(o == HY_ORDER - 1))
    return y.reshape(b, length, HY_WIDTH)


def _hyena_small_body(cu_ref, w_ref, b_ref, ks_ref, ff_ref, fi_ref, bias_ref, o_ref):
    u = cu_ref[0]
    rows = u.shape[0]
    ridx = lax.broadcasted_iota(I32, u.shape, 0)
    up = jnp.where(ridx == 0, 0.0, pltpu.roll(u, 1, 0))
    dn = jnp.where(ridx == rows - 1, 0.0, pltpu.roll(u, rows - 1, 0))
    z = up * w_ref[0:1] + u * w_ref[1:2] + dn * w_ref[2:3] + b_ref[...]
    cw = HY_WIDTH
    nf = ff_ref.shape[0] // 2
    y = z[:, :cw]
    for o in range(HY_ORDER):
        gate = z[:, (o + 1) * cw:(o + 2) * cw]
        x = _dot(ff_ref[...], y.astype(BF16))
        kf = ks_ref[:, o * cw:(o + 1) * cw]
        xr, xi, kr, ki = x[:nf], x[nf:], kf[:nf], kf[nf:]
        prod = jnp.concatenate([xr * kr - xi * ki, xr * ki + xi * kr], axis=0).astype(BF16)
        conv = _dot(fi_ref[...], prod)
        y = gate * (conv + y * bias_ref[o:o + 1])
    o_ref[0] = y


def _spectrum_small_body(ff_ref, f_ref, o_ref):
    o_ref[...] = _dot(ff_ref[...], f_ref[...].astype(BF16))


def _hyena_small(cu_ctx, conv_w, conv_b, filt_params, bias_d):
    b, length, c3 = cu_ctx.shape
    n = 2 * length
    kt = np.arange(n)[:, None] * np.arange(n)[None, :]
    ang = 2.0 * np.pi * (kt % n) / n
    fwd = np.concatenate([np.cos(ang), -np.sin(ang)], axis=0)
    cast = lambda a: jnp.asarray(a.astype(np.float32)).astype(BF16)
    f_full = cast(fwd)
    f_sig = cast(fwd[:, :length])
    f_inv = cast(np.transpose(fwd[:, :length]) / n)
    filt = _hyena_filter(length, *filt_params, transposed=False)
    nch = filt.shape[1]
    kspec = pl.pallas_call(
        _spectrum_small_body,
        out_shape=jax.ShapeDtypeStruct((2 * n, nch), F32),
        compiler_params=_cp(None),
        name="hyena_small_spectrum",
    )(f_full, filt)
    full = lambda shape: pl.BlockSpec(shape, lambda bi: (0,) * len(shape))
    return pl.pallas_call(
        _hyena_small_body,
        grid=(b,),
        in_specs=[pl.BlockSpec((1, length, c3), lambda bi: (bi, 0, 0)), full((3, c3)), full((1, c3)), full(kspec.shape),
                  full(f_sig.shape), full(f_inv.shape), full(bias_d.shape)],
        out_specs=pl.BlockSpec((1, length, HY_WIDTH), lambda bi: (bi, 0, 0)),
        out_shape=jax.ShapeDtypeStruct((b, length, HY_WIDTH), F32),
        compiler_params=_cp(("arbitrary",)),
        name="hyena_small",
    )(cu_ctx, conv_w, conv_b.reshape(1, c3), kspec, f_sig, f_inv, bias_d)


def _merge_body(x_ref, mod_ref, ya_ref, yb_ref, yc_ref, g_ref, wa_ref, wb_ref, wc_ref, wo_ref, nw_ref, rw_ref,
                xo_ref, h_ref, lg_ref):
    d = x_ref.shape[2]
    g = g_ref[0].astype(F32)
    m = (g[:, :d] * _dot(ya_ref[0], wa_ref[...]) + g[:, d:2 * d] * _dot(yb_ref[0], wb_ref[...])
         + g[:, 2 * d:] * _dot(yc_ref[0].astype(BF16), wc_ref[...]))
    y = _dot(m.astype(BF16), wo_ref[...])
    mod = mod_ref[0]
    x = x_ref[0] + mod[2:3] * y
    xo_ref[0] = x
    hn = x * lax.rsqrt(jnp.mean(x * x, axis=-1, keepdims=True) + RMS_EPS) * nw_ref[...]
    h = hn * (1.0 + mod[4:5]) + mod[3:4]
    for c in range(d // LANE):
        h_ref[:, c, :] = h[:, c * LANE:(c + 1) * LANE]
    lg_ref[...] = _dot_nt(rw_ref[...], h, HI)


def _merge(xc, mod, ya, yb, yc, g, w_a, w_b, w_c, w_o, norm2_w, router_wt, lc):
    b, t, d = xc.shape
    tm = 256
    nb = b
    ctx_tiles = lc // tm
    nt = t // tm
    ne = router_wt.shape[0]
    full = lambda shape: pl.BlockSpec(shape, lambda bi, ti: (0,) * len(shape))
    tok = lambda width: pl.BlockSpec((1, tm, width), lambda bi, ti: (bi, ti, 0))
    return pl.pallas_call(
        _merge_body,
        grid=(b, nt),
        in_specs=[tok(d),
                  pl.BlockSpec((1, N_MOD, d), lambda bi, ti: (jnp.where(ti < ctx_tiles, nb, bi), 0, 0)),
                  tok(ya.shape[2]), tok(yb.shape[2]), tok(yc.shape[2]), tok(g.shape[2]),
                  full(w_a.shape), full(w_b.shape), full(w_c.shape), full(w_o.shape), full((1, d)), full(router_wt.shape)],
        out_specs=[tok(d),
                   pl.BlockSpec((tm, d // LANE, LANE), lambda bi, ti: (bi * nt + ti, 0, 0)),
                   pl.BlockSpec((ne, tm), lambda bi, ti: (0, bi * nt + ti))],
        out_shape=[jax.ShapeDtypeStruct((b, t, d), F32),
                   jax.ShapeDtypeStruct((b * t, d // LANE, LANE), F32),
                   jax.ShapeDtypeStruct((ne, b * t), F32)],
        compiler_params=_cp(("arbitrary", "arbitrary")),
        name="merge_outproj",
    )(xc, mod, ya, yb, yc, g, w_a, w_b, w_c, w_o, norm2_w.reshape(1, d), router_wt)


ROUTE_TILE = 512
MOE_ROW_TILE = 512
FF_PAD = 768
GATHER_CHUNK = 512


def _top2_of4(a):
    m1 = jnp.maximum(jnp.maximum(a[0], a[1]), jnp.maximum(a[2], a[3]))
    i1 = jnp.where(a[0] == m1, 0, jnp.where(a[1] == m1, 1, jnp.where(a[2] == m1, 2, 3)))
    neg = jnp.full_like(m1, -jnp.inf)
    r = [jnp.where(i1 == j, neg, a[j]) for j in range(4)]
    m2 = jnp.maximum(jnp.maximum(r[0], r[1]), jnp.maximum(r[2], r[3]))
    i2 = jnp.where((r[0] == m2) & (i1 != 0), 0,
                   jnp.where((r[1] == m2) & (i1 != 1), 1, jnp.where((r[2] == m2) & (i1 != 2), 2, 3)))
    return m1, m2, i1, i2


def _route_body(lg_ref, rb_ref, tri_ref, e_ref, w_ref, rank_ref, cnt_ref, carry_ref):
    i = pl.program_id(0)

    @pl.when(i == 0)
    def _():
        carry_ref[...] = jnp.zeros_like(carry_ref)

    scores = jax.nn.sigmoid(lg_ref[...])
    sel = scores + rb_ref[...]
    rows = [sel[e:e + 1] for e in range(N_EXPERTS)]
    tops = [_top2_of4(rows[4 * g:4 * g + 4]) for g in range(N_GROUPS)]
    gs = [t[0] + t[1] for t in tops]
    gmax = jnp.maximum(jnp.maximum(gs[0], gs[1]), jnp.maximum(gs[2], gs[3]))
    gi = jnp.where(gs[0] == gmax, 0, jnp.where(gs[1] == gmax, 1, jnp.where(gs[2] == gmax, 2, 3)))
    pick = lambda k: jnp.where(gi == 0, tops[0][k], jnp.where(gi == 1, tops[1][k], jnp.where(gi == 2, tops[2][k], tops[3][k])))
    e0 = gi * EXPERTS_PER_GROUP + pick(2)
    e1 = gi * EXPERTS_PER_GROUP + pick(3)
    eid = lax.broadcasted_iota(I32, scores.shape, 0)
    oh0 = eid == e0
    oh1 = eid == e1
    s0 = jnp.sum(jnp.where(oh0, scores, 0.0), axis=0, keepdims=True)
    s1 = jnp.sum(jnp.where(oh1, scores, 0.0), axis=0, keepdims=True)
    tot = s0 + s1
    e_ref[...] = jnp.concatenate([e0, e1], axis=0)
    w_ref[...] = jnp.concatenate([s0 / tot, s1 / tot], axis=0)
    oh = (oh0 | oh1).astype(BF16)
    incl = _dot(oh, tri_ref[...]) + carry_ref[...]
    excl = incl - 1.0
    r0 = jnp.sum(jnp.where(oh0, excl, 0.0), axis=0, keepdims=True)
    r1 = jnp.sum(jnp.where(oh1, excl, 0.0), axis=0, keepdims=True)
    rank_ref[...] = jnp.concatenate([r0, r1], axis=0).astype(I32)
    carry_ref[...] = incl[:, ROUTE_TILE - 1:ROUTE_TILE]
    cnt_ref[...] = jnp.broadcast_to(incl[:, ROUTE_TILE - 1:ROUTE_TILE], cnt_ref.shape).astype(I32)


def _route(logits_t, router_b):
    ne, t = logits_t.shape
    tl = ROUTE_TILE
    assert t % tl == 0
    tri = jnp.asarray(np.triu(np.ones((tl, tl), np.float32)), BF16)
    two = lambda dt: jax.ShapeDtypeStruct((2, t), dt)
    return pl.pallas_call(
        _route_body,
        grid=(t // tl,),
        in_specs=[pl.BlockSpec((ne, tl), lambda i: (0, i)), pl.BlockSpec((ne, 1), lambda i: (0, 0)),
                  pl.BlockSpec((tl, tl), lambda i: (0, 0))],
        out_specs=[pl.BlockSpec((2, tl), lambda i: (0, i))] * 3 + [pl.BlockSpec((ne, LANE), lambda i: (0, 0))],
        out_shape=[two(I32), two(F32), two(I32), jax.ShapeDtypeStruct((ne, LANE), I32)],
        scratch_shapes=[pltpu.VMEM((ne, 1), F32)],
        compiler_params=_cp(("arbitrary",)),
        name="moe_route",
    )(logits_t, router_b.reshape(ne, 1).astype(F32), tri)


def _gather_body(idx_ref, src_ref, out_ref, sem):
    i = pl.program_id(0)
    base = i * GATHER_CHUNK

    def copy(j):
        return pltpu.make_async_copy(src_ref.at[idx_ref[0, 0, j]], out_ref.at[base + j], sem)

    def issue(j, c):
        copy(j).start()
        return c

    def drain(j, c):
        copy(j).wait()
        return c

    lax.fori_loop(0, GATHER_CHUNK, issue, 0)
    lax.fori_loop(0, GATHER_CHUNK, drain, 0)


def _row_gather(src, idx):
    n_out = idx.shape[0]
    assert n_out % GATHER_CHUNK == 0
    n_steps = n_out // GATHER_CHUNK
    return pl.pallas_call(
        _gather_body,
        grid=(n_steps,),
        in_specs=[pl.BlockSpec((1, 1, GATHER_CHUNK), lambda i: (i, 0, 0), memory_space=pltpu.SMEM),
                  pl.BlockSpec(memory_space=pl.ANY)],
        out_specs=pl.BlockSpec(memory_space=pl.ANY),
        out_shape=jax.ShapeDtypeStruct((n_out,) + src.shape[1:], src.dtype),
        scratch_shapes=[pltpu.SemaphoreType.DMA(())],
        compiler_params=pltpu.CompilerParams(dimension_semantics=("arbitrary",), has_side_effects=True),
        name="row_gather",
    )(idx.reshape(n_steps, 1, GATHER_CHUNK), src)


def _ffn_body(te_ref, nu_ref, x_ref, ws_ref, wg_ref, wu_ref, wd_ref, o_ref):
    i = pl.program_id(0)
    d = wg_ref.shape[1]

    @pl.when(i < nu_ref[0])
    def _():
        x = jnp.concatenate([x_ref[:, c, :] for c in range(d // LANE)], axis=1).astype(BF16)
        hmid = (_silu(_dot(x, wg_ref[0])) * _dot(x, wu_ref[0])).astype(BF16)
        y = _dot(hmid, wd_ref[0]) * ws_ref[...]
        for c in range(d // LANE):
            o_ref[:, c, :] = y[:, c * LANE:(c + 1) * LANE]

    @pl.when(i >= nu_ref[0])
    def _():
        o_ref[...] = jnp.zeros_like(o_ref)


def _grouped_ffn(xs, w_slot, tile_expert, n_used, wg, wu, wd):
    s_pad, nc, _ = xs.shape
    tr = MOE_ROW_TILE
    n_tiles = s_pad // tr
    d = wg.shape[1]
    ff = wg.shape[2]
    grid_spec = pltpu.PrefetchScalarGridSpec(
        num_scalar_prefetch=2,
        grid=(n_tiles,),
        in_specs=[pl.BlockSpec((tr, nc, LANE), lambda i, te, nu: (i, 0, 0)),
                  pl.BlockSpec((tr, 1), lambda i, te, nu: (i, 0)),
                  pl.BlockSpec((1, d, ff), lambda i, te, nu: (te[i], 0, 0)),
                  pl.BlockSpec((1, d, ff), lambda i, te, nu: (te[i], 0, 0)),
                  pl.BlockSpec((1, ff, d), lambda i, te, nu: (te[i], 0, 0))],
        out_specs=pl.BlockSpec((tr, nc, LANE), lambda i, te, nu: (i, 0, 0)),
    )
    return pl.pallas_call(
        _ffn_body,
        grid_spec=grid_spec,
        out_shape=jax.ShapeDtypeStruct((s_pad, nc, LANE), F32),
        compiler_params=_cp(("arbitrary",)),
        name="moe_grouped_ffn",
    )(tile_expert, n_used, xs, w_slot, wg, wu, wd)


def _combine_body(x_ref, mod_ref, y0_ref, y1_ref, nw_ref, o_ref, *, final):
    d = x_ref.shape[2]
    y = jnp.concatenate([y0_ref[0, :, c, :] + y1_ref[0, :, c, :] for c in range(d // LANE)], axis=1)
    x = x_ref[0] + mod_ref[0][5:6] * y
    if final:
        x = x * lax.rsqrt(jnp.mean(x * x, axis=-1, keepdims=True) + RMS_EPS) * nw_ref[...]
    o_ref[0] = x


def _moe_combine(xc, mod, yg, final_w, lc, *, final):
    b, t, d = xc.shape
    tm = 256
    nb = b
    ctx_tiles = lc // tm
    nt = t // tm
    row0 = ctx_tiles if final else 0
    n_out = t - row0 * tm
    nc = d // LANE
    ysp = lambda k: pl.BlockSpec((1, tm, nc, LANE), lambda bi, ti: (k, bi * nt + ti + row0, 0, 0))
    return pl.pallas_call(
        functools.partial(_combine_body, final=final),
        grid=(b, nt - row0),
        in_specs=[pl.BlockSpec((1, tm, d), lambda bi, ti: (bi, ti + row0, 0)),
                  pl.BlockSpec((1, N_MOD, d), lambda bi, ti: (jnp.where(ti + row0 < ctx_tiles, nb, bi), 0, 0)),
                  ysp(0), ysp(1),
                  pl.BlockSpec((1, d), lambda bi, ti: (0, 0))],
        out_specs=pl.BlockSpec((1, tm, d), lambda bi, ti: (bi, ti, 0)),
        out_shape=jax.ShapeDtypeStruct((b, n_out, d), F32),
        compiler_params=_cp(("arbitrary", "arbitrary")),
        name="moe_combine",
    )(xc, mod, yg, yg, final_w.reshape(1, d))


def _moe(h3, logits_t, router_b, wg, wu, wd):
    t = h3.shape[0]
    tr = MOE_ROW_TILE
    e_idx, w_tok, rank, counts = _route(logits_t, router_b)
    counts = counts[:, 0]
    padded = ((counts + tr - 1) // tr) * tr
    ends = jnp.cumsum(padded)
    starts = ends - padded
    slots = (starts[e_idx] + rank).reshape(-1)
    s_pad = 2 * t + N_EXPERTS * tr
    s_pad = -(-s_pad // GATHER_CHUNK) * GATHER_CHUNK
    tok = jnp.tile(jnp.arange(t, dtype=I32), 2)
    tok_of_slot = jnp.zeros((s_pad,), I32).at[slots].set(tok)
    w_slot = jnp.zeros((s_pad,), F32).at[slots].set(w_tok.reshape(-1))
    n_tiles = s_pad // tr
    tile_start = jnp.arange(n_tiles, dtype=I32) * tr
    n_used = (ends[-1] // tr).astype(I32)
    tile_expert = jnp.minimum(jnp.searchsorted(ends, tile_start, side="right"), N_EXPERTS - 1).astype(I32)
    tile_expert = jnp.where(jnp.arange(n_tiles) < n_used, tile_expert, tile_expert[jnp.maximum(n_used - 1, 0)])
    xs = _row_gather(h3, tok_of_slot)
    ys = _grouped_ffn(xs, w_slot.reshape(s_pad, 1), tile_expert, n_used.reshape(1), wg, wu, wd)
    yg = _row_gather(ys, slots)
    return yg.reshape((2, t) + h3.shape[1:])


def kernel(x, c, ctx, c_ctx, ada_w, ada_b, norm1_w, norm2_w, w_in, b_gate, na_rpb, q_norm_w, k_norm_w, hy_conv_w, hy_conv_b,
           hy_f_w1, hy_f_b1, hy_f_w2, hy_f_b2, hy_f_w3, hy_sin_freq, hy_bias_d, w_br_a, w_br_b, w_br_c, w_out, router_w,
           router_b, exp_w_gate, exp_w_up, exp_w_down, final_norm_w):
    b, n, d = x.shape
    lc = ctx.shape[1]
    depth = ada_w.shape[0]
    t = lc + n
    cos_t, sin_t = _rope_tables(lc, n)
    xc = jnp.concatenate([ctx, x], axis=1)
    cvec = jnp.concatenate([c, c_ctx[None], jnp.zeros((SUBLANE - b - 1, d), F32)], axis=0)
    router_wt = router_w.T
    ff = exp_w_gate.shape[-1]
    pad_ff = lambda w, axis: jnp.pad(w, [(0, FF_PAD - ff) if a == axis else (0, 0) for a in range(w.ndim)]).astype(BF16)
    na_w = NA_HEADS * HEAD_DIM
    out = None
    for layer in range(depth):
        mod = _ada_mod(cvec, ada_w[layer], ada_b[layer])[:b + 1].reshape(b + 1, N_MOD, d)
        aq, ak, av, bq, bk, bv, cu, g = _norm_inproj(xc, mod, norm1_w[layer], w_in[layer].astype(BF16), b_gate[layer],
                                                     q_norm_w[layer], k_norm_w[layer], cos_t, sin_t, lc)
        filt_params = (hy_f_w1[layer], hy_f_b1[layer], hy_f_w2[layer], hy_f_b2[layer], hy_f_w3[layer], hy_sin_freq[layer])
        ya_lat = _na_attention(aq, ak, av, _na_bias_table(na_rpb[layer]), lc)
        ya_ctx = _pair_attention(aq, ak[:, :lc], av[:, :lc], q_row0=0, n_q=lc, kv_of_pair=lambda p: p, n_pairs=na_w // LANE)
        gq = GQA_Q_HEADS * HEAD_DIM // LANE
        yb_lat = _pair_attention(bq, bk, bv, q_row0=lc, n_q=n, kv_of_pair=lambda p: p // 2, n_pairs=gq)
        yb_ctx = _pair_attention(bq, bk[:, :lc], bv[:, :lc], q_row0=0, n_q=lc, kv_of_pair=lambda p: p // 2, n_pairs=gq)
        yc_lat = _hyena_long(cu[:, lc:], hy_conv_w[layer], hy_conv_b[layer], filt_params, hy_bias_d[layer])
        yc_ctx = _hyena_small(cu[:, :lc], hy_conv_w[layer], hy_conv_b[layer], filt_params, hy_bias_d[layer])
        ya = jnp.concatenate([ya_ctx, ya_lat], axis=1)
        yb = jnp.concatenate([yb_ctx, yb_lat], axis=1)
        yc = jnp.concatenate([yc_ctx, yc_lat], axis=1)
        xc, h3, logits_t = _merge(xc, mod, ya, yb, yc, g, w_br_a[layer].astype(BF16), w_br_b[layer].astype(BF16),
                                  w_br_c[layer].astype(BF16), w_out[layer].astype(BF16), norm2_w[layer], router_wt, lc)
        yg = _moe(h3, logits_t, router_b, pad_ff(exp_w_gate[layer], 2), pad_ff(exp_w_up[layer], 2), pad_ff(exp_w_down[layer], 1))
        final = layer == depth - 1
        res = _moe_combine(xc, mod, yg, final_norm_w, lc, final=final)
        if final:
            out = res
        else:
            xc = res
    return out
```

```python
import functools
import math

import jax
import jax.numpy as jnp
import numpy as np
from jax import lax
from jax.experimental import pallas as pl
from jax.experimental.pallas import tpu as pltpu

F32 = jnp.float32
BF16 = jnp.bfloat16
I32 = jnp.int32
HI = lax.Precision.HIGHEST

GRID_W = 64
HEAD_DIM = 64
RMS_EPS = 1e-6
N_MOD = 6
NA_HEADS = 4
NA_WIN_R = 8
NA_WIN_C = 16
GQA_Q_HEADS = 8
GQA_KV_HEADS = 2
ROPE_THETA = 10000.0
HY_WIDTH = 256
HY_ORDER = 2
HY_EMB_DIM = 33
HY_DECAY_TARGET = 1e-2
HY_FAST_DECAY_PCT = 0.3
HY_SLOW_DECAY_PCT = 1.5
N_BRANCH = 3
N_EXPERTS = 16
N_GROUPS = 4
EXPERTS_PER_GROUP = 4
LANE = 128
SUBLANE = 8
DFT_N2 = 128
LOG2E = math.log2(math.e)
VMEM_LIMIT = 56 * 1024 * 1024


def _cp(sem, vmem=VMEM_LIMIT, flags=None):
    return pltpu.CompilerParams(dimension_semantics=sem, vmem_limit_bytes=vmem, flags=flags)


def _dot(a, b, prec=None):
    return jnp.dot(a, b, preferred_element_type=F32, precision=prec)


def _dot_nt(a, b, prec=None):
    return lax.dot_general(a, b, (((1,), (1,)), ((), ())), preferred_element_type=F32, precision=prec)


def _silu(x):
    return x * jax.nn.sigmoid(x)


def _ada_body(c_ref, w_ref, b_ref, o_ref):
    o_ref[...] = _dot(_silu(c_ref[...]), w_ref[...], HI) + b_ref[...]


def _ada_mod(cvec, ada_w, ada_b):
    rows, d = cvec.shape
    cols = ada_w.shape[1]
    tn = 1536
    return pl.pallas_call(
        _ada_body,
        grid=(cols // tn,),
        in_specs=[pl.BlockSpec((rows, d), lambda j: (0, 0)),
                  pl.BlockSpec((d, tn), lambda j: (0, j)),
                  pl.BlockSpec((1, tn), lambda j: (0, j))],
        out_specs=pl.BlockSpec((rows, tn), lambda j: (0, j)),
        out_shape=jax.ShapeDtypeStruct((rows, cols), F32),
        compiler_params=_cp(("arbitrary",)),
        name="ada_mod",
    )(cvec, ada_w, ada_b.reshape(1, cols))


def _head_rms(x, ones_bd, w_row):
    x2 = x * x
    hi = x2.astype(BF16)
    lo = (x2 - hi.astype(F32)).astype(BF16)
    ss = _dot(hi, ones_bd) + _dot(lo, ones_bd)
    return x * lax.rsqrt(ss * (1.0 / HEAD_DIM) + RMS_EPS) * w_row


def _rope(x, cos_t, sin_t):
    n = x.shape[1]
    half = HEAD_DIM // 2
    lane = lax.broadcasted_iota(I32, x.shape, 1)
    first = (lane % HEAD_DIM) < half
    swapped = jnp.where(first, pltpu.roll(x, n - half, 1), pltpu.roll(x, half, 1))
    return x * cos_t + swapped * sin_t


def _inproj_body(x_ref, mod_ref, nw_ref, w_ref, bg_ref, qw_ref, kw_ref, cos_ref, sin_ref, bdq_ref, bdk_ref, dup_ref, dupv_ref,
                 aq_ref, ak_ref, av_ref, bq_ref, bk_ref, bv_ref, cu_ref, g_ref):
    x = x_ref[0]
    mod = mod_ref[0]
    y = x * lax.rsqrt(jnp.mean(x * x, axis=-1, keepdims=True) + RMS_EPS) * nw_ref[...]
    h = (y * (1.0 + mod[1:2]) + mod[0:1]).astype(BF16)
    scale = HEAD_DIM ** -0.5 * LOG2E
    na = NA_HEADS * HEAD_DIM
    qg = GQA_Q_HEADS * HEAD_DIM
    kg = GQA_KV_HEADS * HEAD_DIM
    o = 0
    pa = _dot(h, w_ref[:, o:o + 3 * na])
    aq_ref[0] = (pa[:, :na] * scale).astype(BF16)
    ak_ref[0] = pa[:, na:2 * na].astype(BF16)
    av_ref[0] = pa[:, 2 * na:].astype(BF16)
    o += 3 * na
    cos2 = cos_ref[...]
    sin2 = sin_ref[...]
    pq = _dot(h, w_ref[:, o:o + qg])
    qn = _head_rms(pq, bdq_ref[...], qw_ref[...])
    reps = qg // LANE
    qr = _rope(qn, jnp.concatenate([cos2] * reps, axis=1), jnp.concatenate([sin2] * reps, axis=1))
    bq_ref[0] = (qr * scale).astype(BF16)
    o += qg
    pkv = _dot(h, w_ref[:, o:o + 2 * kg])
    kn = _head_rms(pkv[:, :kg], bdk_ref[...], kw_ref[...])
    kr = _rope(kn, cos2, sin2).astype(BF16)
    bk_ref[0] = _dot(kr, dup_ref[...]).astype(BF16)
    lane_v = lax.broadcasted_iota(I32, (1, 2 * kg), 1)
    ones_half = ((lane_v % LANE) >= HEAD_DIM).astype(F32)
    bv_ref[0] = (_dot(pkv[:, kg:].astype(BF16), dupv_ref[...]) + ones_half).astype(BF16)
    o += 2 * kg
    cu_ref[0] = _dot(h, w_ref[:, o:o + 3 * HY_WIDTH])
    o += 3 * HY_WIDTH
    g_ref[0] = jax.nn.sigmoid(_dot(h, w_ref[:, o:]) + bg_ref[...]).astype(BF16)


def _norm_inproj(xc, mod, norm_w, w_in_bf, b_gate, q_norm_w, k_norm_w, cos_t, sin_t, lc):
    b, t, d = xc.shape
    tm = 256
    na = NA_HEADS * HEAD_DIM
    qg = GQA_Q_HEADS * HEAD_DIM
    kg = GQA_KV_HEADS * HEAD_DIM
    ng = N_BRANCH * d
    nb = b
    bdq = jnp.asarray(np.kron(np.eye(GQA_Q_HEADS), np.ones((HEAD_DIM, HEAD_DIM))), BF16)
    bdk = jnp.asarray(np.kron(np.eye(GQA_KV_HEADS), np.ones((HEAD_DIM, HEAD_DIM))), BF16)
    dup_np = np.zeros((kg, 2 * kg), np.float32)
    dupv_np = np.zeros((kg, 2 * kg), np.float32)
    for g in range(GQA_KV_HEADS):
        for r in range(2):
            dup_np[g * HEAD_DIM:(g + 1) * HEAD_DIM, (2 * g + r) * HEAD_DIM:(2 * g + r + 1) * HEAD_DIM] = np.eye(HEAD_DIM)
        dupv_np[g * HEAD_DIM:(g + 1) * HEAD_DIM, 2 * g * HEAD_DIM:(2 * g + 1) * HEAD_DIM] = np.eye(HEAD_DIM)
    dup = jnp.asarray(dup_np, BF16)
    dupv = jnp.asarray(dupv_np, BF16)
    qw = jnp.tile(q_norm_w.astype(F32), GQA_Q_HEADS).reshape(1, qg)
    kw = jnp.tile(k_norm_w.astype(F32), GQA_KV_HEADS).reshape(1, kg)
    ctx_tiles = lc // tm
    full = lambda shape: pl.BlockSpec(shape, lambda bi, ti: (0,) * len(shape))
    tok = lambda width: pl.BlockSpec((1, tm, width), lambda bi, ti: (bi, ti, 0))
    outs = [(na, BF16), (na, BF16), (na, BF16), (qg, BF16), (2 * kg, BF16), (2 * kg, BF16), (3 * HY_WIDTH, F32), (ng, BF16)]
    return pl.pallas_call(
        _inproj_body,
        grid=(b, t // tm),
        in_specs=[tok(d),
                  pl.BlockSpec((1, N_MOD, d), lambda bi, ti: (jnp.where(ti < ctx_tiles, nb, bi), 0, 0)),
                  full((1, d)), full(w_in_bf.shape), full((1, ng)), full((1, qg)), full((1, kg)),
                  pl.BlockSpec((tm, LANE), lambda bi, ti: (ti, 0)),
                  pl.BlockSpec((tm, LANE), lambda bi, ti: (ti, 0)),
                  full(bdq.shape), full(bdk.shape), full(dup.shape), full(dupv.shape)],
        out_specs=[tok(w) for w, _ in outs],
        out_shape=[jax.ShapeDtypeStruct((b, t, w), dt) for w, dt in outs],
        compiler_params=_cp(("arbitrary", "arbitrary")),
        name="norm_inproj",
    )(xc, mod, norm_w.reshape(1, d), w_in_bf, b_gate.reshape(1, ng), qw, kw, cos_t, sin_t, bdq, bdk, dup, dupv)


def _rope_tables(lc, n):
    tpos = jnp.arange(n, dtype=I32)
    rows = (tpos // GRID_W).astype(F32)
    cols = (tpos % GRID_W).astype(F32)
    d_axis = HEAD_DIM // 2
    inv = ROPE_THETA ** (-jnp.arange(0, d_axis, 2, dtype=F32) / d_axis)
    ang = jnp.concatenate([rows[:, None] * inv, cols[:, None] * inv], axis=-1)
    cos, sin = jnp.cos(ang), jnp.sin(ang)
    cos_h = jnp.concatenate([cos, cos], axis=-1)
    sin_h = jnp.concatenate([-sin, sin], axis=-1)
    cos_h = jnp.concatenate([jnp.ones((lc, HEAD_DIM), F32), cos_h], axis=0)
    sin_h = jnp.concatenate([jnp.zeros((lc, HEAD_DIM), F32), sin_h], axis=0)
    return jnp.concatenate([cos_h, cos_h], axis=-1), jnp.concatenate([sin_h, sin_h], axis=-1)


NA_ROWS_PER_STEP = 4


def _na_bias_table(rpb):
    qc = np.arange(GRID_W)[:, None]
    kc = np.arange(GRID_W)[None, :]
    win_c0 = np.clip(qc - NA_WIN_C // 2, 0, GRID_W - NA_WIN_C)
    col_ok = (kc >= win_c0) & (kc < win_c0 + NA_WIN_C)
    d_col = np.clip(kc - qc + NA_WIN_C - 1, 0, 2 * NA_WIN_C - 2)
    base = np.arange(NA_WIN_R)[:, None] + np.arange(NA_WIN_R)[None, :]
    tab = rpb.astype(F32)[:, base][:, :, :, d_col]
    tab = jnp.where(jnp.asarray(col_ok)[None, None, None], tab, -1e30) * LOG2E
    tab = tab.transpose(1, 0, 3, 2, 4)
    return tab.reshape(NA_WIN_R, NA_HEADS * GRID_W, NA_WIN_R * GRID_W)


def _na_body(q_ref, k_ref, v_ref, bias_ref, o_ref, *, lc, n_rows):
    i = pl.program_id(1)
    nq = NA_HEADS * GRID_W
    lane_q = lax.broadcasted_iota(I32, (GRID_W, NA_HEADS * HEAD_DIM), 1) // HEAD_DIM
    lane_o = lane_q
    kctx = k_ref[0, 0:lc, :]
    vctx = v_ref[0, 0:lc, :]
    for j in range(NA_ROWS_PER_STEP):
        r = i * NA_ROWS_PER_STEP + j
        kr0 = jnp.clip(r - NA_WIN_R // 2, 0, n_rows - NA_WIN_R)
        variant = kr0 - r + NA_WIN_R - 1
        kstart = pl.multiple_of(lc + kr0 * GRID_W, GRID_W)
        kwin = k_ref[0, pl.ds(kstart, NA_WIN_R * GRID_W), :]
        vwin = v_ref[0, pl.ds(kstart, NA_WIN_R * GRID_W), :]
        q = q_ref[0, j * GRID_W:(j + 1) * GRID_W, :]
        qm = jnp.concatenate([jnp.where(lane_q == h, q, jnp.zeros_like(q)) for h in range(NA_HEADS)], axis=0)
        s_loc = _dot_nt(qm, kwin) + bias_ref[variant]
        s_ctx = _dot_nt(qm, kctx)
        m = jnp.maximum(jnp.max(s_loc, axis=-1, keepdims=True), jnp.max(s_ctx, axis=-1, keepdims=True))
        p_loc = jnp.exp2(s_loc - m)
        p_ctx = jnp.exp2(s_ctx - m)
        l = jnp.sum(p_loc, axis=-1, keepdims=True) + jnp.sum(p_ctx, axis=-1, keepdims=True)
        o = (_dot(p_loc.astype(BF16), vwin) + _dot(p_ctx.astype(BF16), vctx)) / l
        out = jnp.zeros((GRID_W, NA_HEADS * HEAD_DIM), F32)
        for h in range(NA_HEADS):
            out = jnp.where(lane_o == h, o[h * GRID_W:(h + 1) * GRID_W], out)
        o_ref[0, j * GRID_W:(j + 1) * GRID_W, :] = out.astype(o_ref.dtype)
    del nq


def _na_attention(aq, ak, av, bias_tab, lc):
    b, t, w = aq.shape
    n = t - lc
    n_rows = n // GRID_W
    assert n % GRID_W == 0 and n_rows >= NA_WIN_R and n_rows % NA_ROWS_PER_STEP == 0
    tq = NA_ROWS_PER_STEP * GRID_W
    assert lc % tq == 0
    off = lc // tq
    return pl.pallas_call(
        functools.partial(_na_body, lc=lc, n_rows=n_rows),
        grid=(b, n_rows // NA_ROWS_PER_STEP),
        in_specs=[pl.BlockSpec((1, tq, w), lambda bi, i: (bi, i + off, 0)),
                  pl.BlockSpec((1, t, w), lambda bi, i: (bi, 0, 0)),
                  pl.BlockSpec((1, t, w), lambda bi, i: (bi, 0, 0)),
                  pl.BlockSpec(bias_tab.shape, lambda bi, i: (0, 0, 0))],
        out_specs=pl.BlockSpec((1, tq, w), lambda bi, i: (bi, i, 0)),
        out_shape=jax.ShapeDtypeStruct((b, n, w), BF16),
        compiler_params=_cp(("arbitrary", "arbitrary")),
        name="na_attention",
    )(aq, ak, av, bias_tab)


PAIR_STREAMS = 4


def _pair_attn_body(q_ref, k_ref, v_ref, o_ref, *, kc, shared_kv):
    tq = q_ref.shape[1]
    tk = k_ref.shape[1]
    hs = tq // PAIR_STREAMS
    lane = lax.broadcasted_iota(I32, (hs, LANE), 1)
    lo = lane < HEAD_DIM
    qqs = []
    for si in range(PAIR_STREAMS):
        q = q_ref[0, si * hs:(si + 1) * hs, :]
        zero = jnp.zeros_like(q)
        qqs.append(jnp.concatenate([jnp.where(lo, q, zero), jnp.where(lo, zero, q)], axis=0))
    n_chunks = tk // kc

    def step(c, carry):
        start = 0 if n_chunks == 1 else pl.multiple_of(c * kc, kc)
        kk = k_ref[0, pl.ds(start, kc), :]
        vv = v_ref[0, pl.ds(start, kc), :]
        out = []
        for qq, (m, l, acc) in zip(qqs, carry):
            s = _dot_nt(qq, kk)
            m_new = jnp.maximum(m, jnp.max(s, axis=-1, keepdims=True))
            alpha = jnp.exp2(m - m_new)
            p = jnp.exp2(s - m_new)
            if not shared_kv:
                l = alpha * l + jnp.sum(p, axis=-1, keepdims=True)
            acc = alpha * acc + _dot(p.astype(BF16), vv)
            out.append((m_new, l, acc))
        return tuple(out)

    one = (jnp.full((2 * hs, 1), -jnp.inf, F32), jnp.zeros((2 * hs, 1), F32), jnp.zeros((2 * hs, LANE), F32))
    init = (one,) * PAIR_STREAMS
    res = step(0, init) if n_chunks == 1 else lax.fori_loop(0, n_chunks, step, init, unroll=True)
    for si, (m, l, acc) in enumerate(res):
        a, bb = acc[:hs], acc[hs:]
        if shared_kv:
            oa = a / a[:, HEAD_DIM:HEAD_DIM + 1]
            ob = pltpu.roll(bb / bb[:, HEAD_DIM:HEAD_DIM + 1], HEAD_DIM, 1)
        else:
            oa = a / l[:hs]
            ob = bb / l[hs:]
        o_ref[0, si * hs:(si + 1) * hs, :] = jnp.where(lo, oa, ob).astype(o_ref.dtype)


def _pair_attention(q, k, v, *, q_row0, n_q, kv_of_pair, n_pairs, shared_kv):
    b = q.shape[0]
    tk = k.shape[1]
    tq = 512 if (q_row0 % 512 == 0 and n_q % 512 == 0) else 256
    assert q_row0 % tq == 0 and n_q % tq == 0
    kc = 768 if tk % 768 == 0 else tk
    off = q_row0 // tq
    return pl.pallas_call(
        functools.partial(_pair_attn_body, kc=kc, shared_kv=shared_kv),
        grid=(b, n_pairs, n_q // tq),
        in_specs=[pl.BlockSpec((1, tq, LANE), lambda bi, p, i: (bi, i + off, p)),
                  pl.BlockSpec((1, tk, LANE), lambda bi, p, i: (bi, 0, kv_of_pair(p))),
                  pl.BlockSpec((1, tk, LANE), lambda bi, p, i: (bi, 0, kv_of_pair(p)))],
        out_specs=pl.BlockSpec((1, tq, LANE), lambda bi, p, i: (bi, i, p)),
        out_shape=jax.ShapeDtypeStruct((b, n_q, n_pairs * LANE), BF16),
        compiler_params=_cp(("arbitrary", "arbitrary", "arbitrary")),
        name="pair_attention",
    )(q, k, v)


def _filter_body(w1t_ref, w1c_ref, w1s_ref, b1_ref, w2_ref, b2_ref, w3_ref, fr_ref, o_ref, *, length, rows, transposed):
    g = pl.program_id(0)
    n_total = 2 * length
    bands = (HY_EMB_DIM - 1) // 2
    f_row = 1e-4 + lax.broadcasted_iota(I32, (1, bands), 1).astype(F32) * ((bands - 1 - 1e-4) / (bands - 1))
    nch = HY_ORDER * HY_WIDTH
    d_lo = math.log(HY_DECAY_TARGET) / HY_SLOW_DECAY_PCT
    d_hi = math.log(HY_DECAY_TARGET) / HY_FAST_DECAY_PCT
    deltas = d_lo + lax.broadcasted_iota(I32, (1, nch), 1).astype(F32) * ((d_hi - d_lo) / (nch - 1))
    freq = fr_ref[...]
    sub = DFT_N2 if transposed else rows
    for j in range(rows // sub):
        tt = g * rows + j * sub + lax.broadcasted_iota(I32, (sub, 1), 0)
        pos = jnp.where(tt < length, tt, n_total - tt).astype(F32)
        tn = pos * (1.0 / (length - 1))
        w = pos * (2.0 * math.pi / length)
        arg = w * f_row
        pre = tn * w1t_ref[...] + _dot(jnp.cos(arg), w1c_ref[...], HI) - _dot(jnp.sin(arg), w1s_ref[...], HI)
        h1 = jnp.sin(freq * (pre + b1_ref[...]))
        h2 = jnp.sin(freq * (_dot(h1, w2_ref[...], HI) + b2_ref[...]))
        filt = _dot(h2, w3_ref[0], HI) * jnp.exp(-tn * jnp.abs(deltas))
        filt = jnp.where(tt == length, 0.0, filt)
        if transposed:
            o_ref[:, j, :] = filt
        else:
            o_ref[...] = filt


def _hyena_filter(length, w1, b1, w2, b2, w3, freq, *, transposed):
    hid = w1.shape[1]
    nch = HY_ORDER * HY_WIDTH
    n_total = 2 * length
    bands = (HY_EMB_DIM - 1) // 2
    rows = SUBLANE * DFT_N2 if transposed else length
    n_steps = n_total // rows
    assert n_total % rows == 0 and n_steps % 2 == 0
    w3d = w3.reshape(hid, 2, nch).transpose(1, 0, 2)
    full = lambda shape: pl.BlockSpec(shape, lambda g: (0,) * len(shape))
    if transposed:
        out_spec = pl.BlockSpec((DFT_N2, SUBLANE, nch), lambda g: (0, g, 0))
        out_shape = jax.ShapeDtypeStruct((DFT_N2, n_total // DFT_N2, nch), F32)
    else:
        out_spec = pl.BlockSpec((rows, nch), lambda g: (g, 0))
        out_shape = jax.ShapeDtypeStruct((n_total, nch), F32)
    return pl.pallas_call(
        functools.partial(_filter_body, length=length, rows=rows, transposed=transposed),
        grid=(n_steps,),
        in_specs=[full((1, hid)), full((bands, hid)), full((bands, hid)), full((1, hid)), full((hid, hid)), full((1, hid)),
                  pl.BlockSpec((1, hid, nch), lambda g: (g // (n_steps // 2), 0, 0)), full((1, hid))],
        out_specs=out_spec,
        out_shape=out_shape,
        compiler_params=_cp(("arbitrary",)),
        name="hyena_filter",
    )(w1[0:1], w1[1:1 + bands], w1[1 + bands:], b1.reshape(1, hid), w2, b2.reshape(1, hid), w3d, freq.reshape(1, hid))


def _dft_tables(n1, n1_used):
    n = n1 * DFT_N2
    n2 = np.arange(DFT_N2)[:, None, None]
    k1 = np.arange(n1)[None, :, None]
    nn1 = np.arange(n1_used)[None, None, :]
    ang = 2.0 * np.pi * ((k1 * (DFT_N2 * nn1 + n2)) % n) / n
    g_fwd = np.concatenate([np.cos(ang), -np.sin(ang)], axis=1)
    g_inv = np.transpose(g_fwd, (0, 2, 1)) / n
    kk = np.arange(DFT_N2)
    a2 = 2.0 * np.pi * ((kk[:, None] * kk[None, :]) % DFT_N2) / DFT_N2
    fr, fi = np.cos(a2), -np.sin(a2)
    f2 = np.block([[fr, -fi], [fi, fr]])
    f2c = np.block([[fr, fi], [-fi, fr]])
    cast = lambda a: jnp.asarray(a.astype(np.float32)).astype(BF16)
    return cast(g_fwd), cast(g_inv), cast(f2), cast(f2c)


def _stage1_body(x_ref, g_ref, o_ref, *, n1):
    for j in range(SUBLANE):
        res = _dot(g_ref[j], x_ref[0, j].astype(BF16))
        o_ref[0, 0, :, j, :] = res[:n1]
        o_ref[0, 1, :, j, :] = res[n1:]


def _dft_stage1(x_t, g_fwd):
    bx, _, n1u, c = x_t.shape
    n1 = g_fwd.shape[1] // 2
    return pl.pallas_call(
        functools.partial(_stage1_body, n1=n1),
        grid=(bx, DFT_N2 // SUBLANE),
        in_specs=[pl.BlockSpec((1, SUBLANE, n1u, c), lambda bi, g: (bi, g, 0, 0)),
                  pl.BlockSpec((SUBLANE, 2 * n1, n1u), lambda bi, g: (g, 0, 0))],
        out_specs=pl.BlockSpec((1, 2, n1, SUBLANE, c), lambda bi, g: (bi, 0, 0, g, 0)),
        out_shape=jax.ShapeDtypeStruct((bx, 2, n1, DFT_N2, c), F32),
        compiler_params=_cp(("arbitrary", "arbitrary")),
        name="dft_stage1",
    )(x_t, g_fwd)


def _stage2_spec_body(a_ref, f2_ref, o_ref):
    for j in range(SUBLANE):
        slab = jnp.concatenate([a_ref[0, 0, j], a_ref[0, 1, j]], axis=0).astype(BF16)
        o_ref[j] = _dot(f2_ref[...], slab)


def _dft_stage2_spectrum(a, f2):
    _, _, n1, _, c = a.shape
    return pl.pallas_call(
        _stage2_spec_body,
        grid=(n1 // SUBLANE,),
        in_specs=[pl.BlockSpec((1, 2, SUBLANE, DFT_N2, c), lambda g: (0, 0, g, 0, 0)),
                  pl.BlockSpec(f2.shape, lambda g: (0, 0))],
        out_specs=pl.BlockSpec((SUBLANE, 2 * DFT_N2, c), lambda g: (g, 0, 0)),
        out_shape=jax.ShapeDtypeStruct((n1, 2 * DFT_N2, c), F32),
        compiler_params=_cp(("arbitrary",)),
        name="dft_stage2_spectrum",
    )(a, f2)


def _stage2_conv_body(a_ref, k_ref, f2_ref, f2c_ref, o_ref):
    h = DFT_N2
    for j in range(SUBLANE):
        slab = jnp.concatenate([a_ref[0, 0, j], a_ref[0, 1, j]], axis=0).astype(BF16)
        x = _dot(f2_ref[...], slab)
        kf = k_ref[j]
        xr, xi, kr, ki = x[:h], x[h:], kf[:h], kf[h:]
        y = jnp.concatenate([xr * kr - xi * ki, xr * ki + xi * kr], axis=0).astype(BF16)
        bv = _dot(f2c_ref[...], y)
        o_ref[0, 0, :, j, :] = bv[:h]
        o_ref[0, 1, :, j, :] = bv[h:]


def _dft_stage2_conv(a, kspec, order, f2, f2c):
    bx, _, n1, _, c = a.shape
    return pl.pallas_call(
        _stage2_conv_body,
        grid=(bx, n1 // SUBLANE),
        in_specs=[pl.BlockSpec((1, 2, SUBLANE, DFT_N2, c), lambda bi, g: (bi, 0, g, 0, 0)),
                  pl.BlockSpec((SUBLANE, 2 * DFT_N2, c), lambda bi, g: (g, 0, order)),
                  pl.BlockSpec(f2.shape, lambda bi, g: (0, 0)),
                  pl.BlockSpec(f2c.shape, lambda bi, g: (0, 0))],
        out_specs=pl.BlockSpec((1, 2, DFT_N2, SUBLANE, c), lambda bi, g: (bi, 0, 0, g, 0)),
        out_shape=jax.ShapeDtypeStruct((bx, 2, DFT_N2, n1, c), F32),
        compiler_params=_cp(("arbitrary", "arbitrary")),
        name="dft_stage2_conv",
    )(a, kspec, f2, f2c)


def _inverse_body(b_ref, g_ref, u_ref, gate_ref, bias_ref, o_ref, *, token_order):
    for j in range(SUBLANE):
        slab = jnp.concatenate([b_ref[0, 0, j], b_ref[0, 1, j]], axis=0).astype(BF16)
        y = _dot(g_ref[j], slab)
        out = gate_ref[0, j] * (y + u_ref[0, j] * bias_ref[...])
        if token_order:
            o_ref[0, :, j, :] = out
        else:
            o_ref[0, j] = out


def _dft_inverse(bv, g_inv, u_t, gate_t, bias, *, token_order):
    bx, _, _, n1, c = bv.shape
    n1u = g_inv.shape[1]
    t_spec = pl.BlockSpec((1, SUBLANE, n1u, c), lambda bi, g: (bi, g, 0, 0))
    if token_order:
        out_spec = pl.BlockSpec((1, n1u, SUBLANE, c), lambda bi, g: (bi, 0, g, 0))
        out_shape = jax.ShapeDtypeStruct((bx, n1u, DFT_N2, c), F32)
    else:
        out_spec = t_spec
        out_shape = jax.ShapeDtypeStruct((bx, DFT_N2, n1u, c), F32)
    return pl.pallas_call(
        functools.partial(_inverse_body, token_order=token_order),
        grid=(bx, DFT_N2 // SUBLANE),
        in_specs=[pl.BlockSpec((1, 2, SUBLANE, n1, c), lambda bi, g: (bi, 0, g, 0, 0)),
                  pl.BlockSpec((SUBLANE, n1u, 2 * n1), lambda bi, g: (g, 0, 0)),
                  t_spec, t_spec,
                  pl.BlockSpec((1, c), lambda bi, g: (0, 0))],
        out_specs=out_spec,
        out_shape=out_shape,
        compiler_params=_cp(("arbitrary", "arbitrary")),
        name="dft_inverse",
    )(bv, g_inv, u_t, gate_t, bias.reshape(1, c))


def _shortconv_body(prev_ref, cur_ref, next_ref, w_ref, b_ref, v_ref, x1_ref, x2_ref, *, n_steps):
    i = pl.program_id(1)
    u = cur_ref[0]
    rows = u.shape[0]
    prev_row = jnp.where(i > 0, prev_ref[0, SUBLANE - 1:SUBLANE, :], 0.0)
    next_row = jnp.where(i < n_steps - 1, next_ref[0, 0:1, :], 0.0)
    ridx = lax.broadcasted_iota(I32, u.shape, 0)
    up = jnp.where(ridx == 0, prev_row, pltpu.roll(u, 1, 0))
    dn = jnp.where(ridx == rows - 1, next_row, pltpu.roll(u, rows - 1, 0))
    z = up * w_ref[0:1] + u * w_ref[1:2] + dn * w_ref[2:3] + b_ref[...]
    cw = HY_WIDTH
    for j in range(rows // DFT_N2):
        zj = z[j * DFT_N2:(j + 1) * DFT_N2]
        v_ref[0, :, j, :] = zj[:, :cw]
        x1_ref[0, :, j, :] = zj[:, cw:2 * cw]
        x2_ref[0, :, j, :] = zj[:, 2 * cw:]


def _short_conv_t(cu, conv_w, conv_b):
    b, length, c3 = cu.shape
    rows = SUBLANE * DFT_N2
    assert length % rows == 0
    n_steps = length // rows
    rb = rows // SUBLANE
    n1u = length // DFT_N2
    out_spec = pl.BlockSpec((1, DFT_N2, SUBLANE, HY_WIDTH), lambda bi, i: (bi, 0, i, 0))
    out_shape = jax.ShapeDtypeStruct((b, DFT_N2, n1u, HY_WIDTH), F32)
    return pl.pallas_call(
        functools.partial(_shortconv_body, n_steps=n_steps),
        grid=(b, n_steps),
        in_specs=[pl.BlockSpec((1, SUBLANE, c3), lambda bi, i: (bi, jnp.maximum(i * rb - 1, 0), 0)),
                  pl.BlockSpec((1, rows, c3), lambda bi, i: (bi, i, 0)),
                  pl.BlockSpec((1, SUBLANE, c3), lambda bi, i: (bi, jnp.minimum((i + 1) * rb, n_steps * rb - 1), 0)),
                  pl.BlockSpec((3, c3), lambda bi, i: (0, 0)),
                  pl.BlockSpec((1, c3), lambda bi, i: (0, 0))],
        out_specs=[out_spec] * 3,
        out_shape=[out_shape] * 3,
        compiler_params=_cp(("arbitrary", "arbitrary")),
        name="hyena_short_conv",
    )(cu, cu, cu, conv_w, conv_b.reshape(1, c3))


def _hyena_long(cu_lat, conv_w, conv_b, filt_params, bias_d):
    b, length, _ = cu_lat.shape
    n1u = length // DFT_N2
    n1 = 2 * n1u
    g_sig, g_inv, f2, f2c = _dft_tables(n1, n1u)
    g_full = _dft_tables(n1, n1)[0]
    filt_t = _hyena_filter(length, *filt_params, transposed=True)
    kspec = _dft_stage2_spectrum(_dft_stage1(filt_t[None], g_full), f2)
    v_t, x1_t, x2_t = _short_conv_t(cu_lat, conv_w, conv_b)
    y = v_t
    for o, gate in enumerate((x1_t, x2_t)):
        a = _dft_stage1(y, g_sig)
        bv = _dft_stage2_conv(a, kspec, o, f2, f2c)
        y = _dft_inverse(bv, g_inv, y, gate, bias_d[o], token_order=(o == HY_ORDER - 1))
    return y.reshape(b, length, HY_WIDTH)


def _hyena_small_body(cu_ref, w_ref, b_ref, ks_ref, ff_ref, fi_ref, bias_ref, o_ref):
    u = cu_ref[0]
    rows = u.shape[0]
    ridx = lax.broadcasted_iota(I32, u.shape, 0)
    up = jnp.where(ridx == 0, 0.0, pltpu.roll(u, 1, 0))
    dn = jnp.where(ridx == rows - 1, 0.0, pltpu.roll(u, rows - 1, 0))
    z = up * w_ref[0:1] + u * w_ref[1:2] + dn * w_ref[2:3] + b_ref[...]
    cw = HY_WIDTH
    nf = ff_ref.shape[0] // 2
    y = z[:, :cw]
    for o in range(HY_ORDER):
        gate = z[:, (o + 1) * cw:(o + 2) * cw]
        x = _dot(ff_ref[...], y.astype(BF16))
        kf = ks_ref[:, o * cw:(o + 1) * cw]
        xr, xi, kr, ki = x[:nf], x[nf:], kf[:nf], kf[nf:]
        prod = jnp.concatenate([xr * kr - xi * ki, xr * ki + xi * kr], axis=0).astype(BF16)
        conv = _dot(fi_ref[...], prod)
        y = gate * (conv + y * bias_ref[o:o + 1])
    o_ref[0] = y


def _spectrum_small_body(ff_ref, f_ref, o_ref):
    o_ref[...] = _dot(ff_ref[...], f_ref[...].astype(BF16))


def _hyena_small(cu_ctx, conv_w, conv_b, filt_params, bias_d):
    b, length, c3 = cu_ctx.shape
    n = 2 * length
    kt = np.arange(n)[:, None] * np.arange(n)[None, :]
    ang = 2.0 * np.pi * (kt % n) / n
    fwd = np.concatenate([np.cos(ang), -np.sin(ang)], axis=0)
    cast = lambda a: jnp.asarray(a.astype(np.float32)).astype(BF16)
    f_full = cast(fwd)
    f_sig = cast(fwd[:, :length])
    f_inv = cast(np.transpose(fwd[:, :length]) / n)
    filt = _hyena_filter(length, *filt_params, transposed=False)
    nch = filt.shape[1]
    kspec = pl.pallas_call(
        _spectrum_small_body,
        out_shape=jax.ShapeDtypeStruct((2 * n, nch), F32),
        compiler_params=_cp(None),
        name="hyena_small_spectrum",
    )(f_full, filt)
    full = lambda shape: pl.BlockSpec(shape, lambda bi: (0,) * len(shape))
    return pl.pallas_call(
        _hyena_small_body,
        grid=(b,),
        in_specs=[pl.BlockSpec((1, length, c3), lambda bi: (bi, 0, 0)), full((3, c3)), full((1, c3)), full(kspec.shape),
                  full(f_sig.shape), full(f_inv.shape), full(bias_d.shape)],
        out_specs=pl.BlockSpec((1, length, HY_WIDTH), lambda bi: (bi, 0, 0)),
        out_shape=jax.ShapeDtypeStruct((b, length, HY_WIDTH), F32),
        compiler_params=_cp(("arbitrary",)),
        name="hyena_small",
    )(cu_ctx, conv_w, conv_b.reshape(1, c3), kspec, f_sig, f_inv, bias_d)


def _merge_body(x_ref, mod_ref, ya_ref, yb_ref, yc_ref, g_ref, wa_ref, wb_ref, wc_ref, wo_ref, nw_ref, rw_ref,
                xo_ref, h_ref, lg_ref):
    d = x_ref.shape[2]
    g = g_ref[0].astype(F32)
    m = (g[:, :d] * _dot(ya_ref[0], wa_ref[...]) + g[:, d:2 * d] * _dot(yb_ref[0], wb_ref[...])
         + g[:, 2 * d:] * _dot(yc_ref[0].astype(BF16), wc_ref[...]))
    y = _dot(m.astype(BF16), wo_ref[...])
    mod = mod_ref[0]
    x = x_ref[0] + mod[2:3] * y
    xo_ref[0] = x
    hn = x * lax.rsqrt(jnp.mean(x * x, axis=-1, keepdims=True) + RMS_EPS) * nw_ref[...]
    h = hn * (1.0 + mod[4:5]) + mod[3:4]
    for c in range(d // LANE):
        h_ref[:, c, :] = h[:, c * LANE:(c + 1) * LANE]
    lg_ref[...] = _dot_nt(rw_ref[...], h, HI)


def _merge(xc, mod, ya, yb, yc, g, w_a, w_b, w_c, w_o, norm2_w, router_wt, lc):
    b, t, d = xc.shape
    tm = 256
    nb = b
    ctx_tiles = lc // tm
    nt = t // tm
    ne = router_wt.shape[0]
    full = lambda shape: pl.BlockSpec(shape, lambda bi, ti: (0,) * len(shape))
    tok = lambda width: pl.BlockSpec((1, tm, width), lambda bi, ti: (bi, ti, 0))
    return pl.pallas_call(
        _merge_body,
        grid=(b, nt),
        in_specs=[tok(d),
                  pl.BlockSpec((1, N_MOD, d), lambda bi, ti: (jnp.where(ti < ctx_tiles, nb, bi), 0, 0)),
                  tok(ya.shape[2]), tok(yb.shape[2]), tok(yc.shape[2]), tok(g.shape[2]),
                  full(w_a.shape), full(w_b.shape), full(w_c.shape), full(w_o.shape), full((1, d)), full(router_wt.shape)],
        out_specs=[tok(d),
                   pl.BlockSpec((tm, d // LANE, LANE), lambda bi, ti: (bi * nt + ti, 0, 0)),
                   pl.BlockSpec((ne, tm), lambda bi, ti: (0, bi * nt + ti))],
        out_shape=[jax.ShapeDtypeStruct((b, t, d), F32),
                   jax.ShapeDtypeStruct((b * t, d // LANE, LANE), F32),
                   jax.ShapeDtypeStruct((ne, b * t), F32)],
        compiler_params=_cp(("arbitrary", "arbitrary")),
        name="merge_outproj",
    )(xc, mod, ya, yb, yc, g, w_a, w_b, w_c, w_o, norm2_w.reshape(1, d), router_wt)


ROUTE_TILE = 512
MOE_ROW_TILE = 512
FF_PAD = 768
MOE_TOKEN_TILE = 256


def _top2_of4(a):
    m1 = jnp.maximum(jnp.maximum(a[0], a[1]), jnp.maximum(a[2], a[3]))
    i1 = jnp.where(a[0] == m1, 0, jnp.where(a[1] == m1, 1, jnp.where(a[2] == m1, 2, 3)))
    neg = jnp.full_like(m1, -jnp.inf)
    r = [jnp.where(i1 == j, neg, a[j]) for j in range(4)]
    m2 = jnp.maximum(jnp.maximum(r[0], r[1]), jnp.maximum(r[2], r[3]))
    i2 = jnp.where((r[0] == m2) & (i1 != 0), 0,
                   jnp.where((r[1] == m2) & (i1 != 1), 1, jnp.where((r[2] == m2) & (i1 != 2), 2, 3)))
    return m1, m2, i1, i2


def _route_body(lg_ref, rb_ref, tri_ref, e_ref, w_ref, rank_ref, cnt_ref, carry_ref):
    i = pl.program_id(0)

    @pl.when(i == 0)
    def _():
        carry_ref[...] = jnp.zeros_like(carry_ref)

    scores = jax.nn.sigmoid(lg_ref[...])
    sel = scores + rb_ref[...]
    rows = [sel[e:e + 1] for e in range(N_EXPERTS)]
    tops = [_top2_of4(rows[4 * g:4 * g + 4]) for g in range(N_GROUPS)]
    gs = [t[0] + t[1] for t in tops]
    gmax = jnp.maximum(jnp.maximum(gs[0], gs[1]), jnp.maximum(gs[2], gs[3]))
    gi = jnp.where(gs[0] == gmax, 0, jnp.where(gs[1] == gmax, 1, jnp.where(gs[2] == gmax, 2, 3)))
    pick = lambda k: jnp.where(gi == 0, tops[0][k], jnp.where(gi == 1, tops[1][k], jnp.where(gi == 2, tops[2][k], tops[3][k])))
    e0 = gi * EXPERTS_PER_GROUP + pick(2)
    e1 = gi * EXPERTS_PER_GROUP + pick(3)
    eid = lax.broadcasted_iota(I32, scores.shape, 0)
    oh0 = eid == e0
    oh1 = eid == e1
    s0 = jnp.sum(jnp.where(oh0, scores, 0.0), axis=0, keepdims=True)
    s1 = jnp.sum(jnp.where(oh1, scores, 0.0), axis=0, keepdims=True)
    tot = s0 + s1
    e_ref[...] = jnp.concatenate([e0, e1], axis=0)
    w_ref[...] = jnp.concatenate([s0 / tot, s1 / tot], axis=0)
    oh = (oh0 | oh1).astype(BF16)
    incl = _dot(oh, tri_ref[...]) + carry_ref[...]
    excl = incl - 1.0
    r0 = jnp.sum(jnp.where(oh0, excl, 0.0), axis=0, keepdims=True)
    r1 = jnp.sum(jnp.where(oh1, excl, 0.0), axis=0, keepdims=True)
    rank_ref[...] = jnp.concatenate([r0, r1], axis=0).astype(I32)
    carry_ref[...] = incl[:, ROUTE_TILE - 1:ROUTE_TILE]
    cnt_ref[...] = jnp.broadcast_to(incl[:, ROUTE_TILE - 1:ROUTE_TILE], cnt_ref.shape).astype(I32)


def _route(logits_t, router_b):
    ne, t = logits_t.shape
    tl = ROUTE_TILE
    assert t % tl == 0
    tri = jnp.asarray(np.triu(np.ones((tl, tl), np.float32)), BF16)
    two = lambda dt: jax.ShapeDtypeStruct((2, t), dt)
    return pl.pallas_call(
        _route_body,
        grid=(t // tl,),
        in_specs=[pl.BlockSpec((ne, tl), lambda i: (0, i)), pl.BlockSpec((ne, 1), lambda i: (0, 0)),
                  pl.BlockSpec((tl, tl), lambda i: (0, 0))],
        out_specs=[pl.BlockSpec((2, tl), lambda i: (0, i))] * 3 + [pl.BlockSpec((ne, LANE), lambda i: (0, 0))],
        out_shape=[two(I32), two(F32), two(I32), jax.ShapeDtypeStruct((ne, LANE), I32)],
        scratch_shapes=[pltpu.VMEM((ne, 1), F32)],
        compiler_params=_cp(("arbitrary",)),
        name="moe_route",
    )(logits_t, router_b.reshape(ne, 1).astype(F32), tri)


def _dispatch_body(slot_ref, h_ref, init_ref, xs_ref, sem):
    del init_ref
    tm = h_ref.shape[0]

    def copy(j, k):
        return pltpu.make_async_copy(h_ref.at[j], xs_ref.at[slot_ref[0, 0, k * tm + j]], sem)

    def issue(j, c):
        copy(j, 0).start()
        copy(j, 1).start()
        return c

    def drain(j, c):
        copy(j, 0).wait()
        copy(j, 1).wait()
        return c

    lax.fori_loop(0, tm, issue, 0, unroll=8)
    lax.fori_loop(0, tm, drain, 0, unroll=8)


def _dispatch(h3, slots3, s_pad):
    t, nc, _ = h3.shape
    tm = MOE_TOKEN_TILE
    return pl.pallas_call(
        _dispatch_body,
        grid=(t // tm,),
        in_specs=[pl.BlockSpec((1, 1, 2 * tm), lambda i: (i, 0, 0), memory_space=pltpu.SMEM),
                  pl.BlockSpec((tm, nc, LANE), lambda i: (i, 0, 0)),
                  pl.BlockSpec(memory_space=pl.ANY)],
        out_specs=pl.BlockSpec(memory_space=pl.ANY),
        out_shape=jax.ShapeDtypeStruct((s_pad, nc, LANE), F32),
        scratch_shapes=[pltpu.SemaphoreType.DMA(())],
        input_output_aliases={2: 0},
        compiler_params=pltpu.CompilerParams(dimension_semantics=("arbitrary",), has_side_effects=True),
        name="moe_dispatch",
    )(slots3, h3, jnp.zeros((s_pad, nc, LANE), F32))


def _ffn_body(te_ref, nu_ref, x_ref, wg_ref, wu_ref, wd_ref, o_ref):
    i = pl.program_id(0)
    d = wg_ref.shape[1]

    @pl.when(i < nu_ref[0])
    def _():
        x = jnp.concatenate([x_ref[:, c, :] for c in range(d // LANE)], axis=1).astype(BF16)
        hmid = (_silu(_dot(x, wg_ref[0])) * _dot(x, wu_ref[0])).astype(BF16)
        y = _dot(hmid, wd_ref[0])
        for c in range(d // LANE):
            o_ref[:, c, :] = y[:, c * LANE:(c + 1) * LANE]

    @pl.when(i >= nu_ref[0])
    def _():
        o_ref[...] = jnp.zeros_like(o_ref)


def _grouped_ffn(xs, tile_expert, n_used, wg, wu, wd):
    s_pad, nc, _ = xs.shape
    tr = MOE_ROW_TILE
    n_tiles = s_pad // tr
    d = wg.shape[1]
    ff = wg.shape[2]
    grid_spec = pltpu.PrefetchScalarGridSpec(
        num_scalar_prefetch=2,
        grid=(n_tiles,),
        in_specs=[pl.BlockSpec((tr, nc, LANE), lambda i, te, nu: (i, 0, 0)),
                  pl.BlockSpec((1, d, ff), lambda i, te, nu: (te[i], 0, 0)),
                  pl.BlockSpec((1, d, ff), lambda i, te, nu: (te[i], 0, 0)),
                  pl.BlockSpec((1, ff, d), lambda i, te, nu: (te[i], 0, 0))],
        out_specs=pl.BlockSpec((tr, nc, LANE), lambda i, te, nu: (i, 0, 0)),
    )
    return pl.pallas_call(
        _ffn_body,
        grid_spec=grid_spec,
        out_shape=jax.ShapeDtypeStruct((s_pad, nc, LANE), F32),
        compiler_params=_cp(("arbitrary",)),
        name="moe_grouped_ffn",
    )(tile_expert, n_used, xs, wg, wu, wd)


def _combine_body(slot_ref, x_ref, mod_ref, w_ref, nw_ref, ys_ref, o_ref, ybuf, sem, *, final):
    tm = x_ref.shape[1]
    d = x_ref.shape[2]

    def copy(j, k):
        return pltpu.make_async_copy(ys_ref.at[slot_ref[0, 0, k * tm + j]], ybuf.at[k, j], sem)

    def issue(j, c):
        copy(j, 0).start()
        copy(j, 1).start()
        return c

    def drain(j, c):
        copy(j, 0).wait()
        copy(j, 1).wait()
        return c

    lax.fori_loop(0, tm, issue, 0, unroll=8)
    lax.fori_loop(0, tm, drain, 0, unroll=8)
    w0 = w_ref[:, 0:1]
    w1 = w_ref[:, 1:2]
    y = jnp.concatenate([w0 * ybuf[0, :, c, :] + w1 * ybuf[1, :, c, :] for c in range(d // LANE)], axis=1)
    x = x_ref[0] + mod_ref[0][5:6] * y
    if final:
        x = x * lax.rsqrt(jnp.mean(x * x, axis=-1, keepdims=True) + RMS_EPS) * nw_ref[...]
    o_ref[0] = x


def _moe_combine(xc, mod, ys, slots3, w_t, final_w, lc, *, final):
    b, t, d = xc.shape
    tm = MOE_TOKEN_TILE
    nb = b
    ctx_tiles = lc // tm
    nt = t // tm
    row0 = ctx_tiles if final else 0
    n_out = t - row0 * tm
    nc = d // LANE
    return pl.pallas_call(
        functools.partial(_combine_body, final=final),
        grid=(b, nt - row0),
        in_specs=[pl.BlockSpec((1, 1, 2 * tm), lambda bi, ti: (bi * nt + ti + row0, 0, 0), memory_space=pltpu.SMEM),
                  pl.BlockSpec((1, tm, d), lambda bi, ti: (bi, ti + row0, 0)),
                  pl.BlockSpec((1, N_MOD, d), lambda bi, ti: (jnp.where(ti + row0 < ctx_tiles, nb, bi), 0, 0)),
                  pl.BlockSpec((tm, 2), lambda bi, ti: (bi * nt + ti + row0, 0)),
                  pl.BlockSpec((1, d), lambda bi, ti: (0, 0)),
                  pl.BlockSpec(memory_space=pl.ANY)],
        out_specs=pl.BlockSpec((1, tm, d), lambda bi, ti: (bi, ti, 0)),
        out_shape=jax.ShapeDtypeStruct((b, n_out, d), F32),
        scratch_shapes=[pltpu.VMEM((2, tm, nc, LANE), F32), pltpu.SemaphoreType.DMA(())],
        compiler_params=_cp(("arbitrary", "arbitrary")),
        name="moe_combine",
    )(slots3, xc, mod, w_t, final_w.reshape(1, d), ys)


def _moe_experts(h3, logits_t, router_b, wg, wu, wd):
    t = h3.shape[0]
    tr = MOE_ROW_TILE
    tm = MOE_TOKEN_TILE
    e_idx, w_tok, rank, counts = _route(logits_t, router_b)
    counts = counts[:, 0]
    padded = ((counts + tr - 1) // tr) * tr
    ends = jnp.cumsum(padded)
    starts = ends - padded
    start_of = jnp.zeros_like(rank)
    for e in range(N_EXPERTS):
        start_of = jnp.where(e_idx == e, starts[e], start_of)
    slots = start_of + rank
    slots3 = slots.reshape(2, t // tm, tm).transpose(1, 0, 2).reshape(t // tm, 1, 2 * tm)
    s_pad = 2 * t + N_EXPERTS * tr
    n_tiles = s_pad // tr
    tile_start = jnp.arange(n_tiles, dtype=I32) * tr
    n_used = (ends[-1] // tr).astype(I32)
    tile_expert = jnp.sum((ends[None, :] <= tile_start[:, None]).astype(I32), axis=1)
    last_used = jnp.sum((ends <= (n_used - 1) * tr).astype(I32))
    tile_expert = jnp.minimum(jnp.where(jnp.arange(n_tiles) < n_used, tile_expert, last_used), N_EXPERTS - 1).astype(I32)
    xs = _dispatch(h3, slots3, s_pad)
    ys = _grouped_ffn(xs, tile_expert, n_used.reshape(1), wg, wu, wd)
    return ys, slots3, w_tok.T


def kernel(x, c, ctx, c_ctx, ada_w, ada_b, norm1_w, norm2_w, w_in, b_gate, na_rpb, q_norm_w, k_norm_w, hy_conv_w, hy_conv_b,
           hy_f_w1, hy_f_b1, hy_f_w2, hy_f_b2, hy_f_w3, hy_sin_freq, hy_bias_d, w_br_a, w_br_b, w_br_c, w_out, router_w,
           router_b, exp_w_gate, exp_w_up, exp_w_down, final_norm_w):
    b, n, d = x.shape
    lc = ctx.shape[1]
    depth = ada_w.shape[0]
    t = lc + n
    cos_t, sin_t = _rope_tables(lc, n)
    xc = jnp.concatenate([ctx, x], axis=1)
    cvec = jnp.concatenate([c, c_ctx[None], jnp.zeros((SUBLANE - b - 1, d), F32)], axis=0)
    router_wt = router_w.T
    ff = exp_w_gate.shape[-1]
    pad_ff = lambda w, axis: jnp.pad(w, [(0, FF_PAD - ff) if a == axis else (0, 0) for a in range(w.ndim)]).astype(BF16)
    na_w = NA_HEADS * HEAD_DIM
    out = None
    for layer in range(depth):
        mod = _ada_mod(cvec, ada_w[layer], ada_b[layer])[:b + 1].reshape(b + 1, N_MOD, d)
        aq, ak, av, bq, bk, bv, cu, g = _norm_inproj(xc, mod, norm1_w[layer], w_in[layer].astype(BF16), b_gate[layer],
                                                     q_norm_w[layer], k_norm_w[layer], cos_t, sin_t, lc)
        filt_params = (hy_f_w1[layer], hy_f_b1[layer], hy_f_w2[layer], hy_f_b2[layer], hy_f_w3[layer], hy_sin_freq[layer])
        ya_lat = _na_attention(aq, ak, av, _na_bias_table(na_rpb[layer]), lc)
        ya_ctx = _pair_attention(aq, ak[:, :lc], av[:, :lc], q_row0=0, n_q=lc, kv_of_pair=lambda p: p, n_pairs=na_w // LANE,
                                 shared_kv=False)
        gq = GQA_Q_HEADS * HEAD_DIM // LANE
        yb_lat = _pair_attention(bq[:, lc:], bk, bv, q_row0=0, n_q=n, kv_of_pair=lambda p: p // 2, n_pairs=gq, shared_kv=True)
        yb_ctx = _pair_attention(bq, bk[:, :lc], bv[:, :lc], q_row0=0, n_q=lc, kv_of_pair=lambda p: p // 2, n_pairs=gq,
                                 shared_kv=True)
        yc_lat = _hyena_long(cu[:, lc:], hy_conv_w[layer], hy_conv_b[layer], filt_params, hy_bias_d[layer])
        yc_ctx = _hyena_small(cu[:, :lc], hy_conv_w[layer], hy_conv_b[layer], filt_params, hy_bias_d[layer])
        ya = jnp.concatenate([ya_ctx, ya_lat], axis=1)
        yb = jnp.concatenate([yb_ctx, yb_lat], axis=1)
        yc = jnp.concatenate([yc_ctx, yc_lat], axis=1)
        xc, h3, logits_t = _merge(xc, mod, ya, yb, yc, g, w_br_a[layer].astype(BF16), w_br_b[layer].astype(BF16),
                                  w_br_c[layer].astype(BF16), w_out[layer].astype(BF16), norm2_w[layer], router_wt, lc)
        ys, slots3, w_t = _moe_experts(h3, logits_t, router_b, pad_ff(exp_w_gate[layer], 2), pad_ff(exp_w_up[layer], 2),
                                       pad_ff(exp_w_down[layer], 1))
        final = layer == depth - 1
        res = _moe_combine(xc, mod, ys, slots3, w_t, final_norm_w, lc, final=final)
        if final:
            out = res
        else:
            xc = res
    return out
```

```python
import functools
import math

import jax
import jax.numpy as jnp
import numpy as np
from jax import lax
from jax.experimental import pallas as pl
from jax.experimental.pallas import tpu as pltpu

F32 = jnp.float32
BF16 = jnp.bfloat16
I32 = jnp.int32
HI = lax.Precision.HIGHEST

GRID_W = 64
HEAD_DIM = 64
RMS_EPS = 1e-6
N_MOD = 6
NA_HEADS = 4
NA_WIN_R = 8
NA_WIN_C = 16
GQA_Q_HEADS = 8
GQA_KV_HEADS = 2
ROPE_THETA = 10000.0
HY_WIDTH = 256
HY_ORDER = 2
HY_EMB_DIM = 33
HY_DECAY_TARGET = 1e-2
HY_FAST_DECAY_PCT = 0.3
HY_SLOW_DECAY_PCT = 1.5
N_BRANCH = 3
N_EXPERTS = 16
N_GROUPS = 4
EXPERTS_PER_GROUP = 4
LANE = 128
SUBLANE = 8
DFT_N2 = 128
LOG2E = math.log2(math.e)
VMEM_LIMIT = 56 * 1024 * 1024


def _cp(sem, vmem=VMEM_LIMIT, flags=None):
    return pltpu.CompilerParams(dimension_semantics=sem, vmem_limit_bytes=vmem, flags=flags)


def _dot(a, b, prec=None):
    return jnp.dot(a, b, preferred_element_type=F32, precision=prec)


def _dot_nt(a, b, prec=None):
    return lax.dot_general(a, b, (((1,), (1,)), ((), ())), preferred_element_type=F32, precision=prec)


def _silu(x):
    return x * jax.nn.sigmoid(x)


def _ada_body(c_ref, w_ref, b_ref, o_ref):
    o_ref[...] = _dot(_silu(c_ref[...]), w_ref[...], HI) + b_ref[...]


def _ada_mod(cvec, ada_w, ada_b):
    rows, d = cvec.shape
    cols = ada_w.shape[1]
    tn = 1536
    return pl.pallas_call(
        _ada_body,
        grid=(cols // tn,),
        in_specs=[pl.BlockSpec((rows, d), lambda j: (0, 0)),
                  pl.BlockSpec((d, tn), lambda j: (0, j)),
                  pl.BlockSpec((1, tn), lambda j: (0, j))],
        out_specs=pl.BlockSpec((rows, tn), lambda j: (0, j)),
        out_shape=jax.ShapeDtypeStruct((rows, cols), F32),
        compiler_params=_cp(("arbitrary",)),
        name="ada_mod",
    )(cvec, ada_w, ada_b.reshape(1, cols))


def _head_rms(x, ones_bd, w_row):
    x2 = x * x
    hi = x2.astype(BF16)
    lo = (x2 - hi.astype(F32)).astype(BF16)
    ss = _dot(hi, ones_bd) + _dot(lo, ones_bd)
    return x * lax.rsqrt(ss * (1.0 / HEAD_DIM) + RMS_EPS) * w_row


def _rope(x, cos_t, sin_t):
    n = x.shape[1]
    half = HEAD_DIM // 2
    lane = lax.broadcasted_iota(I32, x.shape, 1)
    first = (lane % HEAD_DIM) < half
    swapped = jnp.where(first, pltpu.roll(x, n - half, 1), pltpu.roll(x, half, 1))
    return x * cos_t + swapped * sin_t


def _inproj_body(x_ref, mod_ref, nw_ref, w_ref, bg_ref, qw_ref, kw_ref, cos_ref, sin_ref, bdq_ref, bdk_ref, dup_ref, dupv_ref,
                 aq_ref, ak_ref, av_ref, bqc_ref, bql_ref, bk_ref, bv_ref, cuc_ref, cul_ref, g_ref, *, ctx_tiles):
    is_ctx = pl.program_id(1) < ctx_tiles
    x = x_ref[0]
    mod = mod_ref[0]
    y = x * lax.rsqrt(jnp.mean(x * x, axis=-1, keepdims=True) + RMS_EPS) * nw_ref[...]
    h = (y * (1.0 + mod[1:2]) + mod[0:1]).astype(BF16)
    scale = HEAD_DIM ** -0.5 * LOG2E
    na = NA_HEADS * HEAD_DIM
    qg = GQA_Q_HEADS * HEAD_DIM
    kg = GQA_KV_HEADS * HEAD_DIM
    o = 0
    pa = _dot(h, w_ref[:, o:o + 3 * na])
    aq_ref[0] = (pa[:, :na] * scale).astype(BF16)
    ak_ref[0] = pa[:, na:2 * na].astype(BF16)
    av_ref[0] = pa[:, 2 * na:].astype(BF16)
    o += 3 * na
    cos2 = cos_ref[...]
    sin2 = sin_ref[...]
    pq = _dot(h, w_ref[:, o:o + qg])
    qn = _head_rms(pq, bdq_ref[...], qw_ref[...])
    reps = qg // LANE
    qr = _rope(qn, jnp.concatenate([cos2] * reps, axis=1), jnp.concatenate([sin2] * reps, axis=1))
    bq = (qr * scale).astype(BF16)
    bql_ref[0] = bq

    @pl.when(is_ctx)
    def _():
        bqc_ref[0] = bq

    o += qg
    pkv = _dot(h, w_ref[:, o:o + 2 * kg])
    kn = _head_rms(pkv[:, :kg], bdk_ref[...], kw_ref[...])
    kr = _rope(kn, cos2, sin2).astype(BF16)
    bk_ref[0] = _dot(kr, dup_ref[...]).astype(BF16)
    lane_v = lax.broadcasted_iota(I32, (1, 2 * kg), 1)
    ones_half = ((lane_v % LANE) >= HEAD_DIM).astype(F32)
    bv_ref[0] = (_dot(pkv[:, kg:].astype(BF16), dupv_ref[...]) + ones_half).astype(BF16)
    o += 2 * kg
    cu = _dot(h, w_ref[:, o:o + 3 * HY_WIDTH])
    cul_ref[0] = cu

    @pl.when(is_ctx)
    def _():
        cuc_ref[0] = cu

    o += 3 * HY_WIDTH
    g_ref[0] = jax.nn.sigmoid(_dot(h, w_ref[:, o:]) + bg_ref[...]).astype(BF16)


def _norm_inproj(xc, mod, norm_w, w_in_bf, b_gate, q_norm_w, k_norm_w, cos_t, sin_t, lc):
    b, t, d = xc.shape
    tm = 256
    na = NA_HEADS * HEAD_DIM
    qg = GQA_Q_HEADS * HEAD_DIM
    kg = GQA_KV_HEADS * HEAD_DIM
    ng = N_BRANCH * d
    nb = b
    bdq = jnp.asarray(np.kron(np.eye(GQA_Q_HEADS), np.ones((HEAD_DIM, HEAD_DIM))), BF16)
    bdk = jnp.asarray(np.kron(np.eye(GQA_KV_HEADS), np.ones((HEAD_DIM, HEAD_DIM))), BF16)
    dup_np = np.zeros((kg, 2 * kg), np.float32)
    dupv_np = np.zeros((kg, 2 * kg), np.float32)
    for g in range(GQA_KV_HEADS):
        for r in range(2):
            dup_np[g * HEAD_DIM:(g + 1) * HEAD_DIM, (2 * g + r) * HEAD_DIM:(2 * g + r + 1) * HEAD_DIM] = np.eye(HEAD_DIM)
        dupv_np[g * HEAD_DIM:(g + 1) * HEAD_DIM, 2 * g * HEAD_DIM:(2 * g + 1) * HEAD_DIM] = np.eye(HEAD_DIM)
    dup = jnp.asarray(dup_np, BF16)
    dupv = jnp.asarray(dupv_np, BF16)
    qw = jnp.tile(q_norm_w.astype(F32), GQA_Q_HEADS).reshape(1, qg)
    kw = jnp.tile(k_norm_w.astype(F32), GQA_KV_HEADS).reshape(1, kg)
    ctx_tiles = lc // tm
    full = lambda shape: pl.BlockSpec(shape, lambda bi, ti: (0,) * len(shape))
    tok = lambda width: pl.BlockSpec((1, tm, width), lambda bi, ti: (bi, ti, 0))
    ctx_tok = lambda width: pl.BlockSpec((1, tm, width), lambda bi, ti: (bi, jnp.minimum(ti, ctx_tiles - 1), 0))
    lat_tok = lambda width: pl.BlockSpec((1, tm, width), lambda bi, ti: (bi, jnp.maximum(ti - ctx_tiles, 0), 0))
    cw = 3 * HY_WIDTH
    outs = [(na, BF16, t, tok), (na, BF16, t, tok), (na, BF16, t, tok), (qg, BF16, lc, ctx_tok), (qg, BF16, t - lc, lat_tok),
            (2 * kg, BF16, t, tok), (2 * kg, BF16, t, tok), (cw, F32, lc, ctx_tok), (cw, F32, t - lc, lat_tok), (ng, BF16, t, tok)]
    return pl.pallas_call(
        functools.partial(_inproj_body, ctx_tiles=ctx_tiles),
        grid=(b, t // tm),
        in_specs=[tok(d),
                  pl.BlockSpec((1, N_MOD, d), lambda bi, ti: (jnp.where(ti < ctx_tiles, nb, bi), 0, 0)),
                  full((1, d)), full(w_in_bf.shape), full((1, ng)), full((1, qg)), full((1, kg)),
                  pl.BlockSpec((tm, LANE), lambda bi, ti: (ti, 0)),
                  pl.BlockSpec((tm, LANE), lambda bi, ti: (ti, 0)),
                  full(bdq.shape), full(bdk.shape), full(dup.shape), full(dupv.shape)],
        out_specs=[spec(w) for w, _, _, spec in outs],
        out_shape=[jax.ShapeDtypeStruct((b, rows, w), dt) for w, dt, rows, _ in outs],
        compiler_params=_cp(("arbitrary", "arbitrary")),
        name="norm_inproj",
    )(xc, mod, norm_w.reshape(1, d), w_in_bf, b_gate.reshape(1, ng), qw, kw, cos_t, sin_t, bdq, bdk, dup, dupv)


def _rope_tables(lc, n):
    tpos = jnp.arange(n, dtype=I32)
    rows = (tpos // GRID_W).astype(F32)
    cols = (tpos % GRID_W).astype(F32)
    d_axis = HEAD_DIM // 2
    inv = ROPE_THETA ** (-jnp.arange(0, d_axis, 2, dtype=F32) / d_axis)
    ang = jnp.concatenate([rows[:, None] * inv, cols[:, None] * inv], axis=-1)
    cos, sin = jnp.cos(ang), jnp.sin(ang)
    cos_h = jnp.concatenate([cos, cos], axis=-1)
    sin_h = jnp.concatenate([-sin, sin], axis=-1)
    cos_h = jnp.concatenate([jnp.ones((lc, HEAD_DIM), F32), cos_h], axis=0)
    sin_h = jnp.concatenate([jnp.zeros((lc, HEAD_DIM), F32), sin_h], axis=0)
    return jnp.concatenate([cos_h, cos_h], axis=-1), jnp.concatenate([sin_h, sin_h], axis=-1)


NA_ROWS_PER_STEP = 4


def _na_bias_table(rpb):
    qc = np.arange(GRID_W)[:, None]
    kc = np.arange(GRID_W)[None, :]
    win_c0 = np.clip(qc - NA_WIN_C // 2, 0, GRID_W - NA_WIN_C)
    col_ok = (kc >= win_c0) & (kc < win_c0 + NA_WIN_C)
    d_col = np.clip(kc - qc + NA_WIN_C - 1, 0, 2 * NA_WIN_C - 2)
    base = np.arange(NA_WIN_R)[:, None] + np.arange(NA_WIN_R)[None, :]
    tab = rpb.astype(F32)[:, base][:, :, :, d_col]
    tab = jnp.where(jnp.asarray(col_ok)[None, None, None], tab, -1e30) * LOG2E
    tab = tab.transpose(1, 0, 3, 2, 4)
    return tab.reshape(NA_WIN_R, NA_HEADS * GRID_W, NA_WIN_R * GRID_W)


def _na_body(q_ref, k_ref, v_ref, bias_ref, o_ref, *, lc, n_rows):
    i = pl.program_id(1)
    nq = NA_HEADS * GRID_W
    lane_q = lax.broadcasted_iota(I32, (GRID_W, NA_HEADS * HEAD_DIM), 1) // HEAD_DIM
    lane_o = lane_q
    kctx = k_ref[0, 0:lc, :]
    vctx = v_ref[0, 0:lc, :]
    for j in range(NA_ROWS_PER_STEP):
        r = i * NA_ROWS_PER_STEP + j
        kr0 = jnp.clip(r - NA_WIN_R // 2, 0, n_rows - NA_WIN_R)
        variant = kr0 - r + NA_WIN_R - 1
        kstart = pl.multiple_of(lc + kr0 * GRID_W, GRID_W)
        kwin = k_ref[0, pl.ds(kstart, NA_WIN_R * GRID_W), :]
        vwin = v_ref[0, pl.ds(kstart, NA_WIN_R * GRID_W), :]
        q = q_ref[0, j * GRID_W:(j + 1) * GRID_W, :]
        qm = jnp.concatenate([jnp.where(lane_q == h, q, jnp.zeros_like(q)) for h in range(NA_HEADS)], axis=0)
        s_loc = _dot_nt(qm, kwin) + bias_ref[variant]
        s_ctx = _dot_nt(qm, kctx)
        m = jnp.maximum(jnp.max(s_loc, axis=-1, keepdims=True), jnp.max(s_ctx, axis=-1, keepdims=True))
        p_loc = jnp.exp2(s_loc - m)
        p_ctx = jnp.exp2(s_ctx - m)
        l = jnp.sum(p_loc, axis=-1, keepdims=True) + jnp.sum(p_ctx, axis=-1, keepdims=True)
        o = (_dot(p_loc.astype(BF16), vwin) + _dot(p_ctx.astype(BF16), vctx)) / l
        out = jnp.zeros((GRID_W, NA_HEADS * HEAD_DIM), F32)
        for h in range(NA_HEADS):
            out = jnp.where(lane_o == h, o[h * GRID_W:(h + 1) * GRID_W], out)
        o_ref[0, j * GRID_W:(j + 1) * GRID_W, :] = out.astype(o_ref.dtype)
    del nq


def _na_attention(aq, ak, av, bias_tab, lc):
    b, t, w = aq.shape
    n = t - lc
    n_rows = n // GRID_W
    assert n % GRID_W == 0 and n_rows >= NA_WIN_R and n_rows % NA_ROWS_PER_STEP == 0
    tq = NA_ROWS_PER_STEP * GRID_W
    assert lc % tq == 0
    off = lc // tq
    return pl.pallas_call(
        functools.partial(_na_body, lc=lc, n_rows=n_rows),
        grid=(b, n_rows // NA_ROWS_PER_STEP),
        in_specs=[pl.BlockSpec((1, tq, w), lambda bi, i: (bi, i + off, 0)),
                  pl.BlockSpec((1, t, w), lambda bi, i: (bi, 0, 0)),
                  pl.BlockSpec((1, t, w), lambda bi, i: (bi, 0, 0)),
                  pl.BlockSpec(bias_tab.shape, lambda bi, i: (0, 0, 0))],
        out_specs=pl.BlockSpec((1, tq, w), lambda bi, i: (bi, i, 0)),
        out_shape=jax.ShapeDtypeStruct((b, n, w), BF16),
        compiler_params=_cp(("arbitrary", "arbitrary")),
        name="na_attention",
    )(aq, ak, av, bias_tab)


PAIR_KEY_CHUNK = 256
PAIR_Q_TILE = 512
PAIR_STREAMS = 2


def _pair_attn_body(q_ref, k_ref, v_ref, o_ref, *, kc, shared_kv):
    tq = q_ref.shape[1]
    tk = k_ref.shape[1]
    hs = tq // PAIR_STREAMS
    lane = lax.broadcasted_iota(I32, (hs, LANE), 1)
    lo = lane < HEAD_DIM
    qqs = []
    for si in range(PAIR_STREAMS):
        q = q_ref[0, si * hs:(si + 1) * hs, :]
        zero = jnp.zeros_like(q)
        qqs.append(jnp.concatenate([jnp.where(lo, q, zero), jnp.where(lo, zero, q)], axis=0))
    n_chunks = tk // kc

    def step(c, carry):
        start = 0 if n_chunks == 1 else pl.multiple_of(c * kc, kc)
        kk = k_ref[0, pl.ds(start, kc), :]
        vv = v_ref[0, pl.ds(start, kc), :]
        out = []
        for qq, (m, l, acc) in zip(qqs, carry):
            s = _dot_nt(qq, kk)
            m_new = jnp.maximum(m, jnp.max(s, axis=-1, keepdims=True))
            alpha = jnp.exp2(m - m_new)
            p = jnp.exp2(s - m_new)
            if not shared_kv:
                l = alpha * l + jnp.sum(p, axis=-1, keepdims=True)
            acc = alpha * acc + _dot(p.astype(BF16), vv)
            out.append((m_new, l, acc))
        return tuple(out)

    one = (jnp.full((2 * hs, 1), -jnp.inf, F32), jnp.zeros((2 * hs, 1), F32), jnp.zeros((2 * hs, LANE), F32))
    init = (one,) * PAIR_STREAMS
    res = step(0, init) if n_chunks == 1 else lax.fori_loop(0, n_chunks, step, init, unroll=True)
    for si, (m, l, acc) in enumerate(res):
        a, bb = acc[:hs], acc[hs:]
        if shared_kv:
            oa = a / a[:, HEAD_DIM:HEAD_DIM + 1]
            ob = pltpu.roll(bb / bb[:, HEAD_DIM:HEAD_DIM + 1], HEAD_DIM, 1)
        else:
            oa = a / l[:hs]
            ob = bb / l[hs:]
        o_ref[0, si * hs:(si + 1) * hs, :] = jnp.where(lo, oa, ob).astype(o_ref.dtype)


def _pair_attention(q, k, v, *, q_row0, n_q, kv_of_pair, n_pairs, shared_kv):
    b = q.shape[0]
    tk = k.shape[1]
    tq = PAIR_Q_TILE if (q_row0 % PAIR_Q_TILE == 0 and n_q % PAIR_Q_TILE == 0) else 256
    assert q_row0 % tq == 0 and n_q % tq == 0
    kc = PAIR_KEY_CHUNK if tk % PAIR_KEY_CHUNK == 0 else tk
    off = q_row0 // tq
    return pl.pallas_call(
        functools.partial(_pair_attn_body, kc=kc, shared_kv=shared_kv),
        grid=(b, n_pairs, n_q // tq),
        in_specs=[pl.BlockSpec((1, tq, LANE), lambda bi, p, i: (bi, i + off, p)),
                  pl.BlockSpec((1, tk, LANE), lambda bi, p, i: (bi, 0, kv_of_pair(p))),
                  pl.BlockSpec((1, tk, LANE), lambda bi, p, i: (bi, 0, kv_of_pair(p)))],
        out_specs=pl.BlockSpec((1, tq, LANE), lambda bi, p, i: (bi, i, p)),
        out_shape=jax.ShapeDtypeStruct((b, n_q, n_pairs * LANE), BF16),
        compiler_params=_cp(("arbitrary", "arbitrary", "arbitrary")),
        name="pair_attention",
    )(q, k, v)


def _filter_body(w1t_ref, w1c_ref, w1s_ref, b1_ref, w2_ref, b2_ref, w3_ref, fr_ref, o_ref, *, length, rows, transposed):
    g = pl.program_id(0)
    n_total = 2 * length
    bands = (HY_EMB_DIM - 1) // 2
    f_row = 1e-4 + lax.broadcasted_iota(I32, (1, bands), 1).astype(F32) * ((bands - 1 - 1e-4) / (bands - 1))
    nch = HY_ORDER * HY_WIDTH
    d_lo = math.log(HY_DECAY_TARGET) / HY_SLOW_DECAY_PCT
    d_hi = math.log(HY_DECAY_TARGET) / HY_FAST_DECAY_PCT
    deltas = d_lo + lax.broadcasted_iota(I32, (1, nch), 1).astype(F32) * ((d_hi - d_lo) / (nch - 1))
    freq = fr_ref[...]
    sub = DFT_N2 if transposed else rows
    for j in range(rows // sub):
        tt = g * rows + j * sub + lax.broadcasted_iota(I32, (sub, 1), 0)
        pos = jnp.where(tt < length, tt, n_total - tt).astype(F32)
        tn = pos * (1.0 / (length - 1))
        w = pos * (2.0 * math.pi / length)
        arg = w * f_row
        pre = tn * w1t_ref[...] + _dot(jnp.cos(arg), w1c_ref[...], HI) - _dot(jnp.sin(arg), w1s_ref[...], HI)
        h1 = jnp.sin(freq * (pre + b1_ref[...]))
        h2 = jnp.sin(freq * (_dot(h1, w2_ref[...], HI) + b2_ref[...]))
        filt = _dot(h2, w3_ref[0], HI) * jnp.exp(-tn * jnp.abs(deltas))
        filt = jnp.where(tt == length, 0.0, filt)
        if transposed:
            o_ref[:, j, :] = filt
        else:
            o_ref[...] = filt


def _hyena_filter(length, w1, b1, w2, b2, w3, freq, *, transposed):
    hid = w1.shape[1]
    nch = HY_ORDER * HY_WIDTH
    n_total = 2 * length
    bands = (HY_EMB_DIM - 1) // 2
    rows = SUBLANE * DFT_N2 if transposed else length
    n_steps = n_total // rows
    assert n_total % rows == 0 and n_steps % 2 == 0
    w3d = w3.reshape(hid, 2, nch).transpose(1, 0, 2)
    full = lambda shape: pl.BlockSpec(shape, lambda g: (0,) * len(shape))
    if transposed:
        out_spec = pl.BlockSpec((DFT_N2, SUBLANE, nch), lambda g: (0, g, 0))
        out_shape = jax.ShapeDtypeStruct((DFT_N2, n_total // DFT_N2, nch), F32)
    else:
        out_spec = pl.BlockSpec((rows, nch), lambda g: (g, 0))
        out_shape = jax.ShapeDtypeStruct((n_total, nch), F32)
    return pl.pallas_call(
        functools.partial(_filter_body, length=length, rows=rows, transposed=transposed),
        grid=(n_steps,),
        in_specs=[full((1, hid)), full((bands, hid)), full((bands, hid)), full((1, hid)), full((hid, hid)), full((1, hid)),
                  pl.BlockSpec((1, hid, nch), lambda g: (g // (n_steps // 2), 0, 0)), full((1, hid))],
        out_specs=out_spec,
        out_shape=out_shape,
        compiler_params=_cp(("arbitrary",)),
        name="hyena_filter",
    )(w1[0:1], w1[1:1 + bands], w1[1 + bands:], b1.reshape(1, hid), w2, b2.reshape(1, hid), w3d, freq.reshape(1, hid))


def _dft_tables(n1, n1_used):
    n = n1 * DFT_N2
    n2 = np.arange(DFT_N2)[:, None, None]
    k1 = np.arange(n1)[None, :, None]
    nn1 = np.arange(n1_used)[None, None, :]
    ang = 2.0 * np.pi * ((k1 * (DFT_N2 * nn1 + n2)) % n) / n
    g_fwd = np.concatenate([np.cos(ang), -np.sin(ang)], axis=1)
    g_inv = np.transpose(g_fwd, (0, 2, 1)) / n
    kk = np.arange(DFT_N2)
    a2 = 2.0 * np.pi * ((kk[:, None] * kk[None, :]) % DFT_N2) / DFT_N2
    fr, fi = np.cos(a2), -np.sin(a2)
    f2 = np.block([[fr, -fi], [fi, fr]])
    f2c = np.block([[fr, fi], [-fi, fr]])
    cast = lambda a: jnp.asarray(a.astype(np.float32)).astype(BF16)
    return cast(g_fwd), cast(g_inv), cast(f2), cast(f2c)


def _stage1_body(x_ref, g_ref, o_ref, *, n1):
    for j in range(SUBLANE):
        res = _dot(g_ref[j], x_ref[0, j].astype(BF16))
        o_ref[0, 0, :, j, :] = res[:n1]
        o_ref[0, 1, :, j, :] = res[n1:]


def _dft_stage1(x_t, g_fwd):
    bx, _, n1u, c = x_t.shape
    n1 = g_fwd.shape[1] // 2
    return pl.pallas_call(
        functools.partial(_stage1_body, n1=n1),
        grid=(bx, DFT_N2 // SUBLANE),
        in_specs=[pl.BlockSpec((1, SUBLANE, n1u, c), lambda bi, g: (bi, g, 0, 0)),
                  pl.BlockSpec((SUBLANE, 2 * n1, n1u), lambda bi, g: (g, 0, 0))],
        out_specs=pl.BlockSpec((1, 2, n1, SUBLANE, c), lambda bi, g: (bi, 0, 0, g, 0)),
        out_shape=jax.ShapeDtypeStruct((bx, 2, n1, DFT_N2, c), F32),
        compiler_params=_cp(("arbitrary", "arbitrary")),
        name="dft_stage1",
    )(x_t, g_fwd)


def _stage2_spec_body(a_ref, f2_ref, o_ref):
    for j in range(SUBLANE):
        slab = jnp.concatenate([a_ref[0, 0, j], a_ref[0, 1, j]], axis=0).astype(BF16)
        o_ref[j] = _dot(f2_ref[...], slab)


def _dft_stage2_spectrum(a, f2):
    _, _, n1, _, c = a.shape
    return pl.pallas_call(
        _stage2_spec_body,
        grid=(n1 // SUBLANE,),
        in_specs=[pl.BlockSpec((1, 2, SUBLANE, DFT_N2, c), lambda g: (0, 0, g, 0, 0)),
                  pl.BlockSpec(f2.shape, lambda g: (0, 0))],
        out_specs=pl.BlockSpec((SUBLANE, 2 * DFT_N2, c), lambda g: (g, 0, 0)),
        out_shape=jax.ShapeDtypeStruct((n1, 2 * DFT_N2, c), F32),
        compiler_params=_cp(("arbitrary",)),
        name="dft_stage2_spectrum",
    )(a, f2)


def _stage2_conv_body(a_ref, k_ref, f2_ref, f2c_ref, o_ref):
    h = DFT_N2
    for j in range(SUBLANE):
        slab = jnp.concatenate([a_ref[0, 0, j], a_ref[0, 1, j]], axis=0).astype(BF16)
        x = _dot(f2_ref[...], slab)
        kf = k_ref[j]
        xr, xi, kr, ki = x[:h], x[h:], kf[:h], kf[h:]
        y = jnp.concatenate([xr * kr - xi * ki, xr * ki + xi * kr], axis=0).astype(BF16)
        bv = _dot(f2c_ref[...], y)
        o_ref[0, 0, :, j, :] = bv[:h]
        o_ref[0, 1, :, j, :] = bv[h:]


def _dft_stage2_conv(a, kspec, order, f2, f2c):
    bx, _, n1, _, c = a.shape
    return pl.pallas_call(
        _stage2_conv_body,
        grid=(bx, n1 // SUBLANE),
        in_specs=[pl.BlockSpec((1, 2, SUBLANE, DFT_N2, c), lambda bi, g: (bi, 0, g, 0, 0)),
                  pl.BlockSpec((SUBLANE, 2 * DFT_N2, c), lambda bi, g: (g, 0, order)),
                  pl.BlockSpec(f2.shape, lambda bi, g: (0, 0)),
                  pl.BlockSpec(f2c.shape, lambda bi, g: (0, 0))],
        out_specs=pl.BlockSpec((1, 2, DFT_N2, SUBLANE, c), lambda bi, g: (bi, 0, 0, g, 0)),
        out_shape=jax.ShapeDtypeStruct((bx, 2, DFT_N2, n1, c), F32),
        compiler_params=_cp(("arbitrary", "arbitrary")),
        name="dft_stage2_conv",
    )(a, kspec, f2, f2c)


def _inverse_body(b_ref, g_ref, u_ref, gate_ref, bias_ref, o_ref, *, token_order):
    for j in range(SUBLANE):
        slab = jnp.concatenate([b_ref[0, 0, j], b_ref[0, 1, j]], axis=0).astype(BF16)
        y = _dot(g_ref[j], slab)
        out = gate_ref[0, j] * (y + u_ref[0, j] * bias_ref[...])
        if token_order:
            o_ref[0, :, j, :] = out
        else:
            o_ref[0, j] = out


def _dft_inverse(bv, g_inv, u_t, gate_t, bias, *, token_order):
    bx, _, _, n1, c = bv.shape
    n1u = g_inv.shape[1]
    t_spec = pl.BlockSpec((1, SUBLANE, n1u, c), lambda bi, g: (bi, g, 0, 0))
    if token_order:
        out_spec = pl.BlockSpec((1, n1u, SUBLANE, c), lambda bi, g: (bi, 0, g, 0))
        out_shape = jax.ShapeDtypeStruct((bx, n1u, DFT_N2, c), F32)
    else:
        out_spec = t_spec
        out_shape = jax.ShapeDtypeStruct((bx, DFT_N2, n1u, c), F32)
    return pl.pallas_call(
        functools.partial(_inverse_body, token_order=token_order),
        grid=(bx, DFT_N2 // SUBLANE),
        in_specs=[pl.BlockSpec((1, 2, SUBLANE, n1, c), lambda bi, g: (bi, 0, g, 0, 0)),
                  pl.BlockSpec((SUBLANE, n1u, 2 * n1), lambda bi, g: (g, 0, 0)),
                  t_spec, t_spec,
                  pl.BlockSpec((1, c), lambda bi, g: (0, 0))],
        out_specs=out_spec,
        out_shape=out_shape,
        compiler_params=_cp(("arbitrary", "arbitrary")),
        name="dft_inverse",
    )(bv, g_inv, u_t, gate_t, bias.reshape(1, c))


def _shortconv_body(prev_ref, cur_ref, next_ref, w_ref, b_ref, v_ref, x1_ref, x2_ref, *, n_steps):
    i = pl.program_id(1)
    u = cur_ref[0]
    rows = u.shape[0]
    prev_row = jnp.where(i > 0, prev_ref[0, SUBLANE - 1:SUBLANE, :], 0.0)
    next_row = jnp.where(i < n_steps - 1, next_ref[0, 0:1, :], 0.0)
    ridx = lax.broadcasted_iota(I32, u.shape, 0)
    up = jnp.where(ridx == 0, prev_row, pltpu.roll(u, 1, 0))
    dn = jnp.where(ridx == rows - 1, next_row, pltpu.roll(u, rows - 1, 0))
    z = up * w_ref[0:1] + u * w_ref[1:2] + dn * w_ref[2:3] + b_ref[...]
    cw = HY_WIDTH
    for j in range(rows // DFT_N2):
        zj = z[j * DFT_N2:(j + 1) * DFT_N2]
        v_ref[0, :, j, :] = zj[:, :cw]
        x1_ref[0, :, j, :] = zj[:, cw:2 * cw]
        x2_ref[0, :, j, :] = zj[:, 2 * cw:]


def _short_conv_t(cu, conv_w, conv_b):
    b, length, c3 = cu.shape
    rows = SUBLANE * DFT_N2
    assert length % rows == 0
    n_steps = length // rows
    rb = rows // SUBLANE
    n1u = length // DFT_N2
    out_spec = pl.BlockSpec((1, DFT_N2, SUBLANE, HY_WIDTH), lambda bi, i: (bi, 0, i, 0))
    out_shape = jax.ShapeDtypeStruct((b, DFT_N2, n1u, HY_WIDTH), F32)
    return pl.pallas_call(
        functools.partial(_shortconv_body, n_steps=n_steps),
        grid=(b, n_steps),
        in_specs=[pl.BlockSpec((1, SUBLANE, c3), lambda bi, i: (bi, jnp.maximum(i * rb - 1, 0), 0)),
                  pl.BlockSpec((1, rows, c3), lambda bi, i: (bi, i, 0)),
                  pl.BlockSpec((1, SUBLANE, c3), lambda bi, i: (bi, jnp.minimum((i + 1) * rb, n_steps * rb - 1), 0)),
                  pl.BlockSpec((3, c3), lambda bi, i: (0, 0)),
                  pl.BlockSpec((1, c3), lambda bi, i: (0, 0))],
        out_specs=[out_spec] * 3,
        out_shape=[out_shape] * 3,
        compiler_params=_cp(("arbitrary", "arbitrary")),
        name="hyena_short_conv",
    )(cu, cu, cu, conv_w, conv_b.reshape(1, c3))


def _hyena_long(cu_lat, conv_w, conv_b, filt_params, bias_d):
    b, length, _ = cu_lat.shape
    n1u = length // DFT_N2
    n1 = 2 * n1u
    g_sig, g_inv, f2, f2c = _dft_tables(n1, n1u)
    g_full = _dft_tables(n1, n1)[0]
    filt_t = _hyena_filter(length, *filt_params, transposed=True)
    kspec = _dft_stage2_spectrum(_dft_stage1(filt_t[None], g_full), f2)
    v_t, x1_t, x2_t = _short_conv_t(cu_lat, conv_w, conv_b)
    y = v_t
    for o, gate in enumerate((x1_t, x2_t)):
        a = _dft_stage1(y, g_sig)
        bv = _dft_stage2_conv(a, kspec, o, f2, f2c)
        y = _dft_inverse(bv, g_inv, y, gate, bias_d[o], token_order=(o == HY_ORDER - 1))
    return y.reshape(b, length, HY_WIDTH)


def _hyena_small_body(cu_ref, w_ref, b_ref, ks_ref, ff_ref, fi_ref, bias_ref, o_ref):
    u = cu_ref[0]
    rows = u.shape[0]
    ridx = lax.broadcasted_iota(I32, u.shape, 0)
    up = jnp.where(ridx == 0, 0.0, pltpu.roll(u, 1, 0))
    dn = jnp.where(ridx == rows - 1, 0.0, pltpu.roll(u, rows - 1, 0))
    z = up * w_ref[0:1] + u * w_ref[1:2] + dn * w_ref[2:3] + b_ref[...]
    cw = HY_WIDTH
    nf = ff_ref.shape[0] // 2
    y = z[:, :cw]
    for o in range(HY_ORDER):
        gate = z[:, (o + 1) * cw:(o + 2) * cw]
        x = _dot(ff_ref[...], y.astype(BF16))
        kf = ks_ref[:, o * cw:(o + 1) * cw]
        xr, xi, kr, ki = x[:nf], x[nf:], kf[:nf], kf[nf:]
        prod = jnp.concatenate([xr * kr - xi * ki, xr * ki + xi * kr], axis=0).astype(BF16)
        conv = _dot(fi_ref[...], prod)
        y = gate * (conv + y * bias_ref[o:o + 1])
    o_ref[0] = y


def _spectrum_small_body(ff_ref, f_ref, o_ref):
    o_ref[...] = _dot(ff_ref[...], f_ref[...].astype(BF16))


def _hyena_small(cu_ctx, conv_w, conv_b, filt_params, bias_d):
    b, length, c3 = cu_ctx.shape
    n = 2 * length
    kt = np.arange(n)[:, None] * np.arange(n)[None, :]
    ang = 2.0 * np.pi * (kt % n) / n
    fwd = np.concatenate([np.cos(ang), -np.sin(ang)], axis=0)
    cast = lambda a: jnp.asarray(a.astype(np.float32)).astype(BF16)
    f_full = cast(fwd)
    f_sig = cast(fwd[:, :length])
    f_inv = cast(np.transpose(fwd[:, :length]) / n)
    filt = _hyena_filter(length, *filt_params, transposed=False)
    nch = filt.shape[1]
    kspec = pl.pallas_call(
        _spectrum_small_body,
        out_shape=jax.ShapeDtypeStruct((2 * n, nch), F32),
        compiler_params=_cp(None),
        name="hyena_small_spectrum",
    )(f_full, filt)
    full = lambda shape: pl.BlockSpec(shape, lambda bi: (0,) * len(shape))
    return pl.pallas_call(
        _hyena_small_body,
        grid=(b,),
        in_specs=[pl.BlockSpec((1, length, c3), lambda bi: (bi, 0, 0)), full((3, c3)), full((1, c3)), full(kspec.shape),
                  full(f_sig.shape), full(f_inv.shape), full(bias_d.shape)],
        out_specs=pl.BlockSpec((1, length, HY_WIDTH), lambda bi: (bi, 0, 0)),
        out_shape=jax.ShapeDtypeStruct((b, length, HY_WIDTH), F32),
        compiler_params=_cp(("arbitrary",)),
        name="hyena_small",
    )(cu_ctx, conv_w, conv_b.reshape(1, c3), kspec, f_sig, f_inv, bias_d)


def _merge_body(x_ref, mod_ref, yac_ref, yal_ref, ybc_ref, ybl_ref, ycc_ref, ycl_ref, g_ref, wa_ref, wb_ref, wc_ref, wo_ref,
                nw_ref, rwh_ref, rwl_ref, xo_ref, h_ref, lg_ref, *, ctx_tiles):
    d = x_ref.shape[2]
    is_ctx = pl.program_id(1) < ctx_tiles
    ya = jnp.where(is_ctx, yac_ref[0], yal_ref[0])
    yb = jnp.where(is_ctx, ybc_ref[0], ybl_ref[0])
    yc = jnp.where(is_ctx, ycc_ref[0], ycl_ref[0])
    g = g_ref[0].astype(F32)
    m = (g[:, :d] * _dot(ya, wa_ref[...]) + g[:, d:2 * d] * _dot(yb, wb_ref[...])
         + g[:, 2 * d:] * _dot(yc.astype(BF16), wc_ref[...]))
    y = _dot(m.astype(BF16), wo_ref[...])
    mod = mod_ref[0]
    x = x_ref[0] + mod[2:3] * y
    xo_ref[0] = x
    hn = x * lax.rsqrt(jnp.mean(x * x, axis=-1, keepdims=True) + RMS_EPS) * nw_ref[...]
    h = hn * (1.0 + mod[4:5]) + mod[3:4]
    h_ref[...] = h
    h_hi = h.astype(BF16)
    h_lo = (h - h_hi.astype(F32)).astype(BF16)
    lg_ref[...] = _dot(h_hi, rwh_ref[...]) + _dot(h_hi, rwl_ref[...]) + _dot(h_lo, rwh_ref[...])


def _merge(xc, mod, ya, yb, yc, g, w_a, w_b, w_c, w_o, norm2_w, router_w, lc):
    b, t, d = xc.shape
    tm = 256
    nb = b
    ctx_tiles = lc // tm
    nt = t // tm
    ne = router_w.shape[1]
    rw = jnp.pad(router_w.astype(F32), ((0, 0), (0, LANE - ne)))
    rw_hi = rw.astype(BF16)
    rw_lo = (rw - rw_hi.astype(F32)).astype(BF16)
    full = lambda shape: pl.BlockSpec(shape, lambda bi, ti: (0,) * len(shape))
    tok = lambda width: pl.BlockSpec((1, tm, width), lambda bi, ti: (bi, ti, 0))
    ctx_tok = lambda width: pl.BlockSpec((1, tm, width), lambda bi, ti: (bi, jnp.minimum(ti, ctx_tiles - 1), 0))
    lat_tok = lambda width: pl.BlockSpec((1, tm, width), lambda bi, ti: (bi, jnp.maximum(ti - ctx_tiles, 0), 0))
    branch_specs = []
    for pair in (ya, yb, yc):
        branch_specs += [ctx_tok(pair[0].shape[2]), lat_tok(pair[1].shape[2])]
    return pl.pallas_call(
        functools.partial(_merge_body, ctx_tiles=ctx_tiles),
        grid=(b, nt),
        in_specs=[tok(d),
                  pl.BlockSpec((1, N_MOD, d), lambda bi, ti: (jnp.where(ti < ctx_tiles, nb, bi), 0, 0))]
                 + branch_specs
                 + [tok(g.shape[2]), full(w_a.shape), full(w_b.shape), full(w_c.shape), full(w_o.shape), full((1, d)),
                    full(rw_hi.shape), full(rw_lo.shape)],
        out_specs=[tok(d),
                   pl.BlockSpec((tm, d), lambda bi, ti: (bi * nt + ti, 0)),
                   pl.BlockSpec((tm, LANE), lambda bi, ti: (bi * nt + ti, 0))],
        out_shape=[jax.ShapeDtypeStruct((b, t, d), F32),
                   jax.ShapeDtypeStruct((b * t, d), F32),
                   jax.ShapeDtypeStruct((b * t, LANE), F32)],
        compiler_params=_cp(("arbitrary", "arbitrary")),
        name="merge_outproj",
    )(xc, mod, ya[0], ya[1], yb[0], yb[1], yc[0], yc[1], g, w_a, w_b, w_c, w_o, norm2_w.reshape(1, d), rw_hi, rw_lo)


ROUTE_TILE = 512
MOE_ROW_TILE = 512
FF_PAD = 768
MOE_TOKEN_TILE = 256


def _top2_of4(a):
    m1 = jnp.maximum(jnp.maximum(a[0], a[1]), jnp.maximum(a[2], a[3]))
    i1 = jnp.where(a[0] == m1, 0, jnp.where(a[1] == m1, 1, jnp.where(a[2] == m1, 2, 3)))
    neg = jnp.full_like(m1, -jnp.inf)
    r = [jnp.where(i1 == j, neg, a[j]) for j in range(4)]
    m2 = jnp.maximum(jnp.maximum(r[0], r[1]), jnp.maximum(r[2], r[3]))
    i2 = jnp.where((r[0] == m2) & (i1 != 0), 0,
                   jnp.where((r[1] == m2) & (i1 != 1), 1, jnp.where((r[2] == m2) & (i1 != 2), 2, 3)))
    return m1, m2, i1, i2


def _route_body(lg_ref, rb_ref, tri_ref, e_ref, w_ref, rank_ref, cnt_ref, carry_ref):
    i = pl.program_id(0)

    @pl.when(i == 0)
    def _():
        carry_ref[...] = jnp.zeros_like(carry_ref)

    scores = jax.nn.sigmoid(lg_ref[...])
    sel = scores + rb_ref[...]
    rows = [sel[e:e + 1] for e in range(N_EXPERTS)]
    tops = [_top2_of4(rows[4 * g:4 * g + 4]) for g in range(N_GROUPS)]
    gs = [t[0] + t[1] for t in tops]
    gmax = jnp.maximum(jnp.maximum(gs[0], gs[1]), jnp.maximum(gs[2], gs[3]))
    gi = jnp.where(gs[0] == gmax, 0, jnp.where(gs[1] == gmax, 1, jnp.where(gs[2] == gmax, 2, 3)))
    pick = lambda k: jnp.where(gi == 0, tops[0][k], jnp.where(gi == 1, tops[1][k], jnp.where(gi == 2, tops[2][k], tops[3][k])))
    e0 = gi * EXPERTS_PER_GROUP + pick(2)
    e1 = gi * EXPERTS_PER_GROUP + pick(3)
    eid = lax.broadcasted_iota(I32, scores.shape, 0)
    oh0 = eid == e0
    oh1 = eid == e1
    s0 = jnp.sum(jnp.where(oh0, scores, 0.0), axis=0, keepdims=True)
    s1 = jnp.sum(jnp.where(oh1, scores, 0.0), axis=0, keepdims=True)
    tot = s0 + s1
    e_ref[...] = jnp.concatenate([e0, e1], axis=0)
    w_ref[...] = jnp.concatenate([s0 / tot, s1 / tot], axis=0)
    oh = (oh0 | oh1).astype(BF16)
    incl = _dot(oh, tri_ref[...]) + carry_ref[...]
    excl = incl - 1.0
    r0 = jnp.sum(jnp.where(oh0, excl, 0.0), axis=0, keepdims=True)
    r1 = jnp.sum(jnp.where(oh1, excl, 0.0), axis=0, keepdims=True)
    rank_ref[...] = jnp.concatenate([r0, r1], axis=0).astype(I32)
    carry_ref[...] = incl[:, ROUTE_TILE - 1:ROUTE_TILE]
    cnt_ref[...] = jnp.broadcast_to(incl[:, ROUTE_TILE - 1:ROUTE_TILE], cnt_ref.shape).astype(I32)


def _route(logits_t, router_b):
    ne, t = logits_t.shape
    tl = ROUTE_TILE
    assert t % tl == 0
    tri = jnp.asarray(np.triu(np.ones((tl, tl), np.float32)), BF16)
    two = lambda dt: jax.ShapeDtypeStruct((2, t), dt)
    return pl.pallas_call(
        _route_body,
        grid=(t // tl,),
        in_specs=[pl.BlockSpec((ne, tl), lambda i: (0, i)), pl.BlockSpec((ne, 1), lambda i: (0, 0)),
                  pl.BlockSpec((tl, tl), lambda i: (0, 0))],
        out_specs=[pl.BlockSpec((2, tl), lambda i: (0, i))] * 3 + [pl.BlockSpec((ne, LANE), lambda i: (0, 0))],
        out_shape=[two(I32), two(F32), two(I32), jax.ShapeDtypeStruct((ne, LANE), I32)],
        scratch_shapes=[pltpu.VMEM((ne, 1), F32)],
        compiler_params=_cp(("arbitrary",)),
        name="moe_route",
    )(logits_t, router_b.reshape(ne, 1).astype(F32), tri)


def _dispatch_body(slot_ref, h_ref, init_ref, xs_ref, sem):
    del init_ref
    tm = h_ref.shape[0]

    def copy(j, k):
        return pltpu.make_async_copy(h_ref.at[pl.ds(j, 1), :], xs_ref.at[pl.ds(slot_ref[0, 0, k * tm + j], 1), :], sem)

    def issue(j, c):
        copy(j, 0).start(priority=0)
        copy(j, 1).start(priority=1)
        return c

    def drain(j, c):
        copy(j, 0).wait()
        copy(j, 1).wait()
        return c

    lax.fori_loop(0, tm, issue, 0, unroll=8)
    lax.fori_loop(0, tm, drain, 0, unroll=8)


def _dispatch(h2, slots3, s_pad):
    t, d = h2.shape
    tm = MOE_TOKEN_TILE
    return pl.pallas_call(
        _dispatch_body,
        grid=(t // tm,),
        in_specs=[pl.BlockSpec((1, 1, 2 * tm), lambda i: (i, 0, 0), memory_space=pltpu.SMEM),
                  pl.BlockSpec((tm, d), lambda i: (i, 0)),
                  pl.BlockSpec(memory_space=pl.ANY)],
        out_specs=pl.BlockSpec(memory_space=pl.ANY),
        out_shape=jax.ShapeDtypeStruct((s_pad, d), F32),
        scratch_shapes=[pltpu.SemaphoreType.DMA(())],
        input_output_aliases={2: 0},
        compiler_params=pltpu.CompilerParams(dimension_semantics=("arbitrary",), has_side_effects=True),
        name="moe_dispatch",
    )(slots3, h2, jnp.zeros((s_pad, d), F32))


def _ffn_body(te_ref, nu_ref, x_ref, wg_ref, wu_ref, wd_ref, o_ref):
    i = pl.program_id(0)

    @pl.when(i < nu_ref[0])
    def _():
        x = x_ref[...].astype(BF16)
        hmid = (_silu(_dot(x, wg_ref[0])) * _dot(x, wu_ref[0])).astype(BF16)
        o_ref[...] = _dot(hmid, wd_ref[0])

    @pl.when(i >= nu_ref[0])
    def _():
        o_ref[...] = jnp.zeros_like(o_ref)


def _grouped_ffn(xs, tile_expert, n_used, wg, wu, wd):
    s_pad, d = xs.shape
    tr = MOE_ROW_TILE
    n_tiles = s_pad // tr
    ff = wg.shape[2]
    grid_spec = pltpu.PrefetchScalarGridSpec(
        num_scalar_prefetch=2,
        grid=(n_tiles,),
        in_specs=[pl.BlockSpec((tr, d), lambda i, te, nu: (i, 0)),
                  pl.BlockSpec((1, d, ff), lambda i, te, nu: (te[i], 0, 0)),
                  pl.BlockSpec((1, d, ff), lambda i, te, nu: (te[i], 0, 0)),
                  pl.BlockSpec((1, ff, d), lambda i, te, nu: (te[i], 0, 0))],
        out_specs=pl.BlockSpec((tr, d), lambda i, te, nu: (i, 0)),
    )
    return pl.pallas_call(
        _ffn_body,
        grid_spec=grid_spec,
        out_shape=jax.ShapeDtypeStruct((s_pad, d), F32),
        compiler_params=_cp(("arbitrary",)),
        name="moe_grouped_ffn",
    )(tile_expert, n_used, xs, wg, wu, wd)


def _combine_body(slot_ref, x_ref, mod_ref, w_ref, nw_ref, ys_ref, o_ref, ybuf, sem, *, final):
    tm = x_ref.shape[1]

    def copy(j, k):
        return pltpu.make_async_copy(ys_ref.at[pl.ds(slot_ref[0, 0, k * tm + j], 1), :], ybuf.at[k, pl.ds(j, 1), :], sem)

    def issue(j, c):
        copy(j, 0).start(priority=0)
        copy(j, 1).start(priority=1)
        return c

    def drain(j, c):
        copy(j, 0).wait()
        copy(j, 1).wait()
        return c

    lax.fori_loop(0, tm, issue, 0, unroll=8)
    lax.fori_loop(0, tm, drain, 0, unroll=8)
    w0 = w_ref[:, 0:1]
    w1 = w_ref[:, 1:2]
    y = w0 * ybuf[0] + w1 * ybuf[1]
    x = x_ref[0] + mod_ref[0][5:6] * y
    if final:
        x = x * lax.rsqrt(jnp.mean(x * x, axis=-1, keepdims=True) + RMS_EPS) * nw_ref[...]
    o_ref[0] = x


def _moe_combine(xc, mod, ys, slots3, w_t, final_w, lc, *, final):
    b, t, d = xc.shape
    tm = MOE_TOKEN_TILE
    nb = b
    ctx_tiles = lc // tm
    nt = t // tm
    row0 = ctx_tiles if final else 0
    n_out = t - row0 * tm
    return pl.pallas_call(
        functools.partial(_combine_body, final=final),
        grid=(b, nt - row0),
        in_specs=[pl.BlockSpec((1, 1, 2 * tm), lambda bi, ti: (bi * nt + ti + row0, 0, 0), memory_space=pltpu.SMEM),
                  pl.BlockSpec((1, tm, d), lambda bi, ti: (bi, ti + row0, 0)),
                  pl.BlockSpec((1, N_MOD, d), lambda bi, ti: (jnp.where(ti + row0 < ctx_tiles, nb, bi), 0, 0)),
                  pl.BlockSpec((tm, 2), lambda bi, ti: (bi * nt + ti + row0, 0)),
                  pl.BlockSpec((1, d), lambda bi, ti: (0, 0)),
                  pl.BlockSpec(memory_space=pl.ANY)],
        out_specs=pl.BlockSpec((1, tm, d), lambda bi, ti: (bi, ti, 0)),
        out_shape=jax.ShapeDtypeStruct((b, n_out, d), F32),
        scratch_shapes=[pltpu.VMEM((2, tm, d), F32), pltpu.SemaphoreType.DMA(())],
        compiler_params=_cp(("arbitrary", "arbitrary")),
        name="moe_combine",
    )(slots3, xc, mod, w_t, final_w.reshape(1, d), ys)


def _moe_experts(h2, logits_t, router_b, wg, wu, wd):
    t = h2.shape[0]
    tr = MOE_ROW_TILE
    tm = MOE_TOKEN_TILE
    e_idx, w_tok, rank, counts = _route(logits_t, router_b)
    counts = counts[:, 0]
    padded = ((counts + tr - 1) // tr) * tr
    ends = jnp.cumsum(padded)
    starts = ends - padded
    start_of = jnp.zeros_like(rank)
    for e in range(N_EXPERTS):
        start_of = jnp.where(e_idx == e, starts[e], start_of)
    slots = start_of + rank
    slots3 = slots.reshape(2, t // tm, tm).transpose(1, 0, 2).reshape(t // tm, 1, 2 * tm)
    s_pad = 2 * t + N_EXPERTS * tr
    n_tiles = s_pad // tr
    tile_start = jnp.arange(n_tiles, dtype=I32) * tr
    n_used = (ends[-1] // tr).astype(I32)
    tile_expert = jnp.sum((ends[None, :] <= tile_start[:, None]).astype(I32), axis=1)
    last_used = jnp.sum((ends <= (n_used - 1) * tr).astype(I32))
    tile_expert = jnp.minimum(jnp.where(jnp.arange(n_tiles) < n_used, tile_expert, last_used), N_EXPERTS - 1).astype(I32)
    xs = _dispatch(h2, slots3, s_pad)
    ys = _grouped_ffn(xs, tile_expert, n_used.reshape(1), wg, wu, wd)
    return ys, slots3, w_tok.T


def kernel(x, c, ctx, c_ctx, ada_w, ada_b, norm1_w, norm2_w, w_in, b_gate, na_rpb, q_norm_w, k_norm_w, hy_conv_w, hy_conv_b,
           hy_f_w1, hy_f_b1, hy_f_w2, hy_f_b2, hy_f_w3, hy_sin_freq, hy_bias_d, w_br_a, w_br_b, w_br_c, w_out, router_w,
           router_b, exp_w_gate, exp_w_up, exp_w_down, final_norm_w):
    b, n, d = x.shape
    lc = ctx.shape[1]
    depth = ada_w.shape[0]
    t = lc + n
    cos_t, sin_t = _rope_tables(lc, n)
    xc = jnp.concatenate([ctx, x], axis=1)
    cvec = jnp.concatenate([c, c_ctx[None], jnp.zeros((SUBLANE - b - 1, d), F32)], axis=0)
    ff = exp_w_gate.shape[-1]
    pad_ff = lambda w, axis: jnp.pad(w, [(0, FF_PAD - ff) if a == axis else (0, 0) for a in range(w.ndim)]).astype(BF16)
    na_w = NA_HEADS * HEAD_DIM
    out = None
    for layer in range(depth):
        mod = _ada_mod(cvec, ada_w[layer], ada_b[layer])[:b + 1].reshape(b + 1, N_MOD, d)
        aq, ak, av, bq_ctx, bq_lat, bk, bv, cu_ctx, cu_lat, g = _norm_inproj(xc, mod, norm1_w[layer], w_in[layer].astype(BF16), b_gate[layer],
                                                     q_norm_w[layer], k_norm_w[layer], cos_t, sin_t, lc)
        filt_params = (hy_f_w1[layer], hy_f_b1[layer], hy_f_w2[layer], hy_f_b2[layer], hy_f_w3[layer], hy_sin_freq[layer])
        ya_lat = _na_attention(aq, ak, av, _na_bias_table(na_rpb[layer]), lc)
        ya_ctx = _pair_attention(aq, ak[:, :lc], av[:, :lc], q_row0=0, n_q=lc, kv_of_pair=lambda p: p, n_pairs=na_w // LANE,
                                 shared_kv=False)
        gq = GQA_Q_HEADS * HEAD_DIM // LANE
        yb_lat = _pair_attention(bq_lat, bk, bv, q_row0=0, n_q=n, kv_of_pair=lambda p: p // 2, n_pairs=gq, shared_kv=True)
        yb_ctx = _pair_attention(bq_ctx, bk[:, :lc], bv[:, :lc], q_row0=0, n_q=lc, kv_of_pair=lambda p: p // 2, n_pairs=gq,
                                 shared_kv=True)
        yc_lat = _hyena_long(cu_lat, hy_conv_w[layer], hy_conv_b[layer], filt_params, hy_bias_d[layer])
        yc_ctx = _hyena_small(cu_ctx, hy_conv_w[layer], hy_conv_b[layer], filt_params, hy_bias_d[layer])
        xc, h2, logits = _merge(xc, mod, (ya_ctx, ya_lat), (yb_ctx, yb_lat), (yc_ctx, yc_lat), g, w_br_a[layer].astype(BF16),
                                w_br_b[layer].astype(BF16), w_br_c[layer].astype(BF16), w_out[layer].astype(BF16),
                                norm2_w[layer], router_w, lc)
        logits_t = logits[:, :N_EXPERTS].T
        ys, slots3, w_t = _moe_experts(h2, logits_t, router_b, pad_ff(exp_w_gate[layer], 2), pad_ff(exp_w_up[layer], 2),
                                       pad_ff(exp_w_down[layer], 1))
        final = layer == depth - 1
        res = _moe_combine(xc, mod, ys, slots3, w_t, final_norm_w, lc, final=final)
        if final:
            out = res
        else:
            xc = res
    return out
```

```python
import functools
import math

import jax
import jax.numpy as jnp
import numpy as np
from jax import lax
from jax.experimental import pallas as pl
from jax.experimental.pallas import tpu as pltpu

F32 = jnp.float32
BF16 = jnp.bfloat16
I32 = jnp.int32
HI = lax.Precision.HIGHEST

GRID_W = 64
HEAD_DIM = 64
RMS_EPS = 1e-6
N_MOD = 6
NA_HEADS = 4
NA_WIN_R = 8
NA_WIN_C = 16
GQA_Q_HEADS = 8
GQA_KV_HEADS = 2
ROPE_THETA = 10000.0
HY_WIDTH = 256
HY_ORDER = 2
HY_EMB_DIM = 33
HY_DECAY_TARGET = 1e-2
HY_FAST_DECAY_PCT = 0.3
HY_SLOW_DECAY_PCT = 1.5
N_BRANCH = 3
N_EXPERTS = 16
N_GROUPS = 4
EXPERTS_PER_GROUP = 4
LANE = 128
SUBLANE = 8
DFT_N2 = 128
LOG2E = math.log2(math.e)
VMEM_LIMIT = 56 * 1024 * 1024


def _cp(sem, vmem=VMEM_LIMIT, flags=None):
    return pltpu.CompilerParams(dimension_semantics=sem, vmem_limit_bytes=vmem, flags=flags)


def _dot(a, b, prec=None):
    return jnp.dot(a, b, preferred_element_type=F32, precision=prec)


def _dot_nt(a, b, prec=None):
    return lax.dot_general(a, b, (((1,), (1,)), ((), ())), preferred_element_type=F32, precision=prec)


def _silu(x):
    return x * jax.nn.sigmoid(x)


def _ada_body(c_ref, w_ref, b_ref, o_ref):
    o_ref[...] = _dot(_silu(c_ref[...]), w_ref[0], HI) + b_ref[0]


def _ada_mod(cvec, ada_w, ada_b, layer):
    rows, d = cvec.shape
    depth, _, cols = ada_w.shape
    tn = 1536
    return pl.pallas_call(
        _ada_body,
        grid=(cols // tn,),
        in_specs=[pl.BlockSpec((rows, d), lambda j: (0, 0)),
                  pl.BlockSpec((1, d, tn), lambda j: (layer, 0, j)),
                  pl.BlockSpec((1, 1, tn), lambda j: (layer, 0, j))],
        out_specs=pl.BlockSpec((rows, tn), lambda j: (0, j)),
        out_shape=jax.ShapeDtypeStruct((rows, cols), F32),
        compiler_params=_cp(("arbitrary",)),
        name="ada_mod",
    )(cvec, ada_w, ada_b.reshape(depth, 1, cols))


def _head_rms(x, ones_bd, w_row):
    x2 = x * x
    hi = x2.astype(BF16)
    lo = (x2 - hi.astype(F32)).astype(BF16)
    ss = _dot(hi, ones_bd) + _dot(lo, ones_bd)
    return x * lax.rsqrt(ss * (1.0 / HEAD_DIM) + RMS_EPS) * w_row


def _rope(x, cos_t, sin_t):
    n = x.shape[1]
    half = HEAD_DIM // 2
    lane = lax.broadcasted_iota(I32, x.shape, 1)
    first = (lane % HEAD_DIM) < half
    swapped = jnp.where(first, pltpu.roll(x, n - half, 1), pltpu.roll(x, half, 1))
    return x * cos_t + swapped * sin_t


def _inproj_body(x_ref, mod_ref, nw_ref, w_ref, bg_ref, qw_ref, kw_ref, cos_ref, sin_ref, bdq_ref, bdk_ref, dup_ref, dupv_ref,
                 aq_ref, ak_ref, av_ref, bq_ref, bql_ref, bk_ref, bv_ref, cu_ref, cul_ref, g_ref):
    x = x_ref[0]
    mod = mod_ref[0]
    y = x * lax.rsqrt(jnp.mean(x * x, axis=-1, keepdims=True) + RMS_EPS) * nw_ref[...]
    h = (y * (1.0 + mod[1:2]) + mod[0:1]).astype(BF16)
    scale = HEAD_DIM ** -0.5 * LOG2E
    na = NA_HEADS * HEAD_DIM
    qg = GQA_Q_HEADS * HEAD_DIM
    kg = GQA_KV_HEADS * HEAD_DIM
    o = 0
    pa = _dot(h, w_ref[:, o:o + 3 * na])
    aq_ref[0] = (pa[:, :na] * scale).astype(BF16)
    ak_ref[0] = pa[:, na:2 * na].astype(BF16)
    av_ref[0] = pa[:, 2 * na:].astype(BF16)
    o += 3 * na
    cos2 = cos_ref[...]
    sin2 = sin_ref[...]
    pq = _dot(h, w_ref[:, o:o + qg])
    qn = _head_rms(pq, bdq_ref[...], qw_ref[...])
    reps = qg // LANE
    qr = _rope(qn, jnp.concatenate([cos2] * reps, axis=1), jnp.concatenate([sin2] * reps, axis=1))
    bq = (qr * scale).astype(BF16)
    bq_ref[0] = bq
    bql_ref[0] = bq
    o += qg
    pkv = _dot(h, w_ref[:, o:o + 2 * kg])
    kn = _head_rms(pkv[:, :kg], bdk_ref[...], kw_ref[...])
    kr = _rope(kn, cos2, sin2).astype(BF16)
    bk_ref[0] = _dot(kr, dup_ref[...]).astype(BF16)
    lane_v = lax.broadcasted_iota(I32, (1, 2 * kg), 1)
    ones_half = ((lane_v % LANE) >= HEAD_DIM).astype(F32)
    bv_ref[0] = (_dot(pkv[:, kg:].astype(BF16), dupv_ref[...]) + ones_half).astype(BF16)
    o += 2 * kg
    cu = _dot(h, w_ref[:, o:o + 3 * HY_WIDTH])
    cu_ref[0] = cu
    cul_ref[0] = cu
    o += 3 * HY_WIDTH
    g_ref[0] = jax.nn.sigmoid(_dot(h, w_ref[:, o:]) + bg_ref[...]).astype(BF16)


def _norm_inproj(xc, mod, norm_w, w_in_bf, b_gate, q_norm_w, k_norm_w, cos_t, sin_t, lc):
    b, t, d = xc.shape
    tm = 256
    na = NA_HEADS * HEAD_DIM
    qg = GQA_Q_HEADS * HEAD_DIM
    kg = GQA_KV_HEADS * HEAD_DIM
    ng = N_BRANCH * d
    nb = b
    bdq = jnp.asarray(np.kron(np.eye(GQA_Q_HEADS), np.ones((HEAD_DIM, HEAD_DIM))), BF16)
    bdk = jnp.asarray(np.kron(np.eye(GQA_KV_HEADS), np.ones((HEAD_DIM, HEAD_DIM))), BF16)
    dup_np = np.zeros((kg, 2 * kg), np.float32)
    dupv_np = np.zeros((kg, 2 * kg), np.float32)
    for g in range(GQA_KV_HEADS):
        for r in range(2):
            dup_np[g * HEAD_DIM:(g + 1) * HEAD_DIM, (2 * g + r) * HEAD_DIM:(2 * g + r + 1) * HEAD_DIM] = np.eye(HEAD_DIM)
        dupv_np[g * HEAD_DIM:(g + 1) * HEAD_DIM, 2 * g * HEAD_DIM:(2 * g + 1) * HEAD_DIM] = np.eye(HEAD_DIM)
    dup = jnp.asarray(dup_np, BF16)
    dupv = jnp.asarray(dupv_np, BF16)
    qw = jnp.tile(q_norm_w.astype(F32), GQA_Q_HEADS).reshape(1, qg)
    kw = jnp.tile(k_norm_w.astype(F32), GQA_KV_HEADS).reshape(1, kg)
    ctx_tiles = lc // tm
    full = lambda shape: pl.BlockSpec(shape, lambda bi, ti: (0,) * len(shape))
    tok = lambda width: pl.BlockSpec((1, tm, width), lambda bi, ti: (bi, ti, 0))
    lat_tok = lambda width: pl.BlockSpec((1, tm, width), lambda bi, ti: (bi, jnp.maximum(ti - ctx_tiles, 0), 0))
    cw = 3 * HY_WIDTH
    outs = [(na, BF16, t, tok), (na, BF16, t, tok), (na, BF16, t, tok), (qg, BF16, t, tok), (qg, BF16, t - lc, lat_tok),
            (2 * kg, BF16, t, tok), (2 * kg, BF16, t, tok), (cw, F32, t, tok), (cw, F32, t - lc, lat_tok), (ng, BF16, t, tok)]
    return pl.pallas_call(
        _inproj_body,
        grid=(b, t // tm),
        in_specs=[tok(d),
                  pl.BlockSpec((1, N_MOD, d), lambda bi, ti: (jnp.where(ti < ctx_tiles, nb, bi), 0, 0)),
                  full((1, d)), full(w_in_bf.shape), full((1, ng)), full((1, qg)), full((1, kg)),
                  pl.BlockSpec((tm, LANE), lambda bi, ti: (ti, 0)),
                  pl.BlockSpec((tm, LANE), lambda bi, ti: (ti, 0)),
                  full(bdq.shape), full(bdk.shape), full(dup.shape), full(dupv.shape)],
        out_specs=[spec(w) for w, _, _, spec in outs],
        out_shape=[jax.ShapeDtypeStruct((b, rows, w), dt) for w, dt, rows, _ in outs],
        compiler_params=_cp(("arbitrary", "arbitrary")),
        name="norm_inproj",
    )(xc, mod, norm_w.reshape(1, d), w_in_bf, b_gate.reshape(1, ng), qw, kw, cos_t, sin_t, bdq, bdk, dup, dupv)


def _rope_tables(lc, n):
    tpos = jnp.arange(n, dtype=I32)
    rows = (tpos // GRID_W).astype(F32)
    cols = (tpos % GRID_W).astype(F32)
    d_axis = HEAD_DIM // 2
    inv = ROPE_THETA ** (-jnp.arange(0, d_axis, 2, dtype=F32) / d_axis)
    ang = jnp.concatenate([rows[:, None] * inv, cols[:, None] * inv], axis=-1)
    cos, sin = jnp.cos(ang), jnp.sin(ang)
    cos_h = jnp.concatenate([cos, cos], axis=-1)
    sin_h = jnp.concatenate([-sin, sin], axis=-1)
    cos_h = jnp.concatenate([jnp.ones((lc, HEAD_DIM), F32), cos_h], axis=0)
    sin_h = jnp.concatenate([jnp.zeros((lc, HEAD_DIM), F32), sin_h], axis=0)
    return jnp.concatenate([cos_h, cos_h], axis=-1), jnp.concatenate([sin_h, sin_h], axis=-1)


NA_ROWS_PER_STEP = 4


def _na_bias_table(rpb):
    qc = np.arange(GRID_W)[:, None]
    kc = np.arange(GRID_W)[None, :]
    win_c0 = np.clip(qc - NA_WIN_C // 2, 0, GRID_W - NA_WIN_C)
    col_ok = (kc >= win_c0) & (kc < win_c0 + NA_WIN_C)
    d_col = np.clip(kc - qc + NA_WIN_C - 1, 0, 2 * NA_WIN_C - 2)
    r = rpb.astype(F32)
    dc = jnp.asarray(d_col)[None, :, None, :]
    cols = jnp.zeros((NA_HEADS, GRID_W, 2 * NA_WIN_R - 1, GRID_W), F32)
    for c in range(2 * NA_WIN_C - 1):
        cols = jnp.where(dc == c, r[:, None, :, c, None], cols)
    cols = jnp.where(jnp.asarray(col_ok)[None, :, None, :], cols, -1e30) * LOG2E
    tab = jnp.stack([cols[:, :, v:v + NA_WIN_R, :] for v in range(NA_WIN_R)], axis=0)
    return tab.reshape(NA_WIN_R, NA_HEADS * GRID_W, NA_WIN_R * GRID_W)


def _na_body(q_ref, k_ref, v_ref, bias_ref, o_ref, *, lc, n_rows):
    i = pl.program_id(1)
    nq = NA_HEADS * GRID_W
    lane_q = lax.broadcasted_iota(I32, (GRID_W, NA_HEADS * HEAD_DIM), 1) // HEAD_DIM
    lane_o = lane_q
    kctx = k_ref[0, 0:lc, :]
    vctx = v_ref[0, 0:lc, :]
    for j in range(NA_ROWS_PER_STEP):
        r = i * NA_ROWS_PER_STEP + j
        kr0 = jnp.clip(r - NA_WIN_R // 2, 0, n_rows - NA_WIN_R)
        variant = kr0 - r + NA_WIN_R - 1
        kstart = pl.multiple_of(lc + kr0 * GRID_W, GRID_W)
        kwin = k_ref[0, pl.ds(kstart, NA_WIN_R * GRID_W), :]
        vwin = v_ref[0, pl.ds(kstart, NA_WIN_R * GRID_W), :]
        q = q_ref[0, j * GRID_W:(j + 1) * GRID_W, :]
        qm = jnp.concatenate([jnp.where(lane_q == h, q, jnp.zeros_like(q)) for h in range(NA_HEADS)], axis=0)
        s_loc = _dot_nt(qm, kwin) + bias_ref[variant]
        s_ctx = _dot_nt(qm, kctx)
        m = jnp.maximum(jnp.max(s_loc, axis=-1, keepdims=True), jnp.max(s_ctx, axis=-1, keepdims=True))
        p_loc = jnp.exp2(s_loc - m)
        p_ctx = jnp.exp2(s_ctx - m)
        l = jnp.sum(p_loc, axis=-1, keepdims=True) + jnp.sum(p_ctx, axis=-1, keepdims=True)
        o = (_dot(p_loc.astype(BF16), vwin) + _dot(p_ctx.astype(BF16), vctx)) / l
        out = jnp.zeros((GRID_W, NA_HEADS * HEAD_DIM), F32)
        for h in range(NA_HEADS):
            out = jnp.where(lane_o == h, o[h * GRID_W:(h + 1) * GRID_W], out)
        o_ref[0, j * GRID_W:(j + 1) * GRID_W, :] = out.astype(o_ref.dtype)
    del nq


def _na_attention(aq, ak, av, bias_tab, lc):
    b, t, w = aq.shape
    n = t - lc
    n_rows = n // GRID_W
    assert n % GRID_W == 0 and n_rows >= NA_WIN_R and n_rows % NA_ROWS_PER_STEP == 0
    tq = NA_ROWS_PER_STEP * GRID_W
    assert lc % tq == 0
    off = lc // tq
    return pl.pallas_call(
        functools.partial(_na_body, lc=lc, n_rows=n_rows),
        grid=(b, n_rows // NA_ROWS_PER_STEP),
        in_specs=[pl.BlockSpec((1, tq, w), lambda bi, i: (bi, i + off, 0)),
                  pl.BlockSpec((1, t, w), lambda bi, i: (bi, 0, 0)),
                  pl.BlockSpec((1, t, w), lambda bi, i: (bi, 0, 0)),
                  pl.BlockSpec(bias_tab.shape, lambda bi, i: (0, 0, 0))],
        out_specs=pl.BlockSpec((1, tq, w), lambda bi, i: (bi, i, 0)),
        out_shape=jax.ShapeDtypeStruct((b, n, w), BF16),
        compiler_params=_cp(("arbitrary", "arbitrary")),
        name="na_attention",
    )(aq, ak, av, bias_tab)


PAIR_KEY_CHUNK = 256
PAIR_Q_TILE = 512
PAIR_STREAMS = 2


def _pair_attn_body(q_ref, k_ref, v_ref, o_ref, *, kc, shared_kv):
    tq = q_ref.shape[1]
    tk = k_ref.shape[1]
    hs = tq // PAIR_STREAMS
    lane = lax.broadcasted_iota(I32, (hs, LANE), 1)
    lo = lane < HEAD_DIM
    qqs = []
    for si in range(PAIR_STREAMS):
        q = q_ref[0, si * hs:(si + 1) * hs, :]
        zero = jnp.zeros_like(q)
        qqs.append(jnp.concatenate([jnp.where(lo, q, zero), jnp.where(lo, zero, q)], axis=0))
    n_chunks = tk // kc

    def step(c, carry):
        start = 0 if n_chunks == 1 else pl.multiple_of(c * kc, kc)
        kk = k_ref[0, pl.ds(start, kc), :]
        vv = v_ref[0, pl.ds(start, kc), :]
        out = []
        for qq, (m, l, acc) in zip(qqs, carry):
            s = _dot_nt(qq, kk)
            m_new = jnp.maximum(m, jnp.max(s, axis=-1, keepdims=True))
            alpha = jnp.exp2(m - m_new)
            p = jnp.exp2(s - m_new)
            if not shared_kv:
                l = alpha * l + jnp.sum(p, axis=-1, keepdims=True)
            acc = alpha * acc + _dot(p.astype(BF16), vv)
            out.append((m_new, l, acc))
        return tuple(out)

    one = (jnp.full((2 * hs, 1), -jnp.inf, F32), jnp.zeros((2 * hs, 1), F32), jnp.zeros((2 * hs, LANE), F32))
    init = (one,) * PAIR_STREAMS
    res = step(0, init) if n_chunks == 1 else lax.fori_loop(0, n_chunks, step, init, unroll=True)
    for si, (m, l, acc) in enumerate(res):
        a, bb = acc[:hs], acc[hs:]
        if shared_kv:
            oa = a / a[:, HEAD_DIM:HEAD_DIM + 1]
            ob = pltpu.roll(bb / bb[:, HEAD_DIM:HEAD_DIM + 1], HEAD_DIM, 1)
        else:
            oa = a / l[:hs]
            ob = bb / l[hs:]
        o_ref[0, si * hs:(si + 1) * hs, :] = jnp.where(lo, oa, ob).astype(o_ref.dtype)


def _pair_attention(q, k, v, *, q_row0, n_q, kv_of_pair, n_pairs, shared_kv):
    b = q.shape[0]
    tk = k.shape[1]
    tq = PAIR_Q_TILE if (q_row0 % PAIR_Q_TILE == 0 and n_q % PAIR_Q_TILE == 0) else 256
    assert q_row0 % tq == 0 and n_q % tq == 0
    kc = PAIR_KEY_CHUNK if tk % PAIR_KEY_CHUNK == 0 else tk
    off = q_row0 // tq
    return pl.pallas_call(
        functools.partial(_pair_attn_body, kc=kc, shared_kv=shared_kv),
        grid=(b, n_pairs, n_q // tq),
        in_specs=[pl.BlockSpec((1, tq, LANE), lambda bi, p, i: (bi, i + off, p)),
                  pl.BlockSpec((1, tk, LANE), lambda bi, p, i: (bi, 0, kv_of_pair(p))),
                  pl.BlockSpec((1, tk, LANE), lambda bi, p, i: (bi, 0, kv_of_pair(p)))],
        out_specs=pl.BlockSpec((1, tq, LANE), lambda bi, p, i: (bi, i, p)),
        out_shape=jax.ShapeDtypeStruct((b, n_q, n_pairs * LANE), BF16),
        compiler_params=_cp(("arbitrary", "arbitrary", "arbitrary")),
        name="pair_attention",
    )(q, k, v)


def _filter_body(w1t_ref, w1c_ref, w1s_ref, b1_ref, w2_ref, b2_ref, w3_ref, fr_ref, o_ref, *, length, rows, transposed):
    g = pl.program_id(0)
    n_total = 2 * length
    bands = (HY_EMB_DIM - 1) // 2
    f_col = 1e-4 + lax.broadcasted_iota(I32, (bands, 1), 0).astype(F32) * ((bands - 1 - 1e-4) / (bands - 1))
    nch = HY_ORDER * HY_WIDTH
    d_lo = math.log(HY_DECAY_TARGET) / HY_SLOW_DECAY_PCT
    d_hi = math.log(HY_DECAY_TARGET) / HY_FAST_DECAY_PCT
    deltas = d_lo + lax.broadcasted_iota(I32, (1, nch), 1).astype(F32) * ((d_hi - d_lo) / (nch - 1))
    freq = fr_ref[...]
    position = lambda tt: jnp.where(tt < length, tt, n_total - tt).astype(F32)
    pos_row = position(g * rows + lax.broadcasted_iota(I32, (1, rows), 1))
    arg = f_col * (pos_row * (2.0 * math.pi / length))
    pre = (w1t_ref[...] * (pos_row * (1.0 / (length - 1))) + _dot(w1c_ref[...], jnp.cos(arg), HI)
           - _dot(w1s_ref[...], jnp.sin(arg), HI))
    h1 = jnp.sin(freq * (pre + b1_ref[...]))
    h2 = jnp.sin(freq * (_dot(w2_ref[...], h1, HI) + b2_ref[...])).T
    sub = DFT_N2 if transposed else rows
    for j in range(rows // sub):
        tt = g * rows + j * sub + lax.broadcasted_iota(I32, (sub, 1), 0)
        tn = position(tt) * (1.0 / (length - 1))
        filt = _dot(h2[j * sub:(j + 1) * sub], w3_ref[0], HI) * jnp.exp(-tn * jnp.abs(deltas))
        filt = jnp.where(tt == length, 0.0, filt)
        if transposed:
            o_ref[:, j, :] = filt
        else:
            o_ref[...] = filt


def _hyena_filter(length, w1, b1, w2, b2, w3, freq, *, transposed):
    hid = w1.shape[1]
    nch = HY_ORDER * HY_WIDTH
    n_total = 2 * length
    bands = (HY_EMB_DIM - 1) // 2
    rows = SUBLANE * DFT_N2 if transposed else length
    n_steps = n_total // rows
    assert n_total % rows == 0 and n_steps % 2 == 0
    w3d = w3.reshape(hid, 2, nch).transpose(1, 0, 2)
    full = lambda shape: pl.BlockSpec(shape, lambda g: (0,) * len(shape))
    if transposed:
        out_spec = pl.BlockSpec((DFT_N2, SUBLANE, nch), lambda g: (0, g, 0))
        out_shape = jax.ShapeDtypeStruct((DFT_N2, n_total // DFT_N2, nch), F32)
    else:
        out_spec = pl.BlockSpec((rows, nch), lambda g: (g, 0))
        out_shape = jax.ShapeDtypeStruct((n_total, nch), F32)
    return pl.pallas_call(
        functools.partial(_filter_body, length=length, rows=rows, transposed=transposed),
        grid=(n_steps,),
        in_specs=[full((hid, 1)), full((hid, bands)), full((hid, bands)), full((hid, 1)), full((hid, hid)), full((hid, 1)),
                  pl.BlockSpec((1, hid, nch), lambda g: (g // (n_steps // 2), 0, 0)), full((hid, 1))],
        out_specs=out_spec,
        out_shape=out_shape,
        compiler_params=_cp(("arbitrary",)),
        name="hyena_filter",
    )(w1[0:1].T, w1[1:1 + bands].T, w1[1 + bands:].T, b1.reshape(hid, 1), w2.T, b2.reshape(hid, 1), w3d, freq.reshape(hid, 1))


def _dft_tables(n1, n1_used):
    n = n1 * DFT_N2
    n2 = np.arange(DFT_N2)[:, None, None]
    k1 = np.arange(n1)[None, :, None]
    nn1 = np.arange(n1_used)[None, None, :]
    ang = 2.0 * np.pi * ((k1 * (DFT_N2 * nn1 + n2)) % n) / n
    g_fwd = np.concatenate([np.cos(ang), -np.sin(ang)], axis=1)
    g_inv = np.transpose(g_fwd, (0, 2, 1)) / n
    kk = np.arange(DFT_N2)
    a2 = 2.0 * np.pi * ((kk[:, None] * kk[None, :]) % DFT_N2) / DFT_N2
    fr, fi = np.cos(a2), -np.sin(a2)
    f2 = np.block([[fr, -fi], [fi, fr]])
    f2c = np.block([[fr, fi], [-fi, fr]])
    cast = lambda a: jnp.asarray(a.astype(np.float32)).astype(BF16)
    return cast(g_fwd), cast(g_inv), cast(f2), cast(f2c)


def _stage1_body(x_ref, g_ref, o_ref, *, n1):
    for j in range(SUBLANE):
        res = _dot(g_ref[j], x_ref[0, j].astype(BF16))
        o_ref[0, 0, :, j, :] = res[:n1]
        o_ref[0, 1, :, j, :] = res[n1:]


def _dft_stage1(x_t, g_fwd):
    bx, _, n1u, c = x_t.shape
    n1 = g_fwd.shape[1] // 2
    return pl.pallas_call(
        functools.partial(_stage1_body, n1=n1),
        grid=(bx, DFT_N2 // SUBLANE),
        in_specs=[pl.BlockSpec((1, SUBLANE, n1u, c), lambda bi, g: (bi, g, 0, 0)),
                  pl.BlockSpec((SUBLANE, 2 * n1, n1u), lambda bi, g: (g, 0, 0))],
        out_specs=pl.BlockSpec((1, 2, n1, SUBLANE, c), lambda bi, g: (bi, 0, 0, g, 0)),
        out_shape=jax.ShapeDtypeStruct((bx, 2, n1, DFT_N2, c), F32),
        compiler_params=_cp(("arbitrary", "arbitrary")),
        name="dft_stage1",
    )(x_t, g_fwd)


def _stage2_spec_body(a_ref, f2_ref, o_ref):
    for j in range(SUBLANE):
        slab = jnp.concatenate([a_ref[0, 0, j], a_ref[0, 1, j]], axis=0).astype(BF16)
        o_ref[j] = _dot(f2_ref[...], slab)


def _dft_stage2_spectrum(a, f2):
    _, _, n1, _, c = a.shape
    return pl.pallas_call(
        _stage2_spec_body,
        grid=(n1 // SUBLANE,),
        in_specs=[pl.BlockSpec((1, 2, SUBLANE, DFT_N2, c), lambda g: (0, 0, g, 0, 0)),
                  pl.BlockSpec(f2.shape, lambda g: (0, 0))],
        out_specs=pl.BlockSpec((SUBLANE, 2 * DFT_N2, c), lambda g: (g, 0, 0)),
        out_shape=jax.ShapeDtypeStruct((n1, 2 * DFT_N2, c), F32),
        compiler_params=_cp(("arbitrary",)),
        name="dft_stage2_spectrum",
    )(a, f2)


def _stage2_conv_body(a_ref, k_ref, f2_ref, f2c_ref, o_ref):
    h = DFT_N2
    for j in range(SUBLANE):
        slab = jnp.concatenate([a_ref[0, 0, j], a_ref[0, 1, j]], axis=0).astype(BF16)
        x = _dot(f2_ref[...], slab)
        kf = k_ref[j]
        xr, xi, kr, ki = x[:h], x[h:], kf[:h], kf[h:]
        y = jnp.concatenate([xr * kr - xi * ki, xr * ki + xi * kr], axis=0).astype(BF16)
        bv = _dot(f2c_ref[...], y)
        o_ref[0, 0, :, j, :] = bv[:h]
        o_ref[0, 1, :, j, :] = bv[h:]


def _dft_stage2_conv(a, kspec, order, f2, f2c):
    bx, _, n1, _, c = a.shape
    return pl.pallas_call(
        _stage2_conv_body,
        grid=(bx, n1 // SUBLANE),
        in_specs=[pl.BlockSpec((1, 2, SUBLANE, DFT_N2, c), lambda bi, g: (bi, 0, g, 0, 0)),
                  pl.BlockSpec((SUBLANE, 2 * DFT_N2, c), lambda bi, g: (g, 0, order)),
                  pl.BlockSpec(f2.shape, lambda bi, g: (0, 0)),
                  pl.BlockSpec(f2c.shape, lambda bi, g: (0, 0))],
        out_specs=pl.BlockSpec((1, 2, DFT_N2, SUBLANE, c), lambda bi, g: (bi, 0, 0, g, 0)),
        out_shape=jax.ShapeDtypeStruct((bx, 2, DFT_N2, n1, c), F32),
        compiler_params=_cp(("arbitrary", "arbitrary")),
        name="dft_stage2_conv",
    )(a, kspec, f2, f2c)


def _inverse_body(b_ref, g_ref, u_ref, gate_ref, bias_ref, o_ref, *, token_order):
    for j in range(SUBLANE):
        slab = jnp.concatenate([b_ref[0, 0, j], b_ref[0, 1, j]], axis=0).astype(BF16)
        y = _dot(g_ref[j], slab)
        out = gate_ref[0, j] * (y + u_ref[0, j] * bias_ref[...])
        if token_order:
            o_ref[0, :, j, :] = out
        else:
            o_ref[0, j] = out


def _dft_inverse(bv, g_inv, u_t, gate_t, bias, *, token_order):
    bx, _, _, n1, c = bv.shape
    n1u = g_inv.shape[1]
    t_spec = pl.BlockSpec((1, SUBLANE, n1u, c), lambda bi, g: (bi, g, 0, 0))
    if token_order:
        out_spec = pl.BlockSpec((1, n1u, SUBLANE, c), lambda bi, g: (bi, 0, g, 0))
        out_shape = jax.ShapeDtypeStruct((bx, n1u, DFT_N2, c), F32)
    else:
        out_spec = t_spec
        out_shape = jax.ShapeDtypeStruct((bx, DFT_N2, n1u, c), F32)
    return pl.pallas_call(
        functools.partial(_inverse_body, token_order=token_order),
        grid=(bx, DFT_N2 // SUBLANE),
        in_specs=[pl.BlockSpec((1, 2, SUBLANE, n1, c), lambda bi, g: (bi, 0, g, 0, 0)),
                  pl.BlockSpec((SUBLANE, n1u, 2 * n1), lambda bi, g: (g, 0, 0)),
                  t_spec, t_spec,
                  pl.BlockSpec((1, c), lambda bi, g: (0, 0))],
        out_specs=out_spec,
        out_shape=out_shape,
        compiler_params=_cp(("arbitrary", "arbitrary")),
        name="dft_inverse",
    )(bv, g_inv, u_t, gate_t, bias.reshape(1, c))


def _shortconv_body(prev_ref, cur_ref, next_ref, w_ref, b_ref, v_ref, x1_ref, x2_ref, *, n_steps):
    i = pl.program_id(1)
    u = cur_ref[0]
    rows = u.shape[0]
    prev_row = jnp.where(i > 0, prev_ref[0, SUBLANE - 1:SUBLANE, :], 0.0)
    next_row = jnp.where(i < n_steps - 1, next_ref[0, 0:1, :], 0.0)
    ridx = lax.broadcasted_iota(I32, u.shape, 0)
    up = jnp.where(ridx == 0, prev_row, pltpu.roll(u, 1, 0))
    dn = jnp.where(ridx == rows - 1, next_row, pltpu.roll(u, rows - 1, 0))
    z = up * w_ref[0:1] + u * w_ref[1:2] + dn * w_ref[2:3] + b_ref[...]
    cw = HY_WIDTH
    for j in range(rows // DFT_N2):
        zj = z[j * DFT_N2:(j + 1) * DFT_N2]
        v_ref[0, :, j, :] = zj[:, :cw]
        x1_ref[0, :, j, :] = zj[:, cw:2 * cw]
        x2_ref[0, :, j, :] = zj[:, 2 * cw:]


def _short_conv_t(cu, conv_w, conv_b):
    b, length, c3 = cu.shape
    rows = SUBLANE * DFT_N2
    assert length % rows == 0
    n_steps = length // rows
    rb = rows // SUBLANE
    n1u = length // DFT_N2
    out_spec = pl.BlockSpec((1, DFT_N2, SUBLANE, HY_WIDTH), lambda bi, i: (bi, 0, i, 0))
    out_shape = jax.ShapeDtypeStruct((b, DFT_N2, n1u, HY_WIDTH), F32)
    return pl.pallas_call(
        functools.partial(_shortconv_body, n_steps=n_steps),
        grid=(b, n_steps),
        in_specs=[pl.BlockSpec((1, SUBLANE, c3), lambda bi, i: (bi, jnp.maximum(i * rb - 1, 0), 0)),
                  pl.BlockSpec((1, rows, c3), lambda bi, i: (bi, i, 0)),
                  pl.BlockSpec((1, SUBLANE, c3), lambda bi, i: (bi, jnp.minimum((i + 1) * rb, n_steps * rb - 1), 0)),
                  pl.BlockSpec((3, c3), lambda bi, i: (0, 0)),
                  pl.BlockSpec((1, c3), lambda bi, i: (0, 0))],
        out_specs=[out_spec] * 3,
        out_shape=[out_shape] * 3,
        compiler_params=_cp(("arbitrary", "arbitrary")),
        name="hyena_short_conv",
    )(cu, cu, cu, conv_w, conv_b.reshape(1, c3))


def _hyena_long(cu_lat, conv_w, conv_b, filt_params, bias_d):
    b, length, _ = cu_lat.shape
    n1u = length // DFT_N2
    n1 = 2 * n1u
    g_sig, g_inv, f2, f2c = _dft_tables(n1, n1u)
    g_full = _dft_tables(n1, n1)[0]
    filt_t = _hyena_filter(length, *filt_params, transposed=True)
    kspec = _dft_stage2_spectrum(_dft_stage1(filt_t[None], g_full), f2)
    v_t, x1_t, x2_t = _short_conv_t(cu_lat, conv_w, conv_b)
    y = v_t
    for o, gate in enumerate((x1_t, x2_t)):
        a = _dft_stage1(y, g_sig)
        bv = _dft_stage2_conv(a, kspec, o, f2, f2c)
        y = _dft_inverse(bv, g_inv, y, gate, bias_d[o], token_order=(o == HY_ORDER - 1))
    return y.reshape(b, length, HY_WIDTH)


def _hyena_small_body(cu_ref, w_ref, b_ref, ks_ref, ff_ref, fi_ref, bias_ref, o_ref):
    u = cu_ref[0]
    rows = u.shape[0]
    ridx = lax.broadcasted_iota(I32, u.shape, 0)
    up = jnp.where(ridx == 0, 0.0, pltpu.roll(u, 1, 0))
    dn = jnp.where(ridx == rows - 1, 0.0, pltpu.roll(u, rows - 1, 0))
    z = up * w_ref[0:1] + u * w_ref[1:2] + dn * w_ref[2:3] + b_ref[...]
    cw = HY_WIDTH
    nf = ff_ref.shape[0] // 2
    y = z[:, :cw]
    for o in range(HY_ORDER):
        gate = z[:, (o + 1) * cw:(o + 2) * cw]
        x = _dot(ff_ref[...], y.astype(BF16))
        kf = ks_ref[:, o * cw:(o + 1) * cw]
        xr, xi, kr, ki = x[:nf], x[nf:], kf[:nf], kf[nf:]
        prod = jnp.concatenate([xr * kr - xi * ki, xr * ki + xi * kr], axis=0).astype(BF16)
        conv = _dot(fi_ref[...], prod)
        y = gate * (conv + y * bias_ref[o:o + 1])
    o_ref[0] = y


def _spectrum_small_body(ff_ref, f_ref, o_ref):
    o_ref[...] = _dot(ff_ref[...], f_ref[...].astype(BF16))


def _hyena_small(cu_ctx, conv_w, conv_b, filt_params, bias_d):
    b, length, c3 = cu_ctx.shape
    n = 2 * length
    kt = np.arange(n)[:, None] * np.arange(n)[None, :]
    ang = 2.0 * np.pi * (kt % n) / n
    fwd = np.concatenate([np.cos(ang), -np.sin(ang)], axis=0)
    cast = lambda a: jnp.asarray(a.astype(np.float32)).astype(BF16)
    f_full = cast(fwd)
    f_sig = cast(fwd[:, :length])
    f_inv = cast(np.transpose(fwd[:, :length]) / n)
    filt = _hyena_filter(length, *filt_params, transposed=False)
    nch = filt.shape[1]
    kspec = pl.pallas_call(
        _spectrum_small_body,
        out_shape=jax.ShapeDtypeStruct((2 * n, nch), F32),
        compiler_params=_cp(None),
        name="hyena_small_spectrum",
    )(f_full, filt)
    full = lambda shape: pl.BlockSpec(shape, lambda bi: (0,) * len(shape))
    return pl.pallas_call(
        _hyena_small_body,
        grid=(b,),
        in_specs=[pl.BlockSpec((1, length, c3), lambda bi: (bi, 0, 0)), full((3, c3)), full((1, c3)), full(kspec.shape),
                  full(f_sig.shape), full(f_inv.shape), full(bias_d.shape)],
        out_specs=pl.BlockSpec((1, length, HY_WIDTH), lambda bi: (bi, 0, 0)),
        out_shape=jax.ShapeDtypeStruct((b, length, HY_WIDTH), F32),
        compiler_params=_cp(("arbitrary",)),
        name="hyena_small",
    )(cu_ctx, conv_w, conv_b.reshape(1, c3), kspec, f_sig, f_inv, bias_d)


def _merge_body(x_ref, mod_ref, yac_ref, yal_ref, ybc_ref, ybl_ref, ycc_ref, ycl_ref, g_ref, wa_ref, wb_ref, wc_ref, wo_ref,
                nw_ref, rwh_ref, rwl_ref, xo_ref, h_ref, lg_ref, *, ctx_tiles):
    d = x_ref.shape[2]
    is_ctx = pl.program_id(1) < ctx_tiles
    ya = jnp.where(is_ctx, yac_ref[0], yal_ref[0])
    yb = jnp.where(is_ctx, ybc_ref[0], ybl_ref[0])
    yc = jnp.where(is_ctx, ycc_ref[0], ycl_ref[0])
    g = g_ref[0].astype(F32)
    m = (g[:, :d] * _dot(ya, wa_ref[...]) + g[:, d:2 * d] * _dot(yb, wb_ref[...])
         + g[:, 2 * d:] * _dot(yc.astype(BF16), wc_ref[...]))
    y = _dot(m.astype(BF16), wo_ref[...])
    mod = mod_ref[0]
    x = x_ref[0] + mod[2:3] * y
    xo_ref[0] = x
    hn = x * lax.rsqrt(jnp.mean(x * x, axis=-1, keepdims=True) + RMS_EPS) * nw_ref[...]
    h = hn * (1.0 + mod[4:5]) + mod[3:4]
    h_ref[...] = h
    h_hi = h.astype(BF16)
    h_lo = (h - h_hi.astype(F32)).astype(BF16)
    lg_ref[...] = _dot(h_hi, rwh_ref[...]) + _dot(h_hi, rwl_ref[...]) + _dot(h_lo, rwh_ref[...])


def _merge(xc, mod, ya, yb, yc, g, w_a, w_b, w_c, w_o, norm2_w, router_w, lc):
    b, t, d = xc.shape
    tm = 256
    nb = b
    ctx_tiles = lc // tm
    nt = t // tm
    ne = router_w.shape[1]
    rw = jnp.pad(router_w.astype(F32), ((0, 0), (0, LANE - ne)))
    rw_hi = rw.astype(BF16)
    rw_lo = (rw - rw_hi.astype(F32)).astype(BF16)
    full = lambda shape: pl.BlockSpec(shape, lambda bi, ti: (0,) * len(shape))
    tok = lambda width: pl.BlockSpec((1, tm, width), lambda bi, ti: (bi, ti, 0))
    ctx_tok = lambda width: pl.BlockSpec((1, tm, width), lambda bi, ti: (bi, jnp.minimum(ti, ctx_tiles - 1), 0))
    lat_tok = lambda width: pl.BlockSpec((1, tm, width), lambda bi, ti: (bi, jnp.maximum(ti - ctx_tiles, 0), 0))
    branch_specs = []
    for pair in (ya, yb, yc):
        branch_specs += [ctx_tok(pair[0].shape[2]), lat_tok(pair[1].shape[2])]
    return pl.pallas_call(
        functools.partial(_merge_body, ctx_tiles=ctx_tiles),
        grid=(b, nt),
        in_specs=[tok(d),
                  pl.BlockSpec((1, N_MOD, d), lambda bi, ti: (jnp.where(ti < ctx_tiles, nb, bi), 0, 0))]
                 + branch_specs
                 + [tok(g.shape[2]), full(w_a.shape), full(w_b.shape), full(w_c.shape), full(w_o.shape), full((1, d)),
                    full(rw_hi.shape), full(rw_lo.shape)],
        out_specs=[tok(d),
                   pl.BlockSpec((tm, d), lambda bi, ti: (bi * nt + ti, 0)),
                   pl.BlockSpec((tm, LANE), lambda bi, ti: (bi * nt + ti, 0))],
        out_shape=[jax.ShapeDtypeStruct((b, t, d), F32),
                   jax.ShapeDtypeStruct((b * t, d), F32),
                   jax.ShapeDtypeStruct((b * t, LANE), F32)],
        compiler_params=_cp(("arbitrary", "arbitrary")),
        name="merge_outproj",
    )(xc, mod, ya[0], ya[1], yb[0], yb[1], yc[0], yc[1], g, w_a, w_b, w_c, w_o, norm2_w.reshape(1, d), rw_hi, rw_lo)


ROUTE_TILE = 512
MOE_ROW_TILE = 512
MOE_TOKEN_TILE = 256


def _top2_of4(a):
    m1 = jnp.maximum(jnp.maximum(a[0], a[1]), jnp.maximum(a[2], a[3]))
    i1 = jnp.where(a[0] == m1, 0, jnp.where(a[1] == m1, 1, jnp.where(a[2] == m1, 2, 3)))
    neg = jnp.full_like(m1, -jnp.inf)
    r = [jnp.where(i1 == j, neg, a[j]) for j in range(4)]
    m2 = jnp.maximum(jnp.maximum(r[0], r[1]), jnp.maximum(r[2], r[3]))
    i2 = jnp.where((r[0] == m2) & (i1 != 0), 0,
                   jnp.where((r[1] == m2) & (i1 != 1), 1, jnp.where((r[2] == m2) & (i1 != 2), 2, 3)))
    return m1, m2, i1, i2


def _route_body(lg_ref, rb_ref, tri_ref, e_ref, w_ref, rank_ref, cnt_ref, carry_ref):
    i = pl.program_id(0)

    @pl.when(i == 0)
    def _():
        carry_ref[...] = jnp.zeros_like(carry_ref)

    scores = jax.nn.sigmoid(lg_ref[...])
    sel = scores + rb_ref[...]
    rows = [sel[e:e + 1] for e in range(N_EXPERTS)]
    tops = [_top2_of4(rows[4 * g:4 * g + 4]) for g in range(N_GROUPS)]
    gs = [t[0] + t[1] for t in tops]
    gmax = jnp.maximum(jnp.maximum(gs[0], gs[1]), jnp.maximum(gs[2], gs[3]))
    gi = jnp.where(gs[0] == gmax, 0, jnp.where(gs[1] == gmax, 1, jnp.where(gs[2] == gmax, 2, 3)))
    pick = lambda k: jnp.where(gi == 0, tops[0][k], jnp.where(gi == 1, tops[1][k], jnp.where(gi == 2, tops[2][k], tops[3][k])))
    e0 = gi * EXPERTS_PER_GROUP + pick(2)
    e1 = gi * EXPERTS_PER_GROUP + pick(3)
    eid = lax.broadcasted_iota(I32, scores.shape, 0)
    oh0 = eid == e0
    oh1 = eid == e1
    s0 = jnp.sum(jnp.where(oh0, scores, 0.0), axis=0, keepdims=True)
    s1 = jnp.sum(jnp.where(oh1, scores, 0.0), axis=0, keepdims=True)
    tot = s0 + s1
    e_ref[...] = jnp.concatenate([e0, e1], axis=0)
    w_ref[...] = jnp.concatenate([s0 / tot, s1 / tot], axis=0)
    oh = (oh0 | oh1).astype(BF16)
    incl = _dot(oh, tri_ref[...]) + carry_ref[...]
    excl = incl - 1.0
    r0 = jnp.sum(jnp.where(oh0, excl, 0.0), axis=0, keepdims=True)
    r1 = jnp.sum(jnp.where(oh1, excl, 0.0), axis=0, keepdims=True)
    rank_ref[...] = jnp.concatenate([r0, r1], axis=0).astype(I32)
    carry_ref[...] = incl[:, ROUTE_TILE - 1:ROUTE_TILE]
    cnt_ref[...] = jnp.broadcast_to(incl[:, ROUTE_TILE - 1:ROUTE_TILE], cnt_ref.shape).astype(I32)


def _route(logits_t, router_b):
    ne, t = logits_t.shape
    tl = ROUTE_TILE
    assert t % tl == 0
    tri = jnp.asarray(np.triu(np.ones((tl, tl), np.float32)), BF16)
    two = lambda dt: jax.ShapeDtypeStruct((2, t), dt)
    return pl.pallas_call(
        _route_body,
        grid=(t // tl,),
        in_specs=[pl.BlockSpec((ne, tl), lambda i: (0, i)), pl.BlockSpec((ne, 1), lambda i: (0, 0)),
                  pl.BlockSpec((tl, tl), lambda i: (0, 0))],
        out_specs=[pl.BlockSpec((2, tl), lambda i: (0, i))] * 3 + [pl.BlockSpec((ne, LANE), lambda i: (0, 0))],
        out_shape=[two(I32), two(F32), two(I32), jax.ShapeDtypeStruct((ne, LANE), I32)],
        scratch_shapes=[pltpu.VMEM((ne, 1), F32)],
        compiler_params=_cp(("arbitrary",)),
        name="moe_route",
    )(logits_t, router_b.reshape(ne, 1).astype(F32), tri)


def _dispatch_body(slot_ref, pad_ref, h_ref, xs_ref, sem, zbuf, zsem):
    tm = h_ref.shape[0]
    n_exp = (pad_ref.shape[1] - 1) // 2
    pieces = [1 << b for b in reversed(range(3, MOE_ROW_TILE.bit_length() - 1))]

    def zero_fill(wait):
        def go(cp, cond):
            @pl.when(cond)
            def _():
                cp.wait() if wait else cp.start()

        for e in range(n_exp):
            first = pad_ref[0, e]
            n_pad = pad_ref[0, n_exp + e]
            lead = (-first) & (SUBLANE - 1)
            for r in range(SUBLANE - 1):
                go(pltpu.make_async_copy(zbuf.at[pl.ds(0, 1), :], xs_ref.at[pl.ds(first + r, 1), :], zsem), r < lead)
            cur = pl.multiple_of(first + lead, SUBLANE)
            rem = n_pad - lead
            for s in pieces:
                go(pltpu.make_async_copy(zbuf.at[pl.ds(0, s), :], xs_ref.at[pl.ds(cur, s), :], zsem), (rem & s) != 0)
                cur = pl.multiple_of(cur + (rem & s), SUBLANE)
        zrows = zbuf.shape[0]
        tail = pad_ref[0, 2 * n_exp]
        for k in range(n_exp * MOE_ROW_TILE // zrows):
            row = pl.multiple_of(tail + k * zrows, zrows)
            go(pltpu.make_async_copy(zbuf, xs_ref.at[pl.ds(row, zrows), :], zsem), row < xs_ref.shape[0])

    @pl.when(pl.program_id(0) == 0)
    def _():
        zbuf[...] = jnp.zeros_like(zbuf)
        zero_fill(False)
        zero_fill(True)

    def copy(j, k):
        return pltpu.make_async_copy(h_ref.at[pl.ds(j, 1), :], xs_ref.at[pl.ds(slot_ref[0, 0, k * tm + j], 1), :], sem)

    def issue(j, c):
        copy(j, 0).start(priority=0)
        copy(j, 1).start(priority=1)
        return c

    def drain(j, c):
        copy(j, 0).wait()
        copy(j, 1).wait()
        return c

    lax.fori_loop(0, tm, issue, 0, unroll=8)
    lax.fori_loop(0, tm, drain, 0, unroll=8)


def _dispatch(h2, slots3, pad_info, s_pad):
    t, d = h2.shape
    tm = MOE_TOKEN_TILE
    return pl.pallas_call(
        _dispatch_body,
        grid=(t // tm,),
        in_specs=[pl.BlockSpec((1, 1, 2 * tm), lambda i: (i, 0, 0), memory_space=pltpu.SMEM),
                  pl.BlockSpec(pad_info.shape, lambda i: (0, 0), memory_space=pltpu.SMEM),
                  pl.BlockSpec((tm, d), lambda i: (i, 0))],
        out_specs=pl.BlockSpec(memory_space=pl.ANY),
        out_shape=jax.ShapeDtypeStruct((s_pad, d), F32),
        scratch_shapes=[pltpu.SemaphoreType.DMA(()), pltpu.VMEM((MOE_ROW_TILE // 2, d), F32), pltpu.SemaphoreType.DMA(())],
        compiler_params=pltpu.CompilerParams(dimension_semantics=("arbitrary",), has_side_effects=True,
                                             vmem_limit_bytes=VMEM_LIMIT),
        name="moe_dispatch",
    )(slots3, pad_info, h2)


def _ffn_body(te_ref, nu_ref, x_ref, wg_ref, wu_ref, wd_ref, o_ref):
    i = pl.program_id(0)

    @pl.when(i < nu_ref[0])
    def _():
        x = x_ref[...].astype(BF16)
        hmid = (_silu(_dot(x, wg_ref[0, 0].astype(BF16))) * _dot(x, wu_ref[0, 0].astype(BF16))).astype(BF16)
        o_ref[...] = _dot(hmid, wd_ref[0, 0].astype(BF16))

    @pl.when(i >= nu_ref[0])
    def _():
        o_ref[...] = jnp.zeros_like(o_ref)


def _grouped_ffn(xs, tile_expert, n_used, wg, wu, wd, layer):
    s_pad, d = xs.shape
    tr = MOE_ROW_TILE
    n_tiles = s_pad // tr
    ff = wg.shape[3]
    grid_spec = pltpu.PrefetchScalarGridSpec(
        num_scalar_prefetch=2,
        grid=(n_tiles,),
        in_specs=[pl.BlockSpec((tr, d), lambda i, te, nu: (jnp.minimum(i, nu[0] - 1), 0)),
                  pl.BlockSpec((1, 1, d, ff), lambda i, te, nu: (layer, te[i], 0, 0)),
                  pl.BlockSpec((1, 1, d, ff), lambda i, te, nu: (layer, te[i], 0, 0)),
                  pl.BlockSpec((1, 1, ff, d), lambda i, te, nu: (layer, te[i], 0, 0))],
        out_specs=pl.BlockSpec((tr, d), lambda i, te, nu: (i, 0)),
    )
    return pl.pallas_call(
        _ffn_body,
        grid_spec=grid_spec,
        out_shape=jax.ShapeDtypeStruct((s_pad, d), F32),
        compiler_params=_cp(("arbitrary",)),
        name="moe_grouped_ffn",
    )(tile_expert, n_used, xs, wg, wu, wd)


def _combine_body(slot_ref, x_ref, mod_ref, w_ref, nw_ref, ys_ref, o_ref, ybuf, sem, *, final):
    tm = x_ref.shape[1]

    def copy(j, k):
        return pltpu.make_async_copy(ys_ref.at[pl.ds(slot_ref[0, 0, k * tm + j], 1), :], ybuf.at[k, pl.ds(j, 1), :], sem)

    def issue(j, c):
        copy(j, 0).start(priority=0)
        copy(j, 1).start(priority=1)
        return c

    def drain(j, c):
        copy(j, 0).wait()
        copy(j, 1).wait()
        return c

    lax.fori_loop(0, tm, issue, 0, unroll=8)
    lax.fori_loop(0, tm, drain, 0, unroll=8)
    w0 = w_ref[:, 0:1]
    w1 = w_ref[:, 1:2]
    y = w0 * ybuf[0] + w1 * ybuf[1]
    x = x_ref[0] + mod_ref[0][5:6] * y
    if final:
        x = x * lax.rsqrt(jnp.mean(x * x, axis=-1, keepdims=True) + RMS_EPS) * nw_ref[...]
    o_ref[0] = x


def _moe_combine(xc, mod, ys, slots3, w_t, final_w, lc, *, final):
    b, t, d = xc.shape
    tm = MOE_TOKEN_TILE
    nb = b
    ctx_tiles = lc // tm
    nt = t // tm
    row0 = ctx_tiles if final else 0
    n_out = t - row0 * tm
    return pl.pallas_call(
        functools.partial(_combine_body, final=final),
        grid=(b, nt - row0),
        in_specs=[pl.BlockSpec((1, 1, 2 * tm), lambda bi, ti: (bi * nt + ti + row0, 0, 0), memory_space=pltpu.SMEM),
                  pl.BlockSpec((1, tm, d), lambda bi, ti: (bi, ti + row0, 0)),
                  pl.BlockSpec((1, N_MOD, d), lambda bi, ti: (jnp.where(ti + row0 < ctx_tiles, nb, bi), 0, 0)),
                  pl.BlockSpec((tm, 2), lambda bi, ti: (bi * nt + ti + row0, 0)),
                  pl.BlockSpec((1, d), lambda bi, ti: (0, 0)),
                  pl.BlockSpec(memory_space=pl.ANY)],
        out_specs=pl.BlockSpec((1, tm, d), lambda bi, ti: (bi, ti, 0)),
        out_shape=jax.ShapeDtypeStruct((b, n_out, d), F32),
        scratch_shapes=[pltpu.VMEM((2, tm, d), F32), pltpu.SemaphoreType.DMA(())],
        compiler_params=_cp(("arbitrary", "arbitrary")),
        name="moe_combine",
    )(slots3, xc, mod, w_t, final_w.reshape(1, d), ys)


def _moe_experts(h2, logits_t, router_b, wg, wu, wd, layer):
    t = h2.shape[0]
    tr = MOE_ROW_TILE
    tm = MOE_TOKEN_TILE
    e_idx, w_tok, rank, counts = _route(logits_t, router_b)
    counts = counts[:, 0]
    padded = ((counts + tr - 1) // tr) * tr
    ends = jnp.cumsum(padded)
    starts = ends - padded
    start_of = jnp.zeros_like(rank)
    for e in range(N_EXPERTS):
        start_of = jnp.where(e_idx == e, starts[e], start_of)
    slots = start_of + rank
    slots3 = slots.reshape(2, t // tm, tm).transpose(1, 0, 2).reshape(t // tm, 1, 2 * tm)
    s_pad = 2 * t + N_EXPERTS * tr
    n_tiles = s_pad // tr
    tile_start = jnp.arange(n_tiles, dtype=I32) * tr
    n_used = (ends[-1] // tr).astype(I32)
    tile_expert = jnp.sum((ends[None, :] <= tile_start[:, None]).astype(I32), axis=1)
    last_used = jnp.sum((ends <= (n_used - 1) * tr).astype(I32))
    tile_expert = jnp.minimum(jnp.where(jnp.arange(n_tiles) < n_used, tile_expert, last_used), N_EXPERTS - 1).astype(I32)
    pad_info = jnp.concatenate([starts + counts, padded - counts, ends[-1:]]).astype(I32).reshape(1, 2 * N_EXPERTS + 1)
    xs = _dispatch(h2, slots3, pad_info, s_pad)
    ys = _grouped_ffn(xs, tile_expert, n_used.reshape(1), wg, wu, wd, layer)
    return ys, slots3, w_tok.T


def kernel(x, c, ctx, c_ctx, ada_w, ada_b, norm1_w, norm2_w, w_in, b_gate, na_rpb, q_norm_w, k_norm_w, hy_conv_w, hy_conv_b,
           hy_f_w1, hy_f_b1, hy_f_w2, hy_f_b2, hy_f_w3, hy_sin_freq, hy_bias_d, w_br_a, w_br_b, w_br_c, w_out, router_w,
           router_b, exp_w_gate, exp_w_up, exp_w_down, final_norm_w):
    b, n, d = x.shape
    lc = ctx.shape[1]
    depth = ada_w.shape[0]
    t = lc + n
    cos_t, sin_t = _rope_tables(lc, n)
    xc = jnp.concatenate([ctx, x], axis=1)
    cvec = jnp.concatenate([c, c_ctx[None], jnp.zeros((SUBLANE - b - 1, d), F32)], axis=0)
    na_w = NA_HEADS * HEAD_DIM
    out = None
    for layer in range(depth):
        mod = _ada_mod(cvec, ada_w, ada_b, layer)[:b + 1].reshape(b + 1, N_MOD, d)
        aq, ak, av, bq, bq_lat, bk, bv, cu, cu_lat, g = _norm_inproj(xc, mod, norm1_w[layer], w_in[layer].astype(BF16),
                                                                     b_gate[layer], q_norm_w[layer], k_norm_w[layer],
                                                                     cos_t, sin_t, lc)
        bq_ctx = bq[:, :lc]
        cu_ctx = cu[:, :lc]
        filt_params = (hy_f_w1[layer], hy_f_b1[layer], hy_f_w2[layer], hy_f_b2[layer], hy_f_w3[layer], hy_sin_freq[layer])
        ya_lat = _na_attention(aq, ak, av, _na_bias_table(na_rpb[layer]), lc)
        ya_ctx = _pair_attention(aq, ak[:, :lc], av[:, :lc], q_row0=0, n_q=lc, kv_of_pair=lambda p: p, n_pairs=na_w // LANE,
                                 shared_kv=False)
        gq = GQA_Q_HEADS * HEAD_DIM // LANE
        yb_lat = _pair_attention(bq_lat, bk, bv, q_row0=0, n_q=n, kv_of_pair=lambda p: p // 2, n_pairs=gq, shared_kv=True)
        yb_ctx = _pair_attention(bq_ctx, bk[:, :lc], bv[:, :lc], q_row0=0, n_q=lc, kv_of_pair=lambda p: p // 2, n_pairs=gq,
                                 shared_kv=True)
        yc_lat = _hyena_long(cu_lat, hy_conv_w[layer], hy_conv_b[layer], filt_params, hy_bias_d[layer])
        yc_ctx = _hyena_small(cu_ctx, hy_conv_w[layer], hy_conv_b[layer], filt_params, hy_bias_d[layer])
        xc, h2, logits = _merge(xc, mod, (ya_ctx, ya_lat), (yb_ctx, yb_lat), (yc_ctx, yc_lat), g, w_br_a[layer].astype(BF16),
                                w_br_b[layer].astype(BF16), w_br_c[layer].astype(BF16), w_out[layer].astype(BF16),
                                norm2_w[layer], router_w, lc)
        logits_t = logits[:, :N_EXPERTS].T
        ys, slots3, w_t = _moe_experts(h2, logits_t, router_b, exp_w_gate, exp_w_up, exp_w_down, layer)
        final = layer == depth - 1
        res = _moe_combine(xc, mod, ys, slots3, w_t, final_norm_w, lc, final=final)
        if final:
            out = res
        else:
            xc = res
    return out
```

```python
import functools
import math

import jax
import jax.numpy as jnp
import numpy as np
from jax import lax
from jax.experimental import pallas as pl
from jax.experimental.pallas import tpu as pltpu

F32 = jnp.float32
BF16 = jnp.bfloat16
I32 = jnp.int32
HI = lax.Precision.HIGHEST

GRID_W = 64
HEAD_DIM = 64
RMS_EPS = 1e-6
N_MOD = 6
NA_HEADS = 4
NA_WIN_R = 8
NA_WIN_C = 16
GQA_Q_HEADS = 8
GQA_KV_HEADS = 2
ROPE_THETA = 10000.0
HY_WIDTH = 256
HY_ORDER = 2
HY_EMB_DIM = 33
HY_DECAY_TARGET = 1e-2
HY_FAST_DECAY_PCT = 0.3
HY_SLOW_DECAY_PCT = 1.5
N_BRANCH = 3
N_EXPERTS = 16
N_GROUPS = 4
EXPERTS_PER_GROUP = 4
LANE = 128
SUBLANE = 8
DFT_N2 = 128
LOG2E = math.log2(math.e)
VMEM_LIMIT = 56 * 1024 * 1024


def _cp(sem, vmem=VMEM_LIMIT, flags=None):
    return pltpu.CompilerParams(dimension_semantics=sem, vmem_limit_bytes=vmem, flags=flags)


def _dot(a, b, prec=None):
    return jnp.dot(a, b, preferred_element_type=F32, precision=prec)


def _dot_nt(a, b, prec=None):
    return lax.dot_general(a, b, (((1,), (1,)), ((), ())), preferred_element_type=F32, precision=prec)


def _silu(x):
    return x * jax.nn.sigmoid(x)


def _ada_body(c_ref, w_ref, b_ref, o_ref):
    o_ref[...] = _dot(_silu(c_ref[...]), w_ref[0], HI) + b_ref[0]


def _ada_mod(cvec, ada_w, ada_b, layer):
    rows, d = cvec.shape
    depth, _, cols = ada_w.shape
    tn = 1536
    return pl.pallas_call(
        _ada_body,
        grid=(cols // tn,),
        in_specs=[pl.BlockSpec((rows, d), lambda j: (0, 0)),
                  pl.BlockSpec((1, d, tn), lambda j: (layer, 0, j)),
                  pl.BlockSpec((1, 1, tn), lambda j: (layer, 0, j))],
        out_specs=pl.BlockSpec((rows, tn), lambda j: (0, j)),
        out_shape=jax.ShapeDtypeStruct((rows, cols), F32),
        compiler_params=_cp(("arbitrary",)),
        name="ada_mod",
    )(cvec, ada_w, ada_b.reshape(depth, 1, cols))


def _head_rms(x, ones_bd, w_row):
    x2 = x * x
    hi = x2.astype(BF16)
    lo = (x2 - hi.astype(F32)).astype(BF16)
    ss = _dot(hi, ones_bd) + _dot(lo, ones_bd)
    return x * lax.rsqrt(ss * (1.0 / HEAD_DIM) + RMS_EPS) * w_row


def _rope(x, cos_t, sin_t):
    n = x.shape[1]
    half = HEAD_DIM // 2
    lane = lax.broadcasted_iota(I32, x.shape, 1)
    first = (lane % HEAD_DIM) < half
    swapped = jnp.where(first, pltpu.roll(x, n - half, 1), pltpu.roll(x, half, 1))
    return x * cos_t + swapped * sin_t


def _inproj_body(x_ref, mod_ref, nw_ref, w_ref, bg_ref, qw_ref, kw_ref, cos_ref, sin_ref, bdq_ref, bdk_ref, dup_ref, dupv_ref,
                 aq_ref, ak_ref, av_ref, bq_ref, bql_ref, bk_ref, bv_ref, cu_ref, cul_ref, g_ref):
    x = x_ref[0]
    mod = mod_ref[0]
    y = x * lax.rsqrt(jnp.mean(x * x, axis=-1, keepdims=True) + RMS_EPS) * nw_ref[...]
    h = (y * (1.0 + mod[1:2]) + mod[0:1]).astype(BF16)
    scale = HEAD_DIM ** -0.5 * LOG2E
    na = NA_HEADS * HEAD_DIM
    qg = GQA_Q_HEADS * HEAD_DIM
    kg = GQA_KV_HEADS * HEAD_DIM
    o = 0
    pa = _dot(h, w_ref[:, o:o + 3 * na])
    aq_ref[0] = (pa[:, :na] * scale).astype(BF16)
    ak_ref[0] = pa[:, na:2 * na].astype(BF16)
    av_ref[0] = pa[:, 2 * na:].astype(BF16)
    o += 3 * na
    cos2 = cos_ref[...]
    sin2 = sin_ref[...]
    pq = _dot(h, w_ref[:, o:o + qg])
    qn = _head_rms(pq, bdq_ref[...], qw_ref[...])
    reps = qg // LANE
    qr = _rope(qn, jnp.concatenate([cos2] * reps, axis=1), jnp.concatenate([sin2] * reps, axis=1))
    bq = (qr * scale).astype(BF16)
    bq_ref[0] = bq
    bql_ref[0] = bq
    o += qg
    pkv = _dot(h, w_ref[:, o:o + 2 * kg])
    kn = _head_rms(pkv[:, :kg], bdk_ref[...], kw_ref[...])
    kr = _rope(kn, cos2, sin2).astype(BF16)
    bk_ref[0] = _dot(kr, dup_ref[...]).astype(BF16)
    lane_v = lax.broadcasted_iota(I32, (1, 2 * kg), 1)
    ones_half = ((lane_v % LANE) >= HEAD_DIM).astype(F32)
    bv_ref[0] = (_dot(pkv[:, kg:].astype(BF16), dupv_ref[...]) + ones_half).astype(BF16)
    o += 2 * kg
    cu = _dot(h, w_ref[:, o:o + 3 * HY_WIDTH])
    cu_ref[0] = cu
    cul_ref[0] = cu
    o += 3 * HY_WIDTH
    g_ref[0] = jax.nn.sigmoid(_dot(h, w_ref[:, o:]) + bg_ref[...]).astype(BF16)


def _norm_inproj(xc, mod, norm_w, w_in_bf, b_gate, q_norm_w, k_norm_w, cos_t, sin_t, lc):
    b, t, d = xc.shape
    tm = 256
    na = NA_HEADS * HEAD_DIM
    qg = GQA_Q_HEADS * HEAD_DIM
    kg = GQA_KV_HEADS * HEAD_DIM
    ng = N_BRANCH * d
    nb = b
    bdq = jnp.asarray(np.kron(np.eye(GQA_Q_HEADS), np.ones((HEAD_DIM, HEAD_DIM))), BF16)
    bdk = jnp.asarray(np.kron(np.eye(GQA_KV_HEADS), np.ones((HEAD_DIM, HEAD_DIM))), BF16)
    dup_np = np.zeros((kg, 2 * kg), np.float32)
    dupv_np = np.zeros((kg, 2 * kg), np.float32)
    for g in range(GQA_KV_HEADS):
        for r in range(2):
            dup_np[g * HEAD_DIM:(g + 1) * HEAD_DIM, (2 * g + r) * HEAD_DIM:(2 * g + r + 1) * HEAD_DIM] = np.eye(HEAD_DIM)
        dupv_np[g * HEAD_DIM:(g + 1) * HEAD_DIM, 2 * g * HEAD_DIM:(2 * g + 1) * HEAD_DIM] = np.eye(HEAD_DIM)
    dup = jnp.asarray(dup_np, BF16)
    dupv = jnp.asarray(dupv_np, BF16)
    qw = jnp.tile(q_norm_w.astype(F32), GQA_Q_HEADS).reshape(1, qg)
    kw = jnp.tile(k_norm_w.astype(F32), GQA_KV_HEADS).reshape(1, kg)
    ctx_tiles = lc // tm
    full = lambda shape: pl.BlockSpec(shape, lambda bi, ti: (0,) * len(shape))
    tok = lambda width: pl.BlockSpec((1, tm, width), lambda bi, ti: (bi, ti, 0))
    lat_tok = lambda width: pl.BlockSpec((1, tm, width), lambda bi, ti: (bi, jnp.maximum(ti - ctx_tiles, 0), 0))
    cw = 3 * HY_WIDTH
    outs = [(na, BF16, t, tok), (na, BF16, t, tok), (na, BF16, t, tok), (qg, BF16, t, tok), (qg, BF16, t - lc, lat_tok),
            (2 * kg, BF16, t, tok), (2 * kg, BF16, t, tok), (cw, F32, t, tok), (cw, F32, t - lc, lat_tok), (ng, BF16, t, tok)]
    return pl.pallas_call(
        _inproj_body,
        grid=(b, t // tm),
        in_specs=[tok(d),
                  pl.BlockSpec((1, N_MOD, d), lambda bi, ti: (jnp.where(ti < ctx_tiles, nb, bi), 0, 0)),
                  full((1, d)), full(w_in_bf.shape), full((1, ng)), full((1, qg)), full((1, kg)),
                  pl.BlockSpec((tm, LANE), lambda bi, ti: (ti, 0)),
                  pl.BlockSpec((tm, LANE), lambda bi, ti: (ti, 0)),
                  full(bdq.shape), full(bdk.shape), full(dup.shape), full(dupv.shape)],
        out_specs=[spec(w) for w, _, _, spec in outs],
        out_shape=[jax.ShapeDtypeStruct((b, rows, w), dt) for w, dt, rows, _ in outs],
        compiler_params=_cp(("arbitrary", "arbitrary")),
        name="norm_inproj",
    )(xc, mod, norm_w.reshape(1, d), w_in_bf, b_gate.reshape(1, ng), qw, kw, cos_t, sin_t, bdq, bdk, dup, dupv)


def _rope_tables(lc, n):
    tpos = jnp.arange(n, dtype=I32)
    rows = (tpos // GRID_W).astype(F32)
    cols = (tpos % GRID_W).astype(F32)
    d_axis = HEAD_DIM // 2
    inv = ROPE_THETA ** (-jnp.arange(0, d_axis, 2, dtype=F32) / d_axis)
    ang = jnp.concatenate([rows[:, None] * inv, cols[:, None] * inv], axis=-1)
    cos, sin = jnp.cos(ang), jnp.sin(ang)
    cos_h = jnp.concatenate([cos, cos], axis=-1)
    sin_h = jnp.concatenate([-sin, sin], axis=-1)
    cos_h = jnp.concatenate([jnp.ones((lc, HEAD_DIM), F32), cos_h], axis=0)
    sin_h = jnp.concatenate([jnp.zeros((lc, HEAD_DIM), F32), sin_h], axis=0)
    return jnp.concatenate([cos_h, cos_h], axis=-1), jnp.concatenate([sin_h, sin_h], axis=-1)


NA_ROWS_PER_STEP = 4


def _na_bias_table(rpb):
    qc = np.arange(GRID_W)[:, None]
    kc = np.arange(GRID_W)[None, :]
    win_c0 = np.clip(qc - NA_WIN_C // 2, 0, GRID_W - NA_WIN_C)
    col_ok = (kc >= win_c0) & (kc < win_c0 + NA_WIN_C)
    d_col = np.clip(kc - qc + NA_WIN_C - 1, 0, 2 * NA_WIN_C - 2)
    r = rpb.astype(F32)
    dc = jnp.asarray(d_col)[None, :, None, :]
    cols = jnp.zeros((NA_HEADS, GRID_W, 2 * NA_WIN_R - 1, GRID_W), F32)
    for c in range(2 * NA_WIN_C - 1):
        cols = jnp.where(dc == c, r[:, None, :, c, None], cols)
    cols = jnp.where(jnp.asarray(col_ok)[None, :, None, :], cols, -1e30) * LOG2E
    tab = jnp.stack([cols[:, :, v:v + NA_WIN_R, :] for v in range(NA_WIN_R)], axis=0)
    return tab.reshape(NA_WIN_R, NA_HEADS * GRID_W, NA_WIN_R * GRID_W)


def _na_body(q_ref, k_ref, v_ref, bias_ref, o_ref, *, lc, n_rows):
    i = pl.program_id(1)
    nq = NA_HEADS * GRID_W
    lane_q = lax.broadcasted_iota(I32, (GRID_W, NA_HEADS * HEAD_DIM), 1) // HEAD_DIM
    lane_o = lane_q
    kctx = k_ref[0, 0:lc, :]
    vctx = v_ref[0, 0:lc, :]
    for j in range(NA_ROWS_PER_STEP):
        r = i * NA_ROWS_PER_STEP + j
        kr0 = jnp.clip(r - NA_WIN_R // 2, 0, n_rows - NA_WIN_R)
        variant = kr0 - r + NA_WIN_R - 1
        kstart = pl.multiple_of(lc + kr0 * GRID_W, GRID_W)
        kwin = k_ref[0, pl.ds(kstart, NA_WIN_R * GRID_W), :]
        vwin = v_ref[0, pl.ds(kstart, NA_WIN_R * GRID_W), :]
        q = q_ref[0, j * GRID_W:(j + 1) * GRID_W, :]
        qm = jnp.concatenate([jnp.where(lane_q == h, q, jnp.zeros_like(q)) for h in range(NA_HEADS)], axis=0)
        s_loc = _dot_nt(qm, kwin) + bias_ref[variant]
        s_ctx = _dot_nt(qm, kctx)
        m = jnp.maximum(jnp.max(s_loc, axis=-1, keepdims=True), jnp.max(s_ctx, axis=-1, keepdims=True))
        p_loc = jnp.exp2(s_loc - m)
        p_ctx = jnp.exp2(s_ctx - m)
        l = jnp.sum(p_loc, axis=-1, keepdims=True) + jnp.sum(p_ctx, axis=-1, keepdims=True)
        o = (_dot(p_loc.astype(BF16), vwin) + _dot(p_ctx.astype(BF16), vctx)) / l
        out = jnp.zeros((GRID_W, NA_HEADS * HEAD_DIM), F32)
        for h in range(NA_HEADS):
            out = jnp.where(lane_o == h, o[h * GRID_W:(h + 1) * GRID_W], out)
        o_ref[0, j * GRID_W:(j + 1) * GRID_W, :] = out.astype(o_ref.dtype)
    del nq


def _na_attention(aq, ak, av, bias_tab, lc):
    b, t, w = aq.shape
    n = t - lc
    n_rows = n // GRID_W
    assert n % GRID_W == 0 and n_rows >= NA_WIN_R and n_rows % NA_ROWS_PER_STEP == 0
    tq = NA_ROWS_PER_STEP * GRID_W
    assert lc % tq == 0
    off = lc // tq
    return pl.pallas_call(
        functools.partial(_na_body, lc=lc, n_rows=n_rows),
        grid=(b, n_rows // NA_ROWS_PER_STEP),
        in_specs=[pl.BlockSpec((1, tq, w), lambda bi, i: (bi, i + off, 0)),
                  pl.BlockSpec((1, t, w), lambda bi, i: (bi, 0, 0)),
                  pl.BlockSpec((1, t, w), lambda bi, i: (bi, 0, 0)),
                  pl.BlockSpec(bias_tab.shape, lambda bi, i: (0, 0, 0))],
        out_specs=pl.BlockSpec((1, tq, w), lambda bi, i: (bi, i, 0)),
        out_shape=jax.ShapeDtypeStruct((b, n, w), BF16),
        compiler_params=_cp(("arbitrary", "arbitrary")),
        name="na_attention",
    )(aq, ak, av, bias_tab)


PAIR_KEY_CHUNK = 256
PAIR_Q_TILE = 512
PAIR_STREAMS = 2


def _pair_attn_body(q_ref, k_ref, v_ref, o_ref, *, kc, shared_kv):
    tq = q_ref.shape[1]
    tk = k_ref.shape[1]
    hs = tq // PAIR_STREAMS
    lane = lax.broadcasted_iota(I32, (hs, LANE), 1)
    lo = lane < HEAD_DIM
    qqs = []
    for si in range(PAIR_STREAMS):
        q = q_ref[0, si * hs:(si + 1) * hs, :]
        zero = jnp.zeros_like(q)
        qqs.append(jnp.concatenate([jnp.where(lo, q, zero), jnp.where(lo, zero, q)], axis=0))
    n_chunks = tk // kc

    def step(c, carry):
        start = 0 if n_chunks == 1 else pl.multiple_of(c * kc, kc)
        kk = k_ref[0, pl.ds(start, kc), :]
        vv = v_ref[0, pl.ds(start, kc), :]
        out = []
        for qq, (m, l, acc) in zip(qqs, carry):
            s = _dot_nt(qq, kk)
            m_new = jnp.maximum(m, jnp.max(s, axis=-1, keepdims=True))
            alpha = jnp.exp2(m - m_new)
            p = jnp.exp2(s - m_new)
            if not shared_kv:
                l = alpha * l + jnp.sum(p, axis=-1, keepdims=True)
            acc = alpha * acc + _dot(p.astype(BF16), vv)
            out.append((m_new, l, acc))
        return tuple(out)

    one = (jnp.full((2 * hs, 1), -jnp.inf, F32), jnp.zeros((2 * hs, 1), F32), jnp.zeros((2 * hs, LANE), F32))
    init = (one,) * PAIR_STREAMS
    res = step(0, init) if n_chunks == 1 else lax.fori_loop(0, n_chunks, step, init, unroll=True)
    for si, (m, l, acc) in enumerate(res):
        a, bb = acc[:hs], acc[hs:]
        if shared_kv:
            oa = a / a[:, HEAD_DIM:HEAD_DIM + 1]
            ob = pltpu.roll(bb / bb[:, HEAD_DIM:HEAD_DIM + 1], HEAD_DIM, 1)
        else:
            oa = a / l[:hs]
            ob = bb / l[hs:]
        o_ref[0, si * hs:(si + 1) * hs, :] = jnp.where(lo, oa, ob).astype(o_ref.dtype)


def _pair_attention(q, k, v, *, q_row0, n_q, kv_of_pair, n_pairs, shared_kv):
    b = q.shape[0]
    tk = k.shape[1]
    tq = PAIR_Q_TILE if (q_row0 % PAIR_Q_TILE == 0 and n_q % PAIR_Q_TILE == 0) else 256
    assert q_row0 % tq == 0 and n_q % tq == 0
    kc = PAIR_KEY_CHUNK if tk % PAIR_KEY_CHUNK == 0 else tk
    off = q_row0 // tq
    return pl.pallas_call(
        functools.partial(_pair_attn_body, kc=kc, shared_kv=shared_kv),
        grid=(b, n_pairs, n_q // tq),
        in_specs=[pl.BlockSpec((1, tq, LANE), lambda bi, p, i: (bi, i + off, p)),
                  pl.BlockSpec((1, tk, LANE), lambda bi, p, i: (bi, 0, kv_of_pair(p))),
                  pl.BlockSpec((1, tk, LANE), lambda bi, p, i: (bi, 0, kv_of_pair(p)))],
        out_specs=pl.BlockSpec((1, tq, LANE), lambda bi, p, i: (bi, i, p)),
        out_shape=jax.ShapeDtypeStruct((b, n_q, n_pairs * LANE), BF16),
        compiler_params=_cp(("arbitrary", "arbitrary", "arbitrary")),
        name="pair_attention",
    )(q, k, v)


def _filter_body(w1t_ref, w1c_ref, w1s_ref, b1_ref, w2_ref, b2_ref, w3_ref, fr_ref, o_ref, *, length, rows, transposed):
    g = pl.program_id(0)
    n_total = 2 * length
    bands = (HY_EMB_DIM - 1) // 2
    f_col = 1e-4 + lax.broadcasted_iota(I32, (bands, 1), 0).astype(F32) * ((bands - 1 - 1e-4) / (bands - 1))
    nch = HY_ORDER * HY_WIDTH
    d_lo = math.log(HY_DECAY_TARGET) / HY_SLOW_DECAY_PCT
    d_hi = math.log(HY_DECAY_TARGET) / HY_FAST_DECAY_PCT
    deltas = d_lo + lax.broadcasted_iota(I32, (1, nch), 1).astype(F32) * ((d_hi - d_lo) / (nch - 1))
    freq = fr_ref[...]
    position = lambda tt: jnp.where(tt < length, tt, n_total - tt).astype(F32)
    pos_row = position(g * rows + lax.broadcasted_iota(I32, (1, rows), 1))
    arg = f_col * (pos_row * (2.0 * math.pi / length))
    pre = (w1t_ref[...] * (pos_row * (1.0 / (length - 1))) + _dot(w1c_ref[...], jnp.cos(arg), HI)
           - _dot(w1s_ref[...], jnp.sin(arg), HI))
    h1 = jnp.sin(freq * (pre + b1_ref[...]))
    h2 = jnp.sin(freq * (_dot(w2_ref[...], h1, HI) + b2_ref[...])).T
    sub = DFT_N2 if transposed else rows
    for j in range(rows // sub):
        tt = g * rows + j * sub + lax.broadcasted_iota(I32, (sub, 1), 0)
        tn = position(tt) * (1.0 / (length - 1))
        filt = _dot(h2[j * sub:(j + 1) * sub], w3_ref[0], HI) * jnp.exp(-tn * jnp.abs(deltas))
        filt = jnp.where(tt == length, 0.0, filt)
        if transposed:
            o_ref[:, j, :] = filt
        else:
            o_ref[...] = filt


def _hyena_filter(length, w1, b1, w2, b2, w3, freq, *, transposed):
    hid = w1.shape[1]
    nch = HY_ORDER * HY_WIDTH
    n_total = 2 * length
    bands = (HY_EMB_DIM - 1) // 2
    rows = SUBLANE * DFT_N2 if transposed else length
    n_steps = n_total // rows
    assert n_total % rows == 0 and n_steps % 2 == 0
    w3d = w3.reshape(hid, 2, nch).transpose(1, 0, 2)
    full = lambda shape: pl.BlockSpec(shape, lambda g: (0,) * len(shape))
    if transposed:
        out_spec = pl.BlockSpec((DFT_N2, SUBLANE, nch), lambda g: (0, g, 0))
        out_shape = jax.ShapeDtypeStruct((DFT_N2, n_total // DFT_N2, nch), F32)
    else:
        out_spec = pl.BlockSpec((rows, nch), lambda g: (g, 0))
        out_shape = jax.ShapeDtypeStruct((n_total, nch), F32)
    return pl.pallas_call(
        functools.partial(_filter_body, length=length, rows=rows, transposed=transposed),
        grid=(n_steps,),
        in_specs=[full((hid, 1)), full((hid, bands)), full((hid, bands)), full((hid, 1)), full((hid, hid)), full((hid, 1)),
                  pl.BlockSpec((1, hid, nch), lambda g: (g // (n_steps // 2), 0, 0)), full((hid, 1))],
        out_specs=out_spec,
        out_shape=out_shape,
        compiler_params=_cp(("arbitrary",)),
        name="hyena_filter",
    )(w1[0:1].T, w1[1:1 + bands].T, w1[1 + bands:].T, b1.reshape(hid, 1), w2.T, b2.reshape(hid, 1), w3d, freq.reshape(hid, 1))


def _dft_tables(n1, n1_used):
    n = n1 * DFT_N2
    n2 = np.arange(DFT_N2)[:, None, None]
    k1 = np.arange(n1)[None, :, None]
    nn1 = np.arange(n1_used)[None, None, :]
    ang = 2.0 * np.pi * ((k1 * (DFT_N2 * nn1 + n2)) % n) / n
    g_fwd = np.concatenate([np.cos(ang), -np.sin(ang)], axis=1)
    g_inv = np.transpose(g_fwd, (0, 2, 1)) / n
    kk = np.arange(DFT_N2)
    a2 = 2.0 * np.pi * ((kk[:, None] * kk[None, :]) % DFT_N2) / DFT_N2
    fr, fi = np.cos(a2), -np.sin(a2)
    f2 = np.block([[fr, -fi], [fi, fr]])
    f2c = np.block([[fr, fi], [-fi, fr]])
    cast = lambda a: jnp.asarray(a.astype(np.float32)).astype(BF16)
    return cast(g_fwd), cast(g_inv), cast(f2), cast(f2c)


def _stage1_body(x_ref, g_ref, o_ref, *, n1):
    for j in range(SUBLANE):
        res = _dot(g_ref[j], x_ref[0, j].astype(BF16))
        o_ref[0, 0, :, j, :] = res[:n1]
        o_ref[0, 1, :, j, :] = res[n1:]


def _dft_stage1(x_t, g_fwd):
    bx, _, n1u, c = x_t.shape
    n1 = g_fwd.shape[1] // 2
    return pl.pallas_call(
        functools.partial(_stage1_body, n1=n1),
        grid=(bx, DFT_N2 // SUBLANE),
        in_specs=[pl.BlockSpec((1, SUBLANE, n1u, c), lambda bi, g: (bi, g, 0, 0)),
                  pl.BlockSpec((SUBLANE, 2 * n1, n1u), lambda bi, g: (g, 0, 0))],
        out_specs=pl.BlockSpec((1, 2, n1, SUBLANE, c), lambda bi, g: (bi, 0, 0, g, 0)),
        out_shape=jax.ShapeDtypeStruct((bx, 2, n1, DFT_N2, c), F32),
        compiler_params=_cp(("arbitrary", "arbitrary")),
        name="dft_stage1",
    )(x_t, g_fwd)


def _stage2_spec_body(a_ref, f2_ref, o_ref):
    for j in range(SUBLANE):
        slab = jnp.concatenate([a_ref[0, 0, j], a_ref[0, 1, j]], axis=0).astype(BF16)
        o_ref[j] = _dot(f2_ref[...], slab).astype(o_ref.dtype)


def _dft_stage2_spectrum(a, f2):
    _, _, n1, _, c = a.shape
    return pl.pallas_call(
        _stage2_spec_body,
        grid=(n1 // SUBLANE,),
        in_specs=[pl.BlockSpec((1, 2, SUBLANE, DFT_N2, c), lambda g: (0, 0, g, 0, 0)),
                  pl.BlockSpec(f2.shape, lambda g: (0, 0))],
        out_specs=pl.BlockSpec((SUBLANE, 2 * DFT_N2, c), lambda g: (g, 0, 0)),
        out_shape=jax.ShapeDtypeStruct((n1, 2 * DFT_N2, c), BF16),
        compiler_params=_cp(("arbitrary",)),
        name="dft_stage2_spectrum",
    )(a, f2)


def _stage2_conv_body(a_ref, k_ref, f2_ref, f2c_ref, o_ref):
    h = DFT_N2
    for j in range(SUBLANE):
        slab = jnp.concatenate([a_ref[0, 0, j], a_ref[0, 1, j]], axis=0).astype(BF16)
        x = _dot(f2_ref[...], slab)
        kf = k_ref[j].astype(F32)
        xr, xi, kr, ki = x[:h], x[h:], kf[:h], kf[h:]
        y = jnp.concatenate([xr * kr - xi * ki, xr * ki + xi * kr], axis=0).astype(BF16)
        bv = _dot(f2c_ref[...], y)
        o_ref[0, 0, :, j, :] = bv[:h]
        o_ref[0, 1, :, j, :] = bv[h:]


def _dft_stage2_conv(a, kspec, order, f2, f2c):
    bx, _, n1, _, c = a.shape
    return pl.pallas_call(
        _stage2_conv_body,
        grid=(bx, n1 // SUBLANE),
        in_specs=[pl.BlockSpec((1, 2, SUBLANE, DFT_N2, c), lambda bi, g: (bi, 0, g, 0, 0)),
                  pl.BlockSpec((SUBLANE, 2 * DFT_N2, c), lambda bi, g: (g, 0, order)),
                  pl.BlockSpec(f2.shape, lambda bi, g: (0, 0)),
                  pl.BlockSpec(f2c.shape, lambda bi, g: (0, 0))],
        out_specs=pl.BlockSpec((1, 2, DFT_N2, SUBLANE, c), lambda bi, g: (bi, 0, 0, g, 0)),
        out_shape=jax.ShapeDtypeStruct((bx, 2, DFT_N2, n1, c), F32),
        compiler_params=_cp(("arbitrary", "arbitrary")),
        name="dft_stage2_conv",
    )(a, kspec, f2, f2c)


def _inverse_body(b_ref, g_ref, u_ref, gate_ref, bias_ref, *rest, last, n1):
    if last:
        (o_ref,) = rest
    else:
        gf_ref, o_ref, a_ref = rest
    for j in range(SUBLANE):
        slab = jnp.concatenate([b_ref[0, 0, j], b_ref[0, 1, j]], axis=0).astype(BF16)
        y = _dot(g_ref[j], slab)
        out = gate_ref[0, j] * (y + u_ref[0, j] * bias_ref[...])
        if last:
            o_ref[0, :, j, :] = out
        else:
            o_ref[0, j] = out
            res = _dot(gf_ref[j], out.astype(BF16))
            a_ref[0, 0, :, j, :] = res[:n1]
            a_ref[0, 1, :, j, :] = res[n1:]


def _dft_inverse(bv, g_inv, u_t, gate_t, bias, g_fwd, *, last):
    bx, _, _, n1, c = bv.shape
    n1u = g_inv.shape[1]
    t_spec = pl.BlockSpec((1, SUBLANE, n1u, c), lambda bi, g: (bi, g, 0, 0))
    in_specs = [pl.BlockSpec((1, 2, SUBLANE, n1, c), lambda bi, g: (bi, 0, g, 0, 0)),
                pl.BlockSpec((SUBLANE, n1u, 2 * n1), lambda bi, g: (g, 0, 0)),
                t_spec, t_spec,
                pl.BlockSpec((1, c), lambda bi, g: (0, 0))]
    args = [bv, g_inv, u_t, gate_t, bias.reshape(1, c)]
    if last:
        out_specs = pl.BlockSpec((1, n1u, SUBLANE, c), lambda bi, g: (bi, 0, g, 0))
        out_shape = jax.ShapeDtypeStruct((bx, n1u, DFT_N2, c), F32)
    else:
        in_specs.append(pl.BlockSpec((SUBLANE, 2 * n1, n1u), lambda bi, g: (g, 0, 0)))
        args.append(g_fwd)
        out_specs = [t_spec, pl.BlockSpec((1, 2, n1, SUBLANE, c), lambda bi, g: (bi, 0, 0, g, 0))]
        out_shape = [jax.ShapeDtypeStruct((bx, DFT_N2, n1u, c), F32), jax.ShapeDtypeStruct((bx, 2, n1, DFT_N2, c), F32)]
    return pl.pallas_call(
        functools.partial(_inverse_body, last=last, n1=n1),
        grid=(bx, DFT_N2 // SUBLANE),
        in_specs=in_specs,
        out_specs=out_specs,
        out_shape=out_shape,
        compiler_params=_cp(("arbitrary", "arbitrary")),
        name="dft_inverse",
    )(*args)


def _shortconv_body(prev_ref, cur_ref, next_ref, w_ref, b_ref, v_ref, x1_ref, x2_ref, *, n_steps):
    i = pl.program_id(1)
    u = cur_ref[0]
    rows = u.shape[0]
    prev_row = jnp.where(i > 0, prev_ref[0, SUBLANE - 1:SUBLANE, :], 0.0)
    next_row = jnp.where(i < n_steps - 1, next_ref[0, 0:1, :], 0.0)
    ridx = lax.broadcasted_iota(I32, u.shape, 0)
    up = jnp.where(ridx == 0, prev_row, pltpu.roll(u, 1, 0))
    dn = jnp.where(ridx == rows - 1, next_row, pltpu.roll(u, rows - 1, 0))
    z = up * w_ref[0:1] + u * w_ref[1:2] + dn * w_ref[2:3] + b_ref[...]
    cw = HY_WIDTH
    for j in range(rows // DFT_N2):
        zj = z[j * DFT_N2:(j + 1) * DFT_N2]
        v_ref[0, :, j, :] = zj[:, :cw]
        x1_ref[0, :, j, :] = zj[:, cw:2 * cw]
        x2_ref[0, :, j, :] = zj[:, 2 * cw:]


def _short_conv_t(cu, conv_w, conv_b):
    b, length, c3 = cu.shape
    rows = SUBLANE * DFT_N2
    assert length % rows == 0
    n_steps = length // rows
    rb = rows // SUBLANE
    n1u = length // DFT_N2
    out_spec = pl.BlockSpec((1, DFT_N2, SUBLANE, HY_WIDTH), lambda bi, i: (bi, 0, i, 0))
    out_shape = jax.ShapeDtypeStruct((b, DFT_N2, n1u, HY_WIDTH), F32)
    return pl.pallas_call(
        functools.partial(_shortconv_body, n_steps=n_steps),
        grid=(b, n_steps),
        in_specs=[pl.BlockSpec((1, SUBLANE, c3), lambda bi, i: (bi, jnp.maximum(i * rb - 1, 0), 0)),
                  pl.BlockSpec((1, rows, c3), lambda bi, i: (bi, i, 0)),
                  pl.BlockSpec((1, SUBLANE, c3), lambda bi, i: (bi, jnp.minimum((i + 1) * rb, n_steps * rb - 1), 0)),
                  pl.BlockSpec((3, c3), lambda bi, i: (0, 0)),
                  pl.BlockSpec((1, c3), lambda bi, i: (0, 0))],
        out_specs=[out_spec] * 3,
        out_shape=[out_shape] * 3,
        compiler_params=_cp(("arbitrary", "arbitrary")),
        name="hyena_short_conv",
    )(cu, cu, cu, conv_w, conv_b.reshape(1, c3))


def _hyena_long(cu_lat, conv_w, conv_b, filt_params, bias_d):
    b, length, _ = cu_lat.shape
    n1u = length // DFT_N2
    n1 = 2 * n1u
    g_sig, g_inv, f2, f2c = _dft_tables(n1, n1u)
    g_full = _dft_tables(n1, n1)[0]
    filt_t = _hyena_filter(length, *filt_params, transposed=True)
    kspec = _dft_stage2_spectrum(_dft_stage1(filt_t[None], g_full), f2)
    v_t, x1_t, x2_t = _short_conv_t(cu_lat, conv_w, conv_b)
    y = v_t
    a = _dft_stage1(y, g_sig)
    for o, gate in enumerate((x1_t, x2_t)):
        bv = _dft_stage2_conv(a, kspec, o, f2, f2c)
        if o == HY_ORDER - 1:
            y = _dft_inverse(bv, g_inv, y, gate, bias_d[o], g_sig, last=True)
        else:
            y, a = _dft_inverse(bv, g_inv, y, gate, bias_d[o], g_sig, last=False)
    return y.reshape(b, length, HY_WIDTH)


def _hyena_small_body(cu_ref, w_ref, b_ref, ks_ref, ff_ref, fi_ref, bias_ref, o_ref):
    u = cu_ref[0]
    rows = u.shape[0]
    ridx = lax.broadcasted_iota(I32, u.shape, 0)
    up = jnp.where(ridx == 0, 0.0, pltpu.roll(u, 1, 0))
    dn = jnp.where(ridx == rows - 1, 0.0, pltpu.roll(u, rows - 1, 0))
    z = up * w_ref[0:1] + u * w_ref[1:2] + dn * w_ref[2:3] + b_ref[...]
    cw = HY_WIDTH
    nf = ff_ref.shape[0] // 2
    y = z[:, :cw]
    for o in range(HY_ORDER):
        gate = z[:, (o + 1) * cw:(o + 2) * cw]
        x = _dot(ff_ref[...], y.astype(BF16))
        kf = ks_ref[:, o * cw:(o + 1) * cw]
        xr, xi, kr, ki = x[:nf], x[nf:], kf[:nf], kf[nf:]
        prod = jnp.concatenate([xr * kr - xi * ki, xr * ki + xi * kr], axis=0).astype(BF16)
        conv = _dot(fi_ref[...], prod)
        y = gate * (conv + y * bias_ref[o:o + 1])
    o_ref[0] = y


def _spectrum_small_body(ff_ref, f_ref, o_ref):
    o_ref[...] = _dot(ff_ref[...], f_ref[...].astype(BF16))


def _hyena_small(cu_ctx, conv_w, conv_b, filt_params, bias_d):
    b, length, c3 = cu_ctx.shape
    n = 2 * length
    kt = np.arange(n)[:, None] * np.arange(n)[None, :]
    ang = 2.0 * np.pi * (kt % n) / n
    fwd = np.concatenate([np.cos(ang), -np.sin(ang)], axis=0)
    cast = lambda a: jnp.asarray(a.astype(np.float32)).astype(BF16)
    f_full = cast(fwd)
    f_sig = cast(fwd[:, :length])
    f_inv = cast(np.transpose(fwd[:, :length]) / n)
    filt = _hyena_filter(length, *filt_params, transposed=False)
    nch = filt.shape[1]
    kspec = pl.pallas_call(
        _spectrum_small_body,
        out_shape=jax.ShapeDtypeStruct((2 * n, nch), F32),
        compiler_params=_cp(None),
        name="hyena_small_spectrum",
    )(f_full, filt)
    full = lambda shape: pl.BlockSpec(shape, lambda bi: (0,) * len(shape))
    return pl.pallas_call(
        _hyena_small_body,
        grid=(b,),
        in_specs=[pl.BlockSpec((1, length, c3), lambda bi: (bi, 0, 0)), full((3, c3)), full((1, c3)), full(kspec.shape),
                  full(f_sig.shape), full(f_inv.shape), full(bias_d.shape)],
        out_specs=pl.BlockSpec((1, length, HY_WIDTH), lambda bi: (bi, 0, 0)),
        out_shape=jax.ShapeDtypeStruct((b, length, HY_WIDTH), F32),
        compiler_params=_cp(("arbitrary",)),
        name="hyena_small",
    )(cu_ctx, conv_w, conv_b.reshape(1, c3), kspec, f_sig, f_inv, bias_d)


def _merge_body(x_ref, mod_ref, yac_ref, yal_ref, ybc_ref, ybl_ref, ycc_ref, ycl_ref, g_ref, wa_ref, wb_ref, wc_ref, wo_ref,
                nw_ref, rwh_ref, rwl_ref, xo_ref, h_ref, lg_ref, *, ctx_tiles):
    d = x_ref.shape[2]
    is_ctx = pl.program_id(1) < ctx_tiles
    ya = jnp.where(is_ctx, yac_ref[0], yal_ref[0])
    yb = jnp.where(is_ctx, ybc_ref[0], ybl_ref[0])
    yc = jnp.where(is_ctx, ycc_ref[0], ycl_ref[0])
    g = g_ref[0].astype(F32)
    m = (g[:, :d] * _dot(ya, wa_ref[...]) + g[:, d:2 * d] * _dot(yb, wb_ref[...])
         + g[:, 2 * d:] * _dot(yc.astype(BF16), wc_ref[...]))
    y = _dot(m.astype(BF16), wo_ref[...])
    mod = mod_ref[0]
    x = x_ref[0] + mod[2:3] * y
    xo_ref[0] = x
    hn = x * lax.rsqrt(jnp.mean(x * x, axis=-1, keepdims=True) + RMS_EPS) * nw_ref[...]
    h = hn * (1.0 + mod[4:5]) + mod[3:4]
    h_ref[...] = h
    h_hi = h.astype(BF16)
    h_lo = (h - h_hi.astype(F32)).astype(BF16)
    lg_ref[...] = _dot(h_hi, rwh_ref[...]) + _dot(h_hi, rwl_ref[...]) + _dot(h_lo, rwh_ref[...])


def _merge(xc, mod, ya, yb, yc, g, w_a, w_b, w_c, w_o, norm2_w, router_w, lc):
    b, t, d = xc.shape
    tm = 256
    nb = b
    ctx_tiles = lc // tm
    nt = t // tm
    ne = router_w.shape[1]
    rw = jnp.pad(router_w.astype(F32), ((0, 0), (0, LANE - ne)))
    rw_hi = rw.astype(BF16)
    rw_lo = (rw - rw_hi.astype(F32)).astype(BF16)
    full = lambda shape: pl.BlockSpec(shape, lambda bi, ti: (0,) * len(shape))
    tok = lambda width: pl.BlockSpec((1, tm, width), lambda bi, ti: (bi, ti, 0))
    ctx_tok = lambda width: pl.BlockSpec((1, tm, width), lambda bi, ti: (bi, jnp.minimum(ti, ctx_tiles - 1), 0))
    lat_tok = lambda width: pl.BlockSpec((1, tm, width), lambda bi, ti: (bi, jnp.maximum(ti - ctx_tiles, 0), 0))
    branch_specs = []
    for pair in (ya, yb, yc):
        branch_specs += [ctx_tok(pair[0].shape[2]), lat_tok(pair[1].shape[2])]
    return pl.pallas_call(
        functools.partial(_merge_body, ctx_tiles=ctx_tiles),
        grid=(b, nt),
        in_specs=[tok(d),
                  pl.BlockSpec((1, N_MOD, d), lambda bi, ti: (jnp.where(ti < ctx_tiles, nb, bi), 0, 0))]
                 + branch_specs
                 + [tok(g.shape[2]), full(w_a.shape), full(w_b.shape), full(w_c.shape), full(w_o.shape), full((1, d)),
                    full(rw_hi.shape), full(rw_lo.shape)],
        out_specs=[tok(d),
                   pl.BlockSpec((tm, d), lambda bi, ti: (bi * nt + ti, 0)),
                   pl.BlockSpec((tm, LANE), lambda bi, ti: (bi * nt + ti, 0))],
        out_shape=[jax.ShapeDtypeStruct((b, t, d), F32),
                   jax.ShapeDtypeStruct((b * t, d), F32),
                   jax.ShapeDtypeStruct((b * t, LANE), F32)],
        compiler_params=_cp(("arbitrary", "arbitrary")),
        name="merge_outproj",
    )(xc, mod, ya[0], ya[1], yb[0], yb[1], yc[0], yc[1], g, w_a, w_b, w_c, w_o, norm2_w.reshape(1, d), rw_hi, rw_lo)


ROUTE_TILE = 512
MOE_ROW_TILE = 512
MOE_TOKEN_TILE = 256


def _top2_of4(a):
    m1 = jnp.maximum(jnp.maximum(a[0], a[1]), jnp.maximum(a[2], a[3]))
    i1 = jnp.where(a[0] == m1, 0, jnp.where(a[1] == m1, 1, jnp.where(a[2] == m1, 2, 3)))
    neg = jnp.full_like(m1, -jnp.inf)
    r = [jnp.where(i1 == j, neg, a[j]) for j in range(4)]
    m2 = jnp.maximum(jnp.maximum(r[0], r[1]), jnp.maximum(r[2], r[3]))
    i2 = jnp.where((r[0] == m2) & (i1 != 0), 0,
                   jnp.where((r[1] == m2) & (i1 != 1), 1, jnp.where((r[2] == m2) & (i1 != 2), 2, 3)))
    return m1, m2, i1, i2


def _route_body(lg_ref, rb_ref, tri_ref, e_ref, w_ref, rank_ref, cnt_ref, carry_ref):
    i = pl.program_id(0)

    @pl.when(i == 0)
    def _():
        carry_ref[...] = jnp.zeros_like(carry_ref)

    scores = jax.nn.sigmoid(lg_ref[...])
    sel = scores + rb_ref[...]
    rows = [sel[e:e + 1] for e in range(N_EXPERTS)]
    tops = [_top2_of4(rows[4 * g:4 * g + 4]) for g in range(N_GROUPS)]
    gs = [t[0] + t[1] for t in tops]
    gmax = jnp.maximum(jnp.maximum(gs[0], gs[1]), jnp.maximum(gs[2], gs[3]))
    gi = jnp.where(gs[0] == gmax, 0, jnp.where(gs[1] == gmax, 1, jnp.where(gs[2] == gmax, 2, 3)))
    pick = lambda k: jnp.where(gi == 0, tops[0][k], jnp.where(gi == 1, tops[1][k], jnp.where(gi == 2, tops[2][k], tops[3][k])))
    e0 = gi * EXPERTS_PER_GROUP + pick(2)
    e1 = gi * EXPERTS_PER_GROUP + pick(3)
    eid = lax.broadcasted_iota(I32, scores.shape, 0)
    oh0 = eid == e0
    oh1 = eid == e1
    s0 = jnp.sum(jnp.where(oh0, scores, 0.0), axis=0, keepdims=True)
    s1 = jnp.sum(jnp.where(oh1, scores, 0.0), axis=0, keepdims=True)
    tot = s0 + s1
    e_ref[...] = jnp.concatenate([e0, e1], axis=0)
    w_ref[...] = jnp.concatenate([s0 / tot, s1 / tot], axis=0)
    oh = (oh0 | oh1).astype(BF16)
    incl = _dot(oh, tri_ref[...]) + carry_ref[...]
    excl = incl - 1.0
    r0 = jnp.sum(jnp.where(oh0, excl, 0.0), axis=0, keepdims=True)
    r1 = jnp.sum(jnp.where(oh1, excl, 0.0), axis=0, keepdims=True)
    rank_ref[...] = jnp.concatenate([r0, r1], axis=0).astype(I32)
    carry_ref[...] = incl[:, ROUTE_TILE - 1:ROUTE_TILE]
    cnt_ref[...] = jnp.broadcast_to(incl[:, ROUTE_TILE - 1:ROUTE_TILE], cnt_ref.shape).astype(I32)


def _route(logits_t, router_b):
    ne, t = logits_t.shape
    tl = ROUTE_TILE
    assert t % tl == 0
    tri = jnp.asarray(np.triu(np.ones((tl, tl), np.float32)), BF16)
    two = lambda dt: jax.ShapeDtypeStruct((2, t), dt)
    return pl.pallas_call(
        _route_body,
        grid=(t // tl,),
        in_specs=[pl.BlockSpec((ne, tl), lambda i: (0, i)), pl.BlockSpec((ne, 1), lambda i: (0, 0)),
                  pl.BlockSpec((tl, tl), lambda i: (0, 0))],
        out_specs=[pl.BlockSpec((2, tl), lambda i: (0, i))] * 3 + [pl.BlockSpec((ne, LANE), lambda i: (0, 0))],
        out_shape=[two(I32), two(F32), two(I32), jax.ShapeDtypeStruct((ne, LANE), I32)],
        scratch_shapes=[pltpu.VMEM((ne, 1), F32)],
        compiler_params=_cp(("arbitrary",)),
        name="moe_route",
    )(logits_t, router_b.reshape(ne, 1).astype(F32), tri)


def _dispatch_body(slot_ref, pad_ref, h_ref, xs_ref, sem, zbuf, zsem):
    tm = h_ref.shape[0]
    n_exp = (pad_ref.shape[1] - 1) // 2
    pieces = [1 << b for b in reversed(range(3, MOE_ROW_TILE.bit_length() - 1))]

    def zero_fill(wait):
        def go(cp, cond):
            @pl.when(cond)
            def _():
                cp.wait() if wait else cp.start()

        for e in range(n_exp):
            first = pad_ref[0, e]
            n_pad = pad_ref[0, n_exp + e]
            lead = (-first) & (SUBLANE - 1)
            for r in range(SUBLANE - 1):
                go(pltpu.make_async_copy(zbuf.at[pl.ds(0, 1), :], xs_ref.at[pl.ds(first + r, 1), :], zsem), r < lead)
            cur = pl.multiple_of(first + lead, SUBLANE)
            rem = n_pad - lead
            for s in pieces:
                go(pltpu.make_async_copy(zbuf.at[pl.ds(0, s), :], xs_ref.at[pl.ds(cur, s), :], zsem), (rem & s) != 0)
                cur = pl.multiple_of(cur + (rem & s), SUBLANE)
        zrows = zbuf.shape[0]
        tail = pad_ref[0, 2 * n_exp]
        for k in range(n_exp * MOE_ROW_TILE // zrows):
            row = pl.multiple_of(tail + k * zrows, zrows)
            go(pltpu.make_async_copy(zbuf, xs_ref.at[pl.ds(row, zrows), :], zsem), row < xs_ref.shape[0])

    @pl.when(pl.program_id(0) == 0)
    def _():
        zbuf[...] = jnp.zeros_like(zbuf)
        zero_fill(False)
        zero_fill(True)

    def copy(j, k):
        return pltpu.make_async_copy(h_ref.at[pl.ds(j, 1), :], xs_ref.at[pl.ds(slot_ref[0, 0, k * tm + j], 1), :], sem)

    def issue(j, c):
        copy(j, 0).start(priority=0)
        copy(j, 1).start(priority=1)
        return c

    def drain(j, c):
        copy(j, 0).wait()
        copy(j, 1).wait()
        return c

    lax.fori_loop(0, tm, issue, 0, unroll=8)
    lax.fori_loop(0, tm, drain, 0, unroll=8)


def _dispatch(h2, slots3, pad_info, s_pad):
    t, d = h2.shape
    tm = MOE_TOKEN_TILE
    return pl.pallas_call(
        _dispatch_body,
        grid=(t // tm,),
        in_specs=[pl.BlockSpec((1, 1, 2 * tm), lambda i: (i, 0, 0), memory_space=pltpu.SMEM),
                  pl.BlockSpec(pad_info.shape, lambda i: (0, 0), memory_space=pltpu.SMEM),
                  pl.BlockSpec((tm, d), lambda i: (i, 0))],
        out_specs=pl.BlockSpec(memory_space=pl.ANY),
        out_shape=jax.ShapeDtypeStruct((s_pad, d), F32),
        scratch_shapes=[pltpu.SemaphoreType.DMA(()), pltpu.VMEM((MOE_ROW_TILE // 2, d), F32), pltpu.SemaphoreType.DMA(())],
        compiler_params=pltpu.CompilerParams(dimension_semantics=("arbitrary",), has_side_effects=True,
                                             vmem_limit_bytes=VMEM_LIMIT),
        name="moe_dispatch",
    )(slots3, pad_info, h2)


def _ffn_body(te_ref, nu_ref, x_ref, wg_ref, wu_ref, wd_ref, o_ref):
    i = pl.program_id(0)

    @pl.when(i < nu_ref[0])
    def _():
        x = x_ref[...].astype(BF16)
        hmid = (_silu(_dot_nt(x, wg_ref[0, 0].astype(BF16))) * _dot_nt(x, wu_ref[0, 0].astype(BF16))).astype(BF16)
        o_ref[...] = _dot(hmid, wd_ref[0, 0].astype(BF16))

    @pl.when(i >= nu_ref[0])
    def _():
        o_ref[...] = jnp.zeros_like(o_ref)


def _grouped_ffn(xs, tile_expert, n_used, wg, wu, wd, layer):
    s_pad, d = xs.shape
    tr = MOE_ROW_TILE
    n_tiles = s_pad // tr
    ff = wg.shape[2]
    grid_spec = pltpu.PrefetchScalarGridSpec(
        num_scalar_prefetch=2,
        grid=(n_tiles,),
        in_specs=[pl.BlockSpec((tr, d), lambda i, te, nu: (jnp.minimum(i, nu[0] - 1), 0)),
                  pl.BlockSpec((1, 1, ff, d), lambda i, te, nu: (layer, te[i], 0, 0)),
                  pl.BlockSpec((1, 1, ff, d), lambda i, te, nu: (layer, te[i], 0, 0)),
                  pl.BlockSpec((1, 1, ff, d), lambda i, te, nu: (layer, te[i], 0, 0))],
        out_specs=pl.BlockSpec((tr, d), lambda i, te, nu: (i, 0)),
    )
    return pl.pallas_call(
        _ffn_body,
        grid_spec=grid_spec,
        out_shape=jax.ShapeDtypeStruct((s_pad, d), F32),
        compiler_params=_cp(("arbitrary",)),
        name="moe_grouped_ffn",
    )(tile_expert, n_used, xs, wg, wu, wd)


def _combine_body(slot_ref, x_ref, mod_ref, w_ref, nw_ref, ys_ref, o_ref, ybuf, sem, *, final):
    tm = x_ref.shape[1]

    def copy(j, k):
        return pltpu.make_async_copy(ys_ref.at[pl.ds(slot_ref[0, 0, k * tm + j], 1), :], ybuf.at[k, pl.ds(j, 1), :], sem)

    def issue(j, c):
        copy(j, 0).start(priority=0)
        copy(j, 1).start(priority=1)
        return c

    def drain(j, c):
        copy(j, 0).wait()
        copy(j, 1).wait()
        return c

    lax.fori_loop(0, tm, issue, 0, unroll=8)
    lax.fori_loop(0, tm, drain, 0, unroll=8)
    w0 = w_ref[:, 0:1]
    w1 = w_ref[:, 1:2]
    y = w0 * ybuf[0] + w1 * ybuf[1]
    x = x_ref[0] + mod_ref[0][5:6] * y
    if final:
        x = x * lax.rsqrt(jnp.mean(x * x, axis=-1, keepdims=True) + RMS_EPS) * nw_ref[...]
    o_ref[0] = x


def _moe_combine(xc, mod, ys, slots3, w_t, final_w, lc, *, final):
    b, t, d = xc.shape
    tm = MOE_TOKEN_TILE
    nb = b
    ctx_tiles = lc // tm
    nt = t // tm
    row0 = ctx_tiles if final else 0
    n_out = t - row0 * tm
    return pl.pallas_call(
        functools.partial(_combine_body, final=final),
        grid=(b, nt - row0),
        in_specs=[pl.BlockSpec((1, 1, 2 * tm), lambda bi, ti: (bi * nt + ti + row0, 0, 0), memory_space=pltpu.SMEM),
                  pl.BlockSpec((1, tm, d), lambda bi, ti: (bi, ti + row0, 0)),
                  pl.BlockSpec((1, N_MOD, d), lambda bi, ti: (jnp.where(ti + row0 < ctx_tiles, nb, bi), 0, 0)),
                  pl.BlockSpec((tm, 2), lambda bi, ti: (bi * nt + ti + row0, 0)),
                  pl.BlockSpec((1, d), lambda bi, ti: (0, 0)),
                  pl.BlockSpec(memory_space=pl.ANY)],
        out_specs=pl.BlockSpec((1, tm, d), lambda bi, ti: (bi, ti, 0)),
        out_shape=jax.ShapeDtypeStruct((b, n_out, d), F32),
        scratch_shapes=[pltpu.VMEM((2, tm, d), F32), pltpu.SemaphoreType.DMA(())],
        compiler_params=_cp(("arbitrary", "arbitrary")),
        name="moe_combine",
    )(slots3, xc, mod, w_t, final_w.reshape(1, d), ys)


def _moe_experts(h2, logits_t, router_b, wg, wu, wd, layer):
    t = h2.shape[0]
    tr = MOE_ROW_TILE
    tm = MOE_TOKEN_TILE
    e_idx, w_tok, rank, counts = _route(logits_t, router_b)
    counts = counts[:, 0]
    padded = ((counts + tr - 1) // tr) * tr
    ends = jnp.cumsum(padded)
    starts = ends - padded
    start_of = jnp.zeros_like(rank)
    for e in range(N_EXPERTS):
        start_of = jnp.where(e_idx == e, starts[e], start_of)
    slots = start_of + rank
    slots3 = slots.reshape(2, t // tm, tm).transpose(1, 0, 2).reshape(t // tm, 1, 2 * tm)
    s_pad = 2 * t + N_EXPERTS * tr
    n_tiles = s_pad // tr
    tile_start = jnp.arange(n_tiles, dtype=I32) * tr
    n_used = (ends[-1] // tr).astype(I32)
    tile_expert = jnp.sum((ends[None, :] <= tile_start[:, None]).astype(I32), axis=1)
    last_used = jnp.sum((ends <= (n_used - 1) * tr).astype(I32))
    tile_expert = jnp.minimum(jnp.where(jnp.arange(n_tiles) < n_used, tile_expert, last_used), N_EXPERTS - 1).astype(I32)
    pad_info = jnp.concatenate([starts + counts, padded - counts, ends[-1:]]).astype(I32).reshape(1, 2 * N_EXPERTS + 1)
    xs = _dispatch(h2, slots3, pad_info, s_pad)
    ys = _grouped_ffn(xs, tile_expert, n_used.reshape(1), wg, wu, wd, layer)
    return ys, slots3, w_tok.T


def kernel(x, c, ctx, c_ctx, ada_w, ada_b, norm1_w, norm2_w, w_in, b_gate, na_rpb, q_norm_w, k_norm_w, hy_conv_w, hy_conv_b,
           hy_f_w1, hy_f_b1, hy_f_w2, hy_f_b2, hy_f_w3, hy_sin_freq, hy_bias_d, w_br_a, w_br_b, w_br_c, w_out, router_w,
           router_b, exp_w_gate, exp_w_up, exp_w_down, final_norm_w):
    b, n, d = x.shape
    lc = ctx.shape[1]
    depth = ada_w.shape[0]
    t = lc + n
    cos_t, sin_t = _rope_tables(lc, n)
    xc = jnp.concatenate([ctx, x], axis=1)
    cvec = jnp.concatenate([c, c_ctx[None], jnp.zeros((SUBLANE - b - 1, d), F32)], axis=0)
    na_w = NA_HEADS * HEAD_DIM
    exp_w_gate_t = jnp.swapaxes(exp_w_gate, 2, 3)
    exp_w_up_t = jnp.swapaxes(exp_w_up, 2, 3)
    out = None
    for layer in range(depth):
        mod = _ada_mod(cvec, ada_w, ada_b, layer)[:b + 1].reshape(b + 1, N_MOD, d)
        aq, ak, av, bq, bq_lat, bk, bv, cu, cu_lat, g = _norm_inproj(xc, mod, norm1_w[layer], w_in[layer].astype(BF16),
                                                                     b_gate[layer], q_norm_w[layer], k_norm_w[layer],
                                                                     cos_t, sin_t, lc)
        bq_ctx = bq[:, :lc]
        cu_ctx = cu[:, :lc]
        filt_params = (hy_f_w1[layer], hy_f_b1[layer], hy_f_w2[layer], hy_f_b2[layer], hy_f_w3[layer], hy_sin_freq[layer])
        ya_lat = _na_attention(aq, ak, av, _na_bias_table(na_rpb[layer]), lc)
        ya_ctx = _pair_attention(aq, ak[:, :lc], av[:, :lc], q_row0=0, n_q=lc, kv_of_pair=lambda p: p, n_pairs=na_w // LANE,
                                 shared_kv=False)
        gq = GQA_Q_HEADS * HEAD_DIM // LANE
        yb_lat = _pair_attention(bq_lat, bk, bv, q_row0=0, n_q=n, kv_of_pair=lambda p: p // 2, n_pairs=gq, shared_kv=True)
        yb_ctx = _pair_attention(bq_ctx, bk[:, :lc], bv[:, :lc], q_row0=0, n_q=lc, kv_of_pair=lambda p: p // 2, n_pairs=gq,
                                 shared_kv=True)
        yc_lat = _hyena_long(cu_lat, hy_conv_w[layer], hy_conv_b[layer], filt_params, hy_bias_d[layer])
        yc_ctx = _hyena_small(cu_ctx, hy_conv_w[layer], hy_conv_b[layer], filt_params, hy_bias_d[layer])
        xc, h2, logits = _merge(xc, mod, (ya_ctx, ya_lat), (yb_ctx, yb_lat), (yc_ctx, yc_lat), g, w_br_a[layer].astype(BF16),
                                w_br_b[layer].astype(BF16), w_br_c[layer].astype(BF16), w_out[layer].astype(BF16),
                                norm2_w[layer], router_w, lc)
        logits_t = logits[:, :N_EXPERTS].T
        ys, slots3, w_t = _moe_experts(h2, logits_t, router_b, exp_w_gate_t, exp_w_up_t, exp_w_down, layer)
        final = layer == depth - 1
        res = _moe_combine(xc, mod, ys, slots3, w_t, final_norm_w, lc, final=final)
        if final:
            out = res
        else:
            xc = res
    return out
```

```python
import functools
import math

import jax
import jax.numpy as jnp
import numpy as np
from jax import lax
from jax.experimental import pallas as pl
from jax.experimental.pallas import tpu as pltpu

F32 = jnp.float32
BF16 = jnp.bfloat16
I32 = jnp.int32
HI = lax.Precision.HIGHEST

GRID_W = 64
HEAD_DIM = 64
RMS_EPS = 1e-6
N_MOD = 6
NA_HEADS = 4
NA_WIN_R = 8
NA_WIN_C = 16
GQA_Q_HEADS = 8
GQA_KV_HEADS = 2
ROPE_THETA = 10000.0
HY_WIDTH = 256
HY_ORDER = 2
HY_EMB_DIM = 33
HY_DECAY_TARGET = 1e-2
HY_FAST_DECAY_PCT = 0.3
HY_SLOW_DECAY_PCT = 1.5
N_BRANCH = 3
N_EXPERTS = 16
N_GROUPS = 4
EXPERTS_PER_GROUP = 4
LANE = 128
SUBLANE = 8
DFT_N2 = 128
LOG2E = math.log2(math.e)
VMEM_LIMIT = 56 * 1024 * 1024


def _cp(sem, vmem=VMEM_LIMIT, flags=None):
    return pltpu.CompilerParams(dimension_semantics=sem, vmem_limit_bytes=vmem, flags=flags)


def _dot(a, b, prec=None):
    return jnp.dot(a, b, preferred_element_type=F32, precision=prec)


def _dot_nt(a, b, prec=None):
    return lax.dot_general(a, b, (((1,), (1,)), ((), ())), preferred_element_type=F32, precision=prec)


def _silu(x):
    return x * jax.nn.sigmoid(x)


def _ada_body(c_ref, w_ref, b_ref, o_ref):
    o_ref[...] = _dot(_silu(c_ref[...]), w_ref[0], HI) + b_ref[0]


def _ada_mod(cvec, ada_w, ada_b, layer):
    rows, d = cvec.shape
    depth, _, cols = ada_w.shape
    tn = 1536
    return pl.pallas_call(
        _ada_body,
        grid=(cols // tn,),
        in_specs=[pl.BlockSpec((rows, d), lambda j: (0, 0)),
                  pl.BlockSpec((1, d, tn), lambda j: (layer, 0, j)),
                  pl.BlockSpec((1, 1, tn), lambda j: (layer, 0, j))],
        out_specs=pl.BlockSpec((rows, tn), lambda j: (0, j)),
        out_shape=jax.ShapeDtypeStruct((rows, cols), F32),
        compiler_params=_cp(("arbitrary",)),
        name="ada_mod",
    )(cvec, ada_w, ada_b.reshape(depth, 1, cols))


def _head_rms(x, ones_bd, w_row):
    x2 = x * x
    hi = x2.astype(BF16)
    lo = (x2 - hi.astype(F32)).astype(BF16)
    ss = _dot(hi, ones_bd) + _dot(lo, ones_bd)
    return x * lax.rsqrt(ss * (1.0 / HEAD_DIM) + RMS_EPS) * w_row


def _rope(x, cos_t, sin_t):
    n = x.shape[1]
    half = HEAD_DIM // 2
    lane = lax.broadcasted_iota(I32, x.shape, 1)
    first = (lane % HEAD_DIM) < half
    swapped = jnp.where(first, pltpu.roll(x, n - half, 1), pltpu.roll(x, half, 1))
    return x * cos_t + swapped * sin_t


def _inproj_body(x_ref, mod_ref, nw_ref, w_ref, bg_ref, qw_ref, kw_ref, cos_ref, sin_ref, bdq_ref, bdk_ref, dup_ref, dupv_ref,
                 aq_ref, ak_ref, av_ref, bq_ref, bql_ref, bk_ref, bv_ref, cu_ref, cul_ref, g_ref):
    x = x_ref[0]
    mod = mod_ref[0]
    y = x * lax.rsqrt(jnp.mean(x * x, axis=-1, keepdims=True) + RMS_EPS) * nw_ref[...]
    h = (y * (1.0 + mod[1:2]) + mod[0:1]).astype(BF16)
    scale = HEAD_DIM ** -0.5 * LOG2E
    na = NA_HEADS * HEAD_DIM
    qg = GQA_Q_HEADS * HEAD_DIM
    kg = GQA_KV_HEADS * HEAD_DIM
    o = 0
    pa = _dot(h, w_ref[:, o:o + 3 * na])
    aq_ref[0] = (pa[:, :na] * scale).astype(BF16)
    ak_ref[0] = pa[:, na:2 * na].astype(BF16)
    av_ref[0] = pa[:, 2 * na:].astype(BF16)
    o += 3 * na
    cos2 = cos_ref[...]
    sin2 = sin_ref[...]
    pq = _dot(h, w_ref[:, o:o + qg])
    qn = _head_rms(pq, bdq_ref[...], qw_ref[...])
    reps = qg // LANE
    qr = _rope(qn, jnp.concatenate([cos2] * reps, axis=1), jnp.concatenate([sin2] * reps, axis=1))
    bq = (qr * scale).astype(BF16)
    bq_ref[0] = bq
    bql_ref[0] = bq
    o += qg
    pkv = _dot(h, w_ref[:, o:o + 2 * kg])
    kn = _head_rms(pkv[:, :kg], bdk_ref[...], kw_ref[...])
    kr = _rope(kn, cos2, sin2).astype(BF16)
    bk_ref[0] = _dot(kr, dup_ref[...]).astype(BF16)
    lane_v = lax.broadcasted_iota(I32, (1, 2 * kg), 1)
    ones_half = ((lane_v % LANE) >= HEAD_DIM).astype(F32)
    bv_ref[0] = (_dot(pkv[:, kg:].astype(BF16), dupv_ref[...]) + ones_half).astype(BF16)
    o += 2 * kg
    cu = _dot(h, w_ref[:, o:o + 3 * HY_WIDTH])
    cu_ref[0] = cu
    cul_ref[0] = cu
    o += 3 * HY_WIDTH
    g_ref[0] = jax.nn.sigmoid(_dot(h, w_ref[:, o:]) + bg_ref[...]).astype(BF16)


def _norm_inproj(xc, mod, norm_w, w_in_bf, b_gate, q_norm_w, k_norm_w, cos_t, sin_t, lc):
    b, t, d = xc.shape
    tm = 256
    na = NA_HEADS * HEAD_DIM
    qg = GQA_Q_HEADS * HEAD_DIM
    kg = GQA_KV_HEADS * HEAD_DIM
    ng = N_BRANCH * d
    nb = b
    bdq = jnp.asarray(np.kron(np.eye(GQA_Q_HEADS), np.ones((HEAD_DIM, HEAD_DIM))), BF16)
    bdk = jnp.asarray(np.kron(np.eye(GQA_KV_HEADS), np.ones((HEAD_DIM, HEAD_DIM))), BF16)
    dup_np = np.zeros((kg, 2 * kg), np.float32)
    dupv_np = np.zeros((kg, 2 * kg), np.float32)
    for g in range(GQA_KV_HEADS):
        for r in range(2):
            dup_np[g * HEAD_DIM:(g + 1) * HEAD_DIM, (2 * g + r) * HEAD_DIM:(2 * g + r + 1) * HEAD_DIM] = np.eye(HEAD_DIM)
        dupv_np[g * HEAD_DIM:(g + 1) * HEAD_DIM, 2 * g * HEAD_DIM:(2 * g + 1) * HEAD_DIM] = np.eye(HEAD_DIM)
    dup = jnp.asarray(dup_np, BF16)
    dupv = jnp.asarray(dupv_np, BF16)
    qw = jnp.tile(q_norm_w.astype(F32), GQA_Q_HEADS).reshape(1, qg)
    kw = jnp.tile(k_norm_w.astype(F32), GQA_KV_HEADS).reshape(1, kg)
    ctx_tiles = lc // tm
    full = lambda shape: pl.BlockSpec(shape, lambda bi, ti: (0,) * len(shape))
    tok = lambda width: pl.BlockSpec((1, tm, width), lambda bi, ti: (bi, ti, 0))
    lat_tok = lambda width: pl.BlockSpec((1, tm, width), lambda bi, ti: (bi, jnp.maximum(ti - ctx_tiles, 0), 0))
    cw = 3 * HY_WIDTH
    outs = [(na, BF16, t, tok), (na, BF16, t, tok), (na, BF16, t, tok), (qg, BF16, t, tok), (qg, BF16, t - lc, lat_tok),
            (2 * kg, BF16, t, tok), (2 * kg, BF16, t, tok), (cw, F32, t, tok), (cw, F32, t - lc, lat_tok), (ng, BF16, t, tok)]
    return pl.pallas_call(
        _inproj_body,
        grid=(b, t // tm),
        in_specs=[tok(d),
                  pl.BlockSpec((1, N_MOD, d), lambda bi, ti: (jnp.where(ti < ctx_tiles, nb, bi), 0, 0)),
                  full((1, d)), full(w_in_bf.shape), full((1, ng)), full((1, qg)), full((1, kg)),
                  pl.BlockSpec((tm, LANE), lambda bi, ti: (ti, 0)),
                  pl.BlockSpec((tm, LANE), lambda bi, ti: (ti, 0)),
                  full(bdq.shape), full(bdk.shape), full(dup.shape), full(dupv.shape)],
        out_specs=[spec(w) for w, _, _, spec in outs],
        out_shape=[jax.ShapeDtypeStruct((b, rows, w), dt) for w, dt, rows, _ in outs],
        compiler_params=_cp(("arbitrary", "arbitrary")),
        name="norm_inproj",
    )(xc, mod, norm_w.reshape(1, d), w_in_bf, b_gate.reshape(1, ng), qw, kw, cos_t, sin_t, bdq, bdk, dup, dupv)


def _rope_tables(lc, n):
    tpos = jnp.arange(n, dtype=I32)
    rows = (tpos // GRID_W).astype(F32)
    cols = (tpos % GRID_W).astype(F32)
    d_axis = HEAD_DIM // 2
    inv = ROPE_THETA ** (-jnp.arange(0, d_axis, 2, dtype=F32) / d_axis)
    ang = jnp.concatenate([rows[:, None] * inv, cols[:, None] * inv], axis=-1)
    cos, sin = jnp.cos(ang), jnp.sin(ang)
    cos_h = jnp.concatenate([cos, cos], axis=-1)
    sin_h = jnp.concatenate([-sin, sin], axis=-1)
    cos_h = jnp.concatenate([jnp.ones((lc, HEAD_DIM), F32), cos_h], axis=0)
    sin_h = jnp.concatenate([jnp.zeros((lc, HEAD_DIM), F32), sin_h], axis=0)
    return jnp.concatenate([cos_h, cos_h], axis=-1), jnp.concatenate([sin_h, sin_h], axis=-1)


NA_ROWS_PER_STEP = 4


def _na_bias_table(rpb):
    qc = np.arange(GRID_W)[:, None]
    kc = np.arange(GRID_W)[None, :]
    win_c0 = np.clip(qc - NA_WIN_C // 2, 0, GRID_W - NA_WIN_C)
    col_ok = (kc >= win_c0) & (kc < win_c0 + NA_WIN_C)
    d_col = np.clip(kc - qc + NA_WIN_C - 1, 0, 2 * NA_WIN_C - 2)
    r = rpb.astype(F32)
    dc = jnp.asarray(d_col)[None, :, None, :]
    cols = jnp.zeros((NA_HEADS, GRID_W, 2 * NA_WIN_R - 1, GRID_W), F32)
    for c in range(2 * NA_WIN_C - 1):
        cols = jnp.where(dc == c, r[:, None, :, c, None], cols)
    cols = jnp.where(jnp.asarray(col_ok)[None, :, None, :], cols, -1e30) * LOG2E
    tab = jnp.stack([cols[:, :, v:v + NA_WIN_R, :] for v in range(NA_WIN_R)], axis=0)
    return tab.reshape(NA_WIN_R, NA_HEADS * GRID_W, NA_WIN_R * GRID_W)


def _na_body(q_ref, k_ref, v_ref, bias_ref, o_ref, *, lc, n_rows):
    i = pl.program_id(1)
    nq = NA_HEADS * GRID_W
    lane_q = lax.broadcasted_iota(I32, (GRID_W, NA_HEADS * HEAD_DIM), 1) // HEAD_DIM
    lane_o = lane_q
    kctx = k_ref[0, 0:lc, :]
    vctx = v_ref[0, 0:lc, :]
    for j in range(NA_ROWS_PER_STEP):
        r = i * NA_ROWS_PER_STEP + j
        kr0 = jnp.clip(r - NA_WIN_R // 2, 0, n_rows - NA_WIN_R)
        variant = kr0 - r + NA_WIN_R - 1
        kstart = pl.multiple_of(lc + kr0 * GRID_W, GRID_W)
        kwin = k_ref[0, pl.ds(kstart, NA_WIN_R * GRID_W), :]
        vwin = v_ref[0, pl.ds(kstart, NA_WIN_R * GRID_W), :]
        q = q_ref[0, j * GRID_W:(j + 1) * GRID_W, :]
        qm = jnp.concatenate([jnp.where(lane_q == h, q, jnp.zeros_like(q)) for h in range(NA_HEADS)], axis=0)
        s_loc = _dot_nt(qm, kwin) + bias_ref[variant]
        s_ctx = _dot_nt(qm, kctx)
        m = jnp.maximum(jnp.max(s_loc, axis=-1, keepdims=True), jnp.max(s_ctx, axis=-1, keepdims=True))
        p_loc = jnp.exp2(s_loc - m)
        p_ctx = jnp.exp2(s_ctx - m)
        l = jnp.sum(p_loc, axis=-1, keepdims=True) + jnp.sum(p_ctx, axis=-1, keepdims=True)
        o = (_dot(p_loc.astype(BF16), vwin) + _dot(p_ctx.astype(BF16), vctx)) / l
        out = jnp.zeros((GRID_W, NA_HEADS * HEAD_DIM), F32)
        for h in range(NA_HEADS):
            out = jnp.where(lane_o == h, o[h * GRID_W:(h + 1) * GRID_W], out)
        o_ref[0, j * GRID_W:(j + 1) * GRID_W, :] = out.astype(o_ref.dtype)
    del nq


def _na_attention(aq, ak, av, bias_tab, lc):
    b, t, w = aq.shape
    n = t - lc
    n_rows = n // GRID_W
    assert n % GRID_W == 0 and n_rows >= NA_WIN_R and n_rows % NA_ROWS_PER_STEP == 0
    tq = NA_ROWS_PER_STEP * GRID_W
    assert lc % tq == 0
    off = lc // tq
    return pl.pallas_call(
        functools.partial(_na_body, lc=lc, n_rows=n_rows),
        grid=(b, n_rows // NA_ROWS_PER_STEP),
        in_specs=[pl.BlockSpec((1, tq, w), lambda bi, i: (bi, i + off, 0)),
                  pl.BlockSpec((1, t, w), lambda bi, i: (bi, 0, 0)),
                  pl.BlockSpec((1, t, w), lambda bi, i: (bi, 0, 0)),
                  pl.BlockSpec(bias_tab.shape, lambda bi, i: (0, 0, 0))],
        out_specs=pl.BlockSpec((1, tq, w), lambda bi, i: (bi, i, 0)),
        out_shape=jax.ShapeDtypeStruct((b, n, w), BF16),
        compiler_params=_cp(("arbitrary", "arbitrary")),
        name="na_attention",
    )(aq, ak, av, bias_tab)


PAIR_KEY_CHUNK = 256
PAIR_Q_TILE = 512
PAIR_STREAMS = 2


def _pair_attn_body(q_ref, k_ref, v_ref, o_ref, *, kc, shared_kv):
    tq = q_ref.shape[1]
    tk = k_ref.shape[1]
    hs = tq // PAIR_STREAMS
    lane = lax.broadcasted_iota(I32, (hs, LANE), 1)
    lo = lane < HEAD_DIM
    qqs = []
    for si in range(PAIR_STREAMS):
        q = q_ref[0, si * hs:(si + 1) * hs, :]
        zero = jnp.zeros_like(q)
        qqs.append(jnp.concatenate([jnp.where(lo, q, zero), jnp.where(lo, zero, q)], axis=0))
    n_chunks = tk // kc

    def step(c, carry):
        start = 0 if n_chunks == 1 else pl.multiple_of(c * kc, kc)
        kk = k_ref[0, pl.ds(start, kc), :]
        vv = v_ref[0, pl.ds(start, kc), :]
        out = []
        for qq, (m, l, acc) in zip(qqs, carry):
            s = _dot_nt(qq, kk)
            m_new = jnp.maximum(m, jnp.max(s, axis=-1, keepdims=True))
            alpha = jnp.exp2(m - m_new)
            p = jnp.exp2(s - m_new)
            if not shared_kv:
                l = alpha * l + jnp.sum(p, axis=-1, keepdims=True)
            acc = alpha * acc + _dot(p.astype(BF16), vv)
            out.append((m_new, l, acc))
        return tuple(out)

    one = (jnp.full((2 * hs, 1), -jnp.inf, F32), jnp.zeros((2 * hs, 1), F32), jnp.zeros((2 * hs, LANE), F32))
    init = (one,) * PAIR_STREAMS
    res = step(0, init) if n_chunks == 1 else lax.fori_loop(0, n_chunks, step, init, unroll=True)
    for si, (m, l, acc) in enumerate(res):
        a, bb = acc[:hs], acc[hs:]
        if shared_kv:
            oa = a / a[:, HEAD_DIM:HEAD_DIM + 1]
            ob = pltpu.roll(bb / bb[:, HEAD_DIM:HEAD_DIM + 1], HEAD_DIM, 1)
        else:
            oa = a / l[:hs]
            ob = bb / l[hs:]
        o_ref[0, si * hs:(si + 1) * hs, :] = jnp.where(lo, oa, ob).astype(o_ref.dtype)


def _pair_attention(q, k, v, *, q_row0, n_q, kv_of_pair, n_pairs, shared_kv):
    b = q.shape[0]
    tk = k.shape[1]
    tq = PAIR_Q_TILE if (q_row0 % PAIR_Q_TILE == 0 and n_q % PAIR_Q_TILE == 0) else 256
    assert q_row0 % tq == 0 and n_q % tq == 0
    kc = PAIR_KEY_CHUNK if tk % PAIR_KEY_CHUNK == 0 else tk
    off = q_row0 // tq
    return pl.pallas_call(
        functools.partial(_pair_attn_body, kc=kc, shared_kv=shared_kv),
        grid=(b, n_pairs, n_q // tq),
        in_specs=[pl.BlockSpec((1, tq, LANE), lambda bi, p, i: (bi, i + off, p)),
                  pl.BlockSpec((1, tk, LANE), lambda bi, p, i: (bi, 0, kv_of_pair(p))),
                  pl.BlockSpec((1, tk, LANE), lambda bi, p, i: (bi, 0, kv_of_pair(p)))],
        out_specs=pl.BlockSpec((1, tq, LANE), lambda bi, p, i: (bi, i, p)),
        out_shape=jax.ShapeDtypeStruct((b, n_q, n_pairs * LANE), BF16),
        compiler_params=_cp(("arbitrary", "arbitrary", "arbitrary")),
        name="pair_attention",
    )(q, k, v)


def _filter_body(w1t_ref, w1c_ref, w1s_ref, b1_ref, w2_ref, b2_ref, w3_ref, fr_ref, o_ref, *, length, rows, transposed):
    g = pl.program_id(0)
    n_total = 2 * length
    bands = (HY_EMB_DIM - 1) // 2
    f_col = 1e-4 + lax.broadcasted_iota(I32, (bands, 1), 0).astype(F32) * ((bands - 1 - 1e-4) / (bands - 1))
    nch = HY_ORDER * HY_WIDTH
    d_lo = math.log(HY_DECAY_TARGET) / HY_SLOW_DECAY_PCT
    d_hi = math.log(HY_DECAY_TARGET) / HY_FAST_DECAY_PCT
    deltas = d_lo + lax.broadcasted_iota(I32, (1, nch), 1).astype(F32) * ((d_hi - d_lo) / (nch - 1))
    freq = fr_ref[...]
    position = lambda tt: jnp.where(tt < length, tt, n_total - tt).astype(F32)
    pos_row = position(g * rows + lax.broadcasted_iota(I32, (1, rows), 1))
    arg = f_col * (pos_row * (2.0 * math.pi / length))
    pre = (w1t_ref[...] * (pos_row * (1.0 / (length - 1))) + _dot(w1c_ref[...], jnp.cos(arg), HI)
           - _dot(w1s_ref[...], jnp.sin(arg), HI))
    h1 = jnp.sin(freq * (pre + b1_ref[...]))
    h2 = jnp.sin(freq * (_dot(w2_ref[...], h1, HI) + b2_ref[...])).T
    sub = DFT_N2 if transposed else rows
    for j in range(rows // sub):
        tt = g * rows + j * sub + lax.broadcasted_iota(I32, (sub, 1), 0)
        tn = position(tt) * (1.0 / (length - 1))
        filt = _dot(h2[j * sub:(j + 1) * sub], w3_ref[0], HI) * jnp.exp(-tn * jnp.abs(deltas))
        filt = jnp.where(tt == length, 0.0, filt)
        if transposed:
            o_ref[:, j, :] = filt
        else:
            o_ref[...] = filt


def _hyena_filter(length, w1, b1, w2, b2, w3, freq, *, transposed):
    hid = w1.shape[1]
    nch = HY_ORDER * HY_WIDTH
    n_total = 2 * length
    bands = (HY_EMB_DIM - 1) // 2
    rows = SUBLANE * DFT_N2 if transposed else length
    n_steps = n_total // rows
    assert n_total % rows == 0 and n_steps % 2 == 0
    w3d = w3.reshape(hid, 2, nch).transpose(1, 0, 2)
    full = lambda shape: pl.BlockSpec(shape, lambda g: (0,) * len(shape))
    if transposed:
        out_spec = pl.BlockSpec((DFT_N2, SUBLANE, nch), lambda g: (0, g, 0))
        out_shape = jax.ShapeDtypeStruct((DFT_N2, n_total // DFT_N2, nch), F32)
    else:
        out_spec = pl.BlockSpec((rows, nch), lambda g: (g, 0))
        out_shape = jax.ShapeDtypeStruct((n_total, nch), F32)
    return pl.pallas_call(
        functools.partial(_filter_body, length=length, rows=rows, transposed=transposed),
        grid=(n_steps,),
        in_specs=[full((hid, 1)), full((hid, bands)), full((hid, bands)), full((hid, 1)), full((hid, hid)), full((hid, 1)),
                  pl.BlockSpec((1, hid, nch), lambda g: (g // (n_steps // 2), 0, 0)), full((hid, 1))],
        out_specs=out_spec,
        out_shape=out_shape,
        compiler_params=_cp(("arbitrary",)),
        name="hyena_filter",
    )(w1[0:1].T, w1[1:1 + bands].T, w1[1 + bands:].T, b1.reshape(hid, 1), w2.T, b2.reshape(hid, 1), w3d, freq.reshape(hid, 1))


def _dft_tables(n1, n1_used):
    n = n1 * DFT_N2
    n2 = np.arange(DFT_N2)[:, None, None]
    k1 = np.arange(n1)[None, :, None]
    nn1 = np.arange(n1_used)[None, None, :]
    ang = 2.0 * np.pi * ((k1 * (DFT_N2 * nn1 + n2)) % n) / n
    g_fwd = np.concatenate([np.cos(ang), -np.sin(ang)], axis=1)
    g_inv = np.transpose(g_fwd, (0, 2, 1)) / n
    kk = np.arange(DFT_N2)
    a2 = 2.0 * np.pi * ((kk[:, None] * kk[None, :]) % DFT_N2) / DFT_N2
    fr, fi = np.cos(a2), -np.sin(a2)
    f2 = np.block([[fr, -fi], [fi, fr]])
    f2c = np.block([[fr, fi], [-fi, fr]])
    cast = lambda a: jnp.asarray(a.astype(np.float32)).astype(BF16)
    return cast(g_fwd), cast(g_inv), cast(f2), cast(f2c)


def _stage1_body(x_ref, g_ref, o_ref, *, n1):
    for j in range(SUBLANE):
        res = _dot(g_ref[j], x_ref[0, j].astype(BF16))
        o_ref[0, 0, :, j, :] = res[:n1]
        o_ref[0, 1, :, j, :] = res[n1:]


def _dft_stage1(x_t, g_fwd):
    bx, _, n1u, c = x_t.shape
    n1 = g_fwd.shape[1] // 2
    return pl.pallas_call(
        functools.partial(_stage1_body, n1=n1),
        grid=(bx, DFT_N2 // SUBLANE),
        in_specs=[pl.BlockSpec((1, SUBLANE, n1u, c), lambda bi, g: (bi, g, 0, 0)),
                  pl.BlockSpec((SUBLANE, 2 * n1, n1u), lambda bi, g: (g, 0, 0))],
        out_specs=pl.BlockSpec((1, 2, n1, SUBLANE, c), lambda bi, g: (bi, 0, 0, g, 0)),
        out_shape=jax.ShapeDtypeStruct((bx, 2, n1, DFT_N2, c), F32),
        compiler_params=_cp(("arbitrary", "arbitrary")),
        name="dft_stage1",
    )(x_t, g_fwd)


def _stage2_spec_body(a_ref, f2_ref, o_ref):
    for j in range(SUBLANE):
        slab = jnp.concatenate([a_ref[0, 0, j], a_ref[0, 1, j]], axis=0).astype(BF16)
        o_ref[j] = _dot(f2_ref[...], slab).astype(o_ref.dtype)


def _dft_stage2_spectrum(a, f2):
    _, _, n1, _, c = a.shape
    return pl.pallas_call(
        _stage2_spec_body,
        grid=(n1 // SUBLANE,),
        in_specs=[pl.BlockSpec((1, 2, SUBLANE, DFT_N2, c), lambda g: (0, 0, g, 0, 0)),
                  pl.BlockSpec(f2.shape, lambda g: (0, 0))],
        out_specs=pl.BlockSpec((SUBLANE, 2 * DFT_N2, c), lambda g: (g, 0, 0)),
        out_shape=jax.ShapeDtypeStruct((n1, 2 * DFT_N2, c), BF16),
        compiler_params=_cp(("arbitrary",)),
        name="dft_stage2_spectrum",
    )(a, f2)


def _stage2_conv_body(a_ref, k_ref, f2_ref, f2c_ref, o_ref):
    h = DFT_N2
    for j in range(SUBLANE):
        slab = jnp.concatenate([a_ref[0, 0, j], a_ref[0, 1, j]], axis=0).astype(BF16)
        x = _dot(f2_ref[...], slab)
        kf = k_ref[j].astype(F32)
        xr, xi, kr, ki = x[:h], x[h:], kf[:h], kf[h:]
        y = jnp.concatenate([xr * kr - xi * ki, xr * ki + xi * kr], axis=0).astype(BF16)
        bv = _dot(f2c_ref[...], y)
        o_ref[0, 0, :, j, :] = bv[:h]
        o_ref[0, 1, :, j, :] = bv[h:]


def _dft_stage2_conv(a, kspec, order, f2, f2c):
    bx, _, n1, _, c = a.shape
    return pl.pallas_call(
        _stage2_conv_body,
        grid=(bx, n1 // SUBLANE),
        in_specs=[pl.BlockSpec((1, 2, SUBLANE, DFT_N2, c), lambda bi, g: (bi, 0, g, 0, 0)),
                  pl.BlockSpec((SUBLANE, 2 * DFT_N2, c), lambda bi, g: (g, 0, order)),
                  pl.BlockSpec(f2.shape, lambda bi, g: (0, 0)),
                  pl.BlockSpec(f2c.shape, lambda bi, g: (0, 0))],
        out_specs=pl.BlockSpec((1, 2, DFT_N2, SUBLANE, c), lambda bi, g: (bi, 0, 0, g, 0)),
        out_shape=jax.ShapeDtypeStruct((bx, 2, DFT_N2, n1, c), F32),
        compiler_params=_cp(("arbitrary", "arbitrary")),
        name="dft_stage2_conv",
    )(a, kspec, f2, f2c)


def _inverse_body(b_ref, g_ref, u_ref, gate_ref, bias_ref, *rest, last, n1):
    if last:
        (o_ref,) = rest
    else:
        gf_ref, o_ref, a_ref = rest
    for j in range(SUBLANE):
        slab = jnp.concatenate([b_ref[0, 0, j], b_ref[0, 1, j]], axis=0).astype(BF16)
        y = _dot(g_ref[j], slab)
        out = gate_ref[0, j] * (y + u_ref[0, j] * bias_ref[...])
        if last:
            o_ref[0, :, j, :] = out
        else:
            o_ref[0, j] = out
            res = _dot(gf_ref[j], out.astype(BF16))
            a_ref[0, 0, :, j, :] = res[:n1]
            a_ref[0, 1, :, j, :] = res[n1:]


def _dft_inverse(bv, g_inv, u_t, gate_t, bias, g_fwd, *, last):
    bx, _, _, n1, c = bv.shape
    n1u = g_inv.shape[1]
    t_spec = pl.BlockSpec((1, SUBLANE, n1u, c), lambda bi, g: (bi, g, 0, 0))
    in_specs = [pl.BlockSpec((1, 2, SUBLANE, n1, c), lambda bi, g: (bi, 0, g, 0, 0)),
                pl.BlockSpec((SUBLANE, n1u, 2 * n1), lambda bi, g: (g, 0, 0)),
                t_spec, t_spec,
                pl.BlockSpec((1, c), lambda bi, g: (0, 0))]
    args = [bv, g_inv, u_t, gate_t, bias.reshape(1, c)]
    if last:
        out_specs = pl.BlockSpec((1, n1u, SUBLANE, c), lambda bi, g: (bi, 0, g, 0))
        out_shape = jax.ShapeDtypeStruct((bx, n1u, DFT_N2, c), F32)
    else:
        in_specs.append(pl.BlockSpec((SUBLANE, 2 * n1, n1u), lambda bi, g: (g, 0, 0)))
        args.append(g_fwd)
        out_specs = [t_spec, pl.BlockSpec((1, 2, n1, SUBLANE, c), lambda bi, g: (bi, 0, 0, g, 0))]
        out_shape = [jax.ShapeDtypeStruct((bx, DFT_N2, n1u, c), F32), jax.ShapeDtypeStruct((bx, 2, n1, DFT_N2, c), F32)]
    return pl.pallas_call(
        functools.partial(_inverse_body, last=last, n1=n1),
        grid=(bx, DFT_N2 // SUBLANE),
        in_specs=in_specs,
        out_specs=out_specs,
        out_shape=out_shape,
        compiler_params=_cp(("arbitrary", "arbitrary")),
        name="dft_inverse",
    )(*args)


def _shortconv_body(prev_ref, cur_ref, next_ref, w_ref, b_ref, v_ref, x1_ref, x2_ref, *, n_steps):
    i = pl.program_id(1)
    u = cur_ref[0]
    rows = u.shape[0]
    prev_row = jnp.where(i > 0, prev_ref[0, SUBLANE - 1:SUBLANE, :], 0.0)
    next_row = jnp.where(i < n_steps - 1, next_ref[0, 0:1, :], 0.0)
    ridx = lax.broadcasted_iota(I32, u.shape, 0)
    up = jnp.where(ridx == 0, prev_row, pltpu.roll(u, 1, 0))
    dn = jnp.where(ridx == rows - 1, next_row, pltpu.roll(u, rows - 1, 0))
    z = up * w_ref[0:1] + u * w_ref[1:2] + dn * w_ref[2:3] + b_ref[...]
    cw = HY_WIDTH
    for j in range(rows // DFT_N2):
        zj = z[j * DFT_N2:(j + 1) * DFT_N2]
        v_ref[0, :, j, :] = zj[:, :cw]
        x1_ref[0, :, j, :] = zj[:, cw:2 * cw]
        x2_ref[0, :, j, :] = zj[:, 2 * cw:]


def _short_conv_t(cu, conv_w, conv_b):
    b, length, c3 = cu.shape
    rows = SUBLANE * DFT_N2
    assert length % rows == 0
    n_steps = length // rows
    rb = rows // SUBLANE
    n1u = length // DFT_N2
    out_spec = pl.BlockSpec((1, DFT_N2, SUBLANE, HY_WIDTH), lambda bi, i: (bi, 0, i, 0))
    out_shape = jax.ShapeDtypeStruct((b, DFT_N2, n1u, HY_WIDTH), F32)
    return pl.pallas_call(
        functools.partial(_shortconv_body, n_steps=n_steps),
        grid=(b, n_steps),
        in_specs=[pl.BlockSpec((1, SUBLANE, c3), lambda bi, i: (bi, jnp.maximum(i * rb - 1, 0), 0)),
                  pl.BlockSpec((1, rows, c3), lambda bi, i: (bi, i, 0)),
                  pl.BlockSpec((1, SUBLANE, c3), lambda bi, i: (bi, jnp.minimum((i + 1) * rb, n_steps * rb - 1), 0)),
                  pl.BlockSpec((3, c3), lambda bi, i: (0, 0)),
                  pl.BlockSpec((1, c3), lambda bi, i: (0, 0))],
        out_specs=[out_spec] * 3,
        out_shape=[out_shape] * 3,
        compiler_params=_cp(("arbitrary", "arbitrary")),
        name="hyena_short_conv",
    )(cu, cu, cu, conv_w, conv_b.reshape(1, c3))


def _hyena_long(cu_lat, conv_w, conv_b, filt_params, bias_d):
    b, length, _ = cu_lat.shape
    n1u = length // DFT_N2
    n1 = 2 * n1u
    g_sig, g_inv, f2, f2c = _dft_tables(n1, n1u)
    g_full = _dft_tables(n1, n1)[0]
    filt_t = _hyena_filter(length, *filt_params, transposed=True)
    kspec = _dft_stage2_spectrum(_dft_stage1(filt_t[None], g_full), f2)
    v_t, x1_t, x2_t = _short_conv_t(cu_lat, conv_w, conv_b)
    y = v_t
    a = _dft_stage1(y, g_sig)
    for o, gate in enumerate((x1_t, x2_t)):
        bv = _dft_stage2_conv(a, kspec, o, f2, f2c)
        if o == HY_ORDER - 1:
            y = _dft_inverse(bv, g_inv, y, gate, bias_d[o], g_sig, last=True)
        else:
            y, a = _dft_inverse(bv, g_inv, y, gate, bias_d[o], g_sig, last=False)
    return y.reshape(b, length, HY_WIDTH)


def _hyena_small_body(cu_ref, w_ref, b_ref, ks_ref, ff_ref, fi_ref, bias_ref, o_ref):
    u = cu_ref[0]
    rows = u.shape[0]
    ridx = lax.broadcasted_iota(I32, u.shape, 0)
    up = jnp.where(ridx == 0, 0.0, pltpu.roll(u, 1, 0))
    dn = jnp.where(ridx == rows - 1, 0.0, pltpu.roll(u, rows - 1, 0))
    z = up * w_ref[0:1] + u * w_ref[1:2] + dn * w_ref[2:3] + b_ref[...]
    cw = HY_WIDTH
    nf = ff_ref.shape[0] // 2
    y = z[:, :cw]
    for o in range(HY_ORDER):
        gate = z[:, (o + 1) * cw:(o + 2) * cw]
        x = _dot(ff_ref[...], y.astype(BF16))
        kf = ks_ref[:, o * cw:(o + 1) * cw]
        xr, xi, kr, ki = x[:nf], x[nf:], kf[:nf], kf[nf:]
        prod = jnp.concatenate([xr * kr - xi * ki, xr * ki + xi * kr], axis=0).astype(BF16)
        conv = _dot(fi_ref[...], prod)
        y = gate * (conv + y * bias_ref[o:o + 1])
    o_ref[0] = y


def _spectrum_small_body(ff_ref, f_ref, o_ref):
    o_ref[...] = _dot(ff_ref[...], f_ref[...].astype(BF16))


def _hyena_small(cu_ctx, conv_w, conv_b, filt_params, bias_d):
    b, length, c3 = cu_ctx.shape
    n = 2 * length
    kt = np.arange(n)[:, None] * np.arange(n)[None, :]
    ang = 2.0 * np.pi * (kt % n) / n
    fwd = np.concatenate([np.cos(ang), -np.sin(ang)], axis=0)
    cast = lambda a: jnp.asarray(a.astype(np.float32)).astype(BF16)
    f_full = cast(fwd)
    f_sig = cast(fwd[:, :length])
    f_inv = cast(np.transpose(fwd[:, :length]) / n)
    filt = _hyena_filter(length, *filt_params, transposed=False)
    nch = filt.shape[1]
    kspec = pl.pallas_call(
        _spectrum_small_body,
        out_shape=jax.ShapeDtypeStruct((2 * n, nch), F32),
        compiler_params=_cp(None),
        name="hyena_small_spectrum",
    )(f_full, filt)
    full = lambda shape: pl.BlockSpec(shape, lambda bi: (0,) * len(shape))
    return pl.pallas_call(
        _hyena_small_body,
        grid=(b,),
        in_specs=[pl.BlockSpec((1, length, c3), lambda bi: (bi, 0, 0)), full((3, c3)), full((1, c3)), full(kspec.shape),
                  full(f_sig.shape), full(f_inv.shape), full(bias_d.shape)],
        out_specs=pl.BlockSpec((1, length, HY_WIDTH), lambda bi: (bi, 0, 0)),
        out_shape=jax.ShapeDtypeStruct((b, length, HY_WIDTH), F32),
        compiler_params=_cp(("arbitrary",)),
        name="hyena_small",
    )(cu_ctx, conv_w, conv_b.reshape(1, c3), kspec, f_sig, f_inv, bias_d)


def _merge_body(x_ref, mod_ref, yac_ref, yal_ref, ybc_ref, ybl_ref, ycc_ref, ycl_ref, g_ref, wa_ref, wb_ref, wc_ref, wo_ref,
                nw_ref, rwh_ref, rwl_ref, xo_ref, h_ref, lg_ref, *, ctx_tiles):
    d = x_ref.shape[2]
    is_ctx = pl.program_id(1) < ctx_tiles
    ya = jnp.where(is_ctx, yac_ref[0], yal_ref[0])
    yb = jnp.where(is_ctx, ybc_ref[0], ybl_ref[0])
    yc = jnp.where(is_ctx, ycc_ref[0], ycl_ref[0])
    g = g_ref[0].astype(F32)
    m = (g[:, :d] * _dot(ya, wa_ref[...]) + g[:, d:2 * d] * _dot(yb, wb_ref[...])
         + g[:, 2 * d:] * _dot(yc.astype(BF16), wc_ref[...]))
    y = _dot(m.astype(BF16), wo_ref[...])
    mod = mod_ref[0]
    x = x_ref[0] + mod[2:3] * y
    xo_ref[0] = x
    hn = x * lax.rsqrt(jnp.mean(x * x, axis=-1, keepdims=True) + RMS_EPS) * nw_ref[...]
    h = hn * (1.0 + mod[4:5]) + mod[3:4]
    h_ref[...] = h
    h_hi = h.astype(BF16)
    h_lo = (h - h_hi.astype(F32)).astype(BF16)
    lg_ref[...] = _dot(h_hi, rwh_ref[...]) + _dot(h_hi, rwl_ref[...]) + _dot(h_lo, rwh_ref[...])


def _merge(xc, mod, ya, yb, yc, g, w_a, w_b, w_c, w_o, norm2_w, router_w, lc):
    b, t, d = xc.shape
    tm = 256
    nb = b
    ctx_tiles = lc // tm
    nt = t // tm
    ne = router_w.shape[1]
    rw = jnp.pad(router_w.astype(F32), ((0, 0), (0, LANE - ne)))
    rw_hi = rw.astype(BF16)
    rw_lo = (rw - rw_hi.astype(F32)).astype(BF16)
    full = lambda shape: pl.BlockSpec(shape, lambda bi, ti: (0,) * len(shape))
    tok = lambda width: pl.BlockSpec((1, tm, width), lambda bi, ti: (bi, ti, 0))
    ctx_tok = lambda width: pl.BlockSpec((1, tm, width), lambda bi, ti: (bi, jnp.minimum(ti, ctx_tiles - 1), 0))
    lat_tok = lambda width: pl.BlockSpec((1, tm, width), lambda bi, ti: (bi, jnp.maximum(ti - ctx_tiles, 0), 0))
    branch_specs = []
    for pair in (ya, yb, yc):
        branch_specs += [ctx_tok(pair[0].shape[2]), lat_tok(pair[1].shape[2])]
    return pl.pallas_call(
        functools.partial(_merge_body, ctx_tiles=ctx_tiles),
        grid=(b, nt),
        in_specs=[tok(d),
                  pl.BlockSpec((1, N_MOD, d), lambda bi, ti: (jnp.where(ti < ctx_tiles, nb, bi), 0, 0))]
                 + branch_specs
                 + [tok(g.shape[2]), full(w_a.shape), full(w_b.shape), full(w_c.shape), full(w_o.shape), full((1, d)),
                    full(rw_hi.shape), full(rw_lo.shape)],
        out_specs=[tok(d),
                   pl.BlockSpec((tm, d), lambda bi, ti: (bi * nt + ti, 0)),
                   pl.BlockSpec((tm, LANE), lambda bi, ti: (bi * nt + ti, 0))],
        out_shape=[jax.ShapeDtypeStruct((b, t, d), F32),
                   jax.ShapeDtypeStruct((b * t, d), F32),
                   jax.ShapeDtypeStruct((b * t, LANE), F32)],
        compiler_params=_cp(("arbitrary", "arbitrary")),
        name="merge_outproj",
    )(xc, mod, ya[0], ya[1], yb[0], yb[1], yc[0], yc[1], g, w_a, w_b, w_c, w_o, norm2_w.reshape(1, d), rw_hi, rw_lo)


MOE_ROW_TILE = 512
MOE_TOKEN_TILE = 256
MOE_RUN_ALIGN = SUBLANE
MOE_LOCAL_ROWS = -(-(2 * MOE_TOKEN_TILE + N_EXPERTS * (MOE_RUN_ALIGN - 1)) // LANE) * LANE


def _top2_of4(a):
    m1 = jnp.maximum(jnp.maximum(a[0], a[1]), jnp.maximum(a[2], a[3]))
    i1 = jnp.where(a[0] == m1, 0, jnp.where(a[1] == m1, 1, jnp.where(a[2] == m1, 2, 3)))
    neg = jnp.full_like(m1, -jnp.inf)
    r = [jnp.where(i1 == j, neg, a[j]) for j in range(4)]
    m2 = jnp.maximum(jnp.maximum(r[0], r[1]), jnp.maximum(r[2], r[3]))
    i2 = jnp.where((r[0] == m2) & (i1 != 0), 0,
                   jnp.where((r[1] == m2) & (i1 != 1), 1, jnp.where((r[2] == m2) & (i1 != 2), 2, 3)))
    return m1, m2, i1, i2


def _route_body(lg_ref, rb_ref, tri_ref, e_ref, w_ref, rank_ref, cnt_ref):
    scores = jax.nn.sigmoid(lg_ref[...])
    sel = scores + rb_ref[...]
    rows = [sel[e:e + 1] for e in range(N_EXPERTS)]
    tops = [_top2_of4(rows[4 * g:4 * g + 4]) for g in range(N_GROUPS)]
    gs = [t[0] + t[1] for t in tops]
    gmax = jnp.maximum(jnp.maximum(gs[0], gs[1]), jnp.maximum(gs[2], gs[3]))
    gi = jnp.where(gs[0] == gmax, 0, jnp.where(gs[1] == gmax, 1, jnp.where(gs[2] == gmax, 2, 3)))
    pick = lambda k: jnp.where(gi == 0, tops[0][k], jnp.where(gi == 1, tops[1][k], jnp.where(gi == 2, tops[2][k], tops[3][k])))
    e0 = gi * EXPERTS_PER_GROUP + pick(2)
    e1 = gi * EXPERTS_PER_GROUP + pick(3)
    eid = lax.broadcasted_iota(I32, scores.shape, 0)
    oh0 = eid == e0
    oh1 = eid == e1
    s0 = jnp.sum(jnp.where(oh0, scores, 0.0), axis=0, keepdims=True)
    s1 = jnp.sum(jnp.where(oh1, scores, 0.0), axis=0, keepdims=True)
    tot = s0 + s1
    e_ref[...] = jnp.concatenate([e0, e1], axis=0)
    w_ref[...] = jnp.concatenate([s0 / tot, s1 / tot], axis=0)
    oh = (oh0 | oh1).astype(BF16)
    incl = _dot(oh, tri_ref[...])
    excl = incl - 1.0
    r0 = jnp.sum(jnp.where(oh0, excl, 0.0), axis=0, keepdims=True)
    r1 = jnp.sum(jnp.where(oh1, excl, 0.0), axis=0, keepdims=True)
    rank_ref[...] = jnp.concatenate([r0, r1], axis=0).astype(I32)
    tl = incl.shape[1]
    cnt_ref[0] = jnp.broadcast_to(incl[:, tl - 1:tl], cnt_ref.shape[1:]).astype(I32)


def _route(logits_t, router_b):
    ne, t = logits_t.shape
    tl = MOE_TOKEN_TILE
    assert t % tl == 0
    tri = jnp.asarray(np.triu(np.ones((tl, tl), np.float32)), BF16)
    two = lambda dt: jax.ShapeDtypeStruct((2, t), dt)
    return pl.pallas_call(
        _route_body,
        grid=(t // tl,),
        in_specs=[pl.BlockSpec((ne, tl), lambda i: (0, i)), pl.BlockSpec((ne, 1), lambda i: (0, 0)),
                  pl.BlockSpec((tl, tl), lambda i: (0, 0))],
        out_specs=[pl.BlockSpec((2, tl), lambda i: (0, i))] * 3 + [pl.BlockSpec((1, ne, LANE), lambda i: (i, 0, 0))],
        out_shape=[two(I32), two(F32), two(I32), jax.ShapeDtypeStruct((t // tl, ne, LANE), I32)],
        compiler_params=_cp(("arbitrary",)),
        name="moe_route",
    )(logits_t, router_b.reshape(ne, 1).astype(F32), tri)


RUN_PIECES = [1 << b for b in reversed(range(MOE_RUN_ALIGN.bit_length() - 1, MOE_ROW_TILE.bit_length() - 1))]


def _when_go(cp, cond, wait):
    @pl.when(cond)
    def _():
        cp.wait() if wait else cp.start()


def _run_copies(run_ref, local_ref, slots_ref, sem, *, to_slots, wait):
    for e in range(N_EXPERTS):
        srow = run_ref[0, 0, e]
        n = run_ref[0, 0, N_EXPERTS + e]
        lrow = run_ref[0, 0, 2 * N_EXPERTS + e]
        for s in RUN_PIECES:
            s_sl = slots_ref.at[pl.ds(pl.multiple_of(srow, MOE_RUN_ALIGN), s), :]
            l_sl = local_ref.at[pl.ds(pl.multiple_of(lrow, MOE_RUN_ALIGN), s), :]
            cp = pltpu.make_async_copy(l_sl, s_sl, sem) if to_slots else pltpu.make_async_copy(s_sl, l_sl, sem)
            _when_go(cp, (n & s) != 0, wait)
            srow = srow + (n & s)
            lrow = lrow + (n & s)


def _dispatch_body(run_ref, pad_ref, loc_ref, h_ref, xs_ref, sem, pbuf, zbuf, zsem, *, n_tail):
    tm = h_ref.shape[0]

    def zero_fill(wait):
        for e in range(N_EXPERTS):
            cur = pad_ref[0, e]
            n_pad = pad_ref[0, N_EXPERTS + e]
            for s in RUN_PIECES:
                dst = xs_ref.at[pl.ds(pl.multiple_of(cur, MOE_RUN_ALIGN), s), :]
                _when_go(pltpu.make_async_copy(zbuf.at[pl.ds(0, s), :], dst, zsem), (n_pad & s) != 0, wait)
                cur = cur + (n_pad & s)
        zrows = zbuf.shape[0]
        tail = pad_ref[0, 2 * N_EXPERTS]
        for k in range(n_tail):
            row = pl.multiple_of(tail + k * zrows, zrows)
            _when_go(pltpu.make_async_copy(zbuf, xs_ref.at[pl.ds(row, zrows), :], zsem), row < xs_ref.shape[0], wait)

    @pl.when(pl.program_id(0) == 0)
    def _():
        zbuf[...] = jnp.zeros_like(zbuf)
        zero_fill(False)
        zero_fill(True)

    loc = loc_ref[...]
    rid = lax.broadcasted_iota(I32, (pbuf.shape[0], tm), 0)
    onehot = ((rid == loc[0:1]) | (rid == loc[1:2])).astype(BF16)
    pbuf[...] = _dot(onehot, h_ref[...].astype(BF16))
    _run_copies(run_ref, pbuf, xs_ref, sem, to_slots=True, wait=False)
    _run_copies(run_ref, pbuf, xs_ref, sem, to_slots=True, wait=True)


def _dispatch(h2, run_info, pad_info, loc, s_pad):
    t, d = h2.shape
    tm = MOE_TOKEN_TILE
    zrows = MOE_ROW_TILE // 2
    return pl.pallas_call(
        functools.partial(_dispatch_body, n_tail=(s_pad - 2 * t) // zrows),
        grid=(t // tm,),
        in_specs=[pl.BlockSpec((1, 1, run_info.shape[2]), lambda i: (i, 0, 0), memory_space=pltpu.SMEM),
                  pl.BlockSpec(pad_info.shape, lambda i: (0, 0), memory_space=pltpu.SMEM),
                  pl.BlockSpec((2, tm), lambda i: (0, i)),
                  pl.BlockSpec((tm, d), lambda i: (i, 0))],
        out_specs=pl.BlockSpec(memory_space=pl.ANY),
        out_shape=jax.ShapeDtypeStruct((s_pad, d), F32),
        scratch_shapes=[pltpu.SemaphoreType.DMA(()), pltpu.VMEM((MOE_LOCAL_ROWS, d), F32), pltpu.VMEM((zrows, d), F32),
                        pltpu.SemaphoreType.DMA(())],
        compiler_params=pltpu.CompilerParams(dimension_semantics=("arbitrary",), has_side_effects=True,
                                             vmem_limit_bytes=VMEM_LIMIT),
        name="moe_dispatch",
    )(run_info, pad_info, loc, h2)


def _ffn_body(te_ref, nu_ref, x_ref, wg_ref, wu_ref, wd_ref, o_ref):
    i = pl.program_id(0)

    @pl.when(i < nu_ref[0])
    def _():
        x = x_ref[...].astype(BF16)
        hmid = (_silu(_dot_nt(x, wg_ref[0, 0].astype(BF16))) * _dot_nt(x, wu_ref[0, 0].astype(BF16))).astype(BF16)
        o_ref[...] = _dot(hmid, wd_ref[0, 0].astype(BF16))

    @pl.when(i >= nu_ref[0])
    def _():
        o_ref[...] = jnp.zeros_like(o_ref)


def _grouped_ffn(xs, tile_expert, n_used, wg, wu, wd, layer):
    s_pad, d = xs.shape
    tr = MOE_ROW_TILE
    n_tiles = s_pad // tr
    ff = wg.shape[2]
    grid_spec = pltpu.PrefetchScalarGridSpec(
        num_scalar_prefetch=2,
        grid=(n_tiles,),
        in_specs=[pl.BlockSpec((tr, d), lambda i, te, nu: (jnp.minimum(i, nu[0] - 1), 0)),
                  pl.BlockSpec((1, 1, ff, d), lambda i, te, nu: (layer, te[i], 0, 0)),
                  pl.BlockSpec((1, 1, ff, d), lambda i, te, nu: (layer, te[i], 0, 0)),
                  pl.BlockSpec((1, 1, ff, d), lambda i, te, nu: (layer, te[i], 0, 0))],
        out_specs=pl.BlockSpec((tr, d), lambda i, te, nu: (i, 0)),
    )
    return pl.pallas_call(
        _ffn_body,
        grid_spec=grid_spec,
        out_shape=jax.ShapeDtypeStruct((s_pad, d), F32),
        compiler_params=_cp(("arbitrary",)),
        name="moe_grouped_ffn",
    )(tile_expert, n_used, xs, wg, wu, wd)


def _combine_body(run_ref, x_ref, mod_ref, w_ref, loc_ref, nw_ref, ys_ref, o_ref, ybuf, sem, *, final):
    tm = x_ref.shape[1]

    @pl.when((pl.program_id(0) == 0) & (pl.program_id(1) == 0))
    def _():
        ybuf[...] = jnp.zeros_like(ybuf)

    _run_copies(run_ref, ybuf, ys_ref, sem, to_slots=False, wait=False)
    _run_copies(run_ref, ybuf, ys_ref, sem, to_slots=False, wait=True)
    y16 = ybuf[...].astype(BF16)
    cid = lax.broadcasted_iota(I32, (tm, ybuf.shape[0]), 1)
    pick0 = (cid == loc_ref[:, 0:1]).astype(BF16)
    pick1 = (cid == loc_ref[:, 1:2]).astype(BF16)
    y = w_ref[:, 0:1] * _dot(pick0, y16) + w_ref[:, 1:2] * _dot(pick1, y16)
    x = x_ref[0] + mod_ref[0][5:6] * y
    if final:
        x = x * lax.rsqrt(jnp.mean(x * x, axis=-1, keepdims=True) + RMS_EPS) * nw_ref[...]
    o_ref[0] = x


def _moe_combine(xc, mod, ys, run_info, w_t, loc_t, final_w, lc, *, final):
    b, t, d = xc.shape
    tm = MOE_TOKEN_TILE
    nb = b
    ctx_tiles = lc // tm
    nt = t // tm
    row0 = ctx_tiles if final else 0
    n_out = t - row0 * tm
    return pl.pallas_call(
        functools.partial(_combine_body, final=final),
        grid=(b, nt - row0),
        in_specs=[pl.BlockSpec((1, 1, run_info.shape[2]), lambda bi, ti: (bi * nt + ti + row0, 0, 0), memory_space=pltpu.SMEM),
                  pl.BlockSpec((1, tm, d), lambda bi, ti: (bi, ti + row0, 0)),
                  pl.BlockSpec((1, N_MOD, d), lambda bi, ti: (jnp.where(ti + row0 < ctx_tiles, nb, bi), 0, 0)),
                  pl.BlockSpec((tm, 2), lambda bi, ti: (bi * nt + ti + row0, 0)),
                  pl.BlockSpec((tm, 2), lambda bi, ti: (bi * nt + ti + row0, 0)),
                  pl.BlockSpec((1, d), lambda bi, ti: (0, 0)),
                  pl.BlockSpec(memory_space=pl.ANY)],
        out_specs=pl.BlockSpec((1, tm, d), lambda bi, ti: (bi, ti, 0)),
        out_shape=jax.ShapeDtypeStruct((b, n_out, d), F32),
        scratch_shapes=[pltpu.VMEM((MOE_LOCAL_ROWS, d), F32), pltpu.SemaphoreType.DMA(())],
        compiler_params=_cp(("arbitrary", "arbitrary")),
        name="moe_combine",
    )(run_info, xc, mod, w_t, loc_t, final_w.reshape(1, d), ys)


def _moe_experts(h2, logits_t, router_b, wg, wu, wd, layer):
    t = h2.shape[0]
    tr = MOE_ROW_TILE
    tm = MOE_TOKEN_TILE
    al = MOE_RUN_ALIGN
    nt = t // tm
    e_idx, w_tok, lrank, cnt = _route(logits_t, router_b)
    cnt = cnt[:, :, 0]
    run_len = (cnt + al - 1) // al * al
    total = jnp.sum(run_len, axis=0)
    padded = (total + tr - 1) // tr * tr
    ends = jnp.cumsum(padded)
    starts = ends - padded
    run_start = starts[None, :] + jnp.cumsum(run_len, axis=0) - run_len
    loc_start = jnp.cumsum(run_len, axis=1) - run_len
    run_info = jnp.concatenate([run_start, run_len, loc_start], axis=1).astype(I32).reshape(nt, 1, 3 * N_EXPERTS)
    e3 = e_idx.reshape(2, nt, tm)
    loc = jnp.zeros_like(e3)
    for e in range(N_EXPERTS):
        loc = jnp.where(e3 == e, loc_start[None, :, e, None], loc)
    loc = (loc + lrank.reshape(2, nt, tm)).reshape(2, t).astype(I32)
    s_pad = (-(-(2 * t + nt * N_EXPERTS * (al - 1)) // tr) + N_EXPERTS) * tr
    n_tiles = s_pad // tr
    tile_start = jnp.arange(n_tiles, dtype=I32) * tr
    n_used = (ends[-1] // tr).astype(I32)
    tile_expert = jnp.sum((ends[None, :] <= tile_start[:, None]).astype(I32), axis=1)
    last_used = jnp.sum((ends <= (n_used - 1) * tr).astype(I32))
    tile_expert = jnp.minimum(jnp.where(jnp.arange(n_tiles) < n_used, tile_expert, last_used), N_EXPERTS - 1).astype(I32)
    pad_info = jnp.concatenate([starts + total, padded - total, ends[-1:]]).astype(I32).reshape(1, 2 * N_EXPERTS + 1)
    xs = _dispatch(h2, run_info, pad_info, loc, s_pad)
    ys = _grouped_ffn(xs, tile_expert, n_used.reshape(1), wg, wu, wd, layer)
    return ys, run_info, w_tok.T, loc.T


def kernel(x, c, ctx, c_ctx, ada_w, ada_b, norm1_w, norm2_w, w_in, b_gate, na_rpb, q_norm_w, k_norm_w, hy_conv_w, hy_conv_b,
           hy_f_w1, hy_f_b1, hy_f_w2, hy_f_b2, hy_f_w3, hy_sin_freq, hy_bias_d, w_br_a, w_br_b, w_br_c, w_out, router_w,
           router_b, exp_w_gate, exp_w_up, exp_w_down, final_norm_w):
    b, n, d = x.shape
    lc = ctx.shape[1]
    depth = ada_w.shape[0]
    t = lc + n
    cos_t, sin_t = _rope_tables(lc, n)
    xc = jnp.concatenate([ctx, x], axis=1)
    cvec = jnp.concatenate([c, c_ctx[None], jnp.zeros((SUBLANE - b - 1, d), F32)], axis=0)
    na_w = NA_HEADS * HEAD_DIM
    exp_w_gate_t = jnp.swapaxes(exp_w_gate, 2, 3)
    exp_w_up_t = jnp.swapaxes(exp_w_up, 2, 3)
    out = None
    for layer in range(depth):
        mod = _ada_mod(cvec, ada_w, ada_b, layer)[:b + 1].reshape(b + 1, N_MOD, d)
        aq, ak, av, bq, bq_lat, bk, bv, cu, cu_lat, g = _norm_inproj(xc, mod, norm1_w[layer], w_in[layer].astype(BF16),
                                                                     b_gate[layer], q_norm_w[layer], k_norm_w[layer],
                                                                     cos_t, sin_t, lc)
        bq_ctx = bq[:, :lc]
        cu_ctx = cu[:, :lc]
        filt_params = (hy_f_w1[layer], hy_f_b1[layer], hy_f_w2[layer], hy_f_b2[layer], hy_f_w3[layer], hy_sin_freq[layer])
        ya_lat = _na_attention(aq, ak, av, _na_bias_table(na_rpb[layer]), lc)
        ya_ctx = _pair_attention(aq, ak[:, :lc], av[:, :lc], q_row0=0, n_q=lc, kv_of_pair=lambda p: p, n_pairs=na_w // LANE,
                                 shared_kv=False)
        gq = GQA_Q_HEADS * HEAD_DIM // LANE
        yb_lat = _pair_attention(bq_lat, bk, bv, q_row0=0, n_q=n, kv_of_pair=lambda p: p // 2, n_pairs=gq, shared_kv=True)
        yb_ctx = _pair_attention(bq_ctx, bk[:, :lc], bv[:, :lc], q_row0=0, n_q=lc, kv_of_pair=lambda p: p // 2, n_pairs=gq,
                                 shared_kv=True)
        yc_lat = _hyena_long(cu_lat, hy_conv_w[layer], hy_conv_b[layer], filt_params, hy_bias_d[layer])
        yc_ctx = _hyena_small(cu_ctx, hy_conv_w[layer], hy_conv_b[layer], filt_params, hy_bias_d[layer])
        xc, h2, logits = _merge(xc, mod, (ya_ctx, ya_lat), (yb_ctx, yb_lat), (yc_ctx, yc_lat), g, w_br_a[layer].astype(BF16),
                                w_br_b[layer].astype(BF16), w_br_c[layer].astype(BF16), w_out[layer].astype(BF16),
                                norm2_w[layer], router_w, lc)
        logits_t = logits[:, :N_EXPERTS].T
        ys, run_info, w_t, loc_t = _moe_experts(h2, logits_t, router_b, exp_w_gate_t, exp_w_up_t, exp_w_down, layer)
        final = layer == depth - 1
        res = _moe_combine(xc, mod, ys, run_info, w_t, loc_t, final_norm_w, lc, final=final)
        if final:
            out = res
        else:
            xc = res
    return out
```

```python
import functools
import math

import jax
import jax.numpy as jnp
import numpy as np
from jax import lax
from jax.experimental import pallas as pl
from jax.experimental.pallas import tpu as pltpu

F32 = jnp.float32
BF16 = jnp.bfloat16
I32 = jnp.int32
HI = lax.Precision.HIGHEST

GRID_W = 64
HEAD_DIM = 64
RMS_EPS = 1e-6
N_MOD = 6
NA_HEADS = 4
NA_WIN_R = 8
NA_WIN_C = 16
GQA_Q_HEADS = 8
GQA_KV_HEADS = 2
ROPE_THETA = 10000.0
HY_WIDTH = 256
HY_ORDER = 2
HY_EMB_DIM = 33
HY_DECAY_TARGET = 1e-2
HY_FAST_DECAY_PCT = 0.3
HY_SLOW_DECAY_PCT = 1.5
N_BRANCH = 3
N_EXPERTS = 16
N_GROUPS = 4
EXPERTS_PER_GROUP = 4
LANE = 128
SUBLANE = 8
DFT_N2 = 128
LOG2E = math.log2(math.e)
VMEM_LIMIT = 56 * 1024 * 1024


def _cp(sem, vmem=VMEM_LIMIT, flags=None):
    return pltpu.CompilerParams(dimension_semantics=sem, vmem_limit_bytes=vmem, flags=flags)


def _dot(a, b, prec=None):
    return jnp.dot(a, b, preferred_element_type=F32, precision=prec)


def _dot_nt(a, b, prec=None):
    return lax.dot_general(a, b, (((1,), (1,)), ((), ())), preferred_element_type=F32, precision=prec)


def _silu(x):
    return x * jax.nn.sigmoid(x)


def _ada_body(c_ref, w_ref, b_ref, o_ref):
    o_ref[...] = _dot(_silu(c_ref[...]), w_ref[0], HI) + b_ref[0]


def _ada_mod(cvec, ada_w, ada_b, layer):
    rows, d = cvec.shape
    depth, _, cols = ada_w.shape
    tn = 1536
    return pl.pallas_call(
        _ada_body,
        grid=(cols // tn,),
        in_specs=[pl.BlockSpec((rows, d), lambda j: (0, 0)),
                  pl.BlockSpec((1, d, tn), lambda j: (layer, 0, j)),
                  pl.BlockSpec((1, 1, tn), lambda j: (layer, 0, j))],
        out_specs=pl.BlockSpec((rows, tn), lambda j: (0, j)),
        out_shape=jax.ShapeDtypeStruct((rows, cols), F32),
        compiler_params=_cp(("arbitrary",)),
        name="ada_mod",
    )(cvec, ada_w, ada_b.reshape(depth, 1, cols))


def _head_rms(x, ones_bd, w_row):
    x2 = x * x
    hi = x2.astype(BF16)
    lo = (x2 - hi.astype(F32)).astype(BF16)
    ss = _dot(hi, ones_bd) + _dot(lo, ones_bd)
    return x * lax.rsqrt(ss * (1.0 / HEAD_DIM) + RMS_EPS) * w_row


def _rope(x, cos_t, sin_t):
    n = x.shape[1]
    half = HEAD_DIM // 2
    lane = lax.broadcasted_iota(I32, x.shape, 1)
    first = (lane % HEAD_DIM) < half
    swapped = jnp.where(first, pltpu.roll(x, n - half, 1), pltpu.roll(x, half, 1))
    return x * cos_t + swapped * sin_t


def _inproj_body(x_ref, mod_ref, nw_ref, w_ref, bg_ref, qw_ref, kw_ref, cos_ref, sin_ref, bdq_ref, bdk_ref, dup_ref, dupv_ref,
                 aq_ref, ak_ref, av_ref, bq_ref, bql_ref, bk_ref, bv_ref, cu_ref, cul_ref, g_ref):
    x = x_ref[0]
    mod = mod_ref[0]
    y = x * lax.rsqrt(jnp.mean(x * x, axis=-1, keepdims=True) + RMS_EPS) * nw_ref[...]
    h = (y * (1.0 + mod[1:2]) + mod[0:1]).astype(BF16)
    scale = HEAD_DIM ** -0.5 * LOG2E
    na = NA_HEADS * HEAD_DIM
    qg = GQA_Q_HEADS * HEAD_DIM
    kg = GQA_KV_HEADS * HEAD_DIM
    o = 0
    pa = _dot(h, w_ref[:, o:o + 3 * na])
    aq_ref[0] = (pa[:, :na] * scale).astype(BF16)
    ak_ref[0] = pa[:, na:2 * na].astype(BF16)
    av_ref[0] = pa[:, 2 * na:].astype(BF16)
    o += 3 * na
    cos2 = cos_ref[...]
    sin2 = sin_ref[...]
    pq = _dot(h, w_ref[:, o:o + qg])
    qn = _head_rms(pq, bdq_ref[...], qw_ref[...])
    reps = qg // LANE
    qr = _rope(qn, jnp.concatenate([cos2] * reps, axis=1), jnp.concatenate([sin2] * reps, axis=1))
    bq = (qr * scale).astype(BF16)
    bq_ref[0] = bq
    bql_ref[0] = bq
    o += qg
    pkv = _dot(h, w_ref[:, o:o + 2 * kg])
    kn = _head_rms(pkv[:, :kg], bdk_ref[...], kw_ref[...])
    kr = _rope(kn, cos2, sin2).astype(BF16)
    bk_ref[0] = _dot(kr, dup_ref[...]).astype(BF16)
    lane_v = lax.broadcasted_iota(I32, (1, 2 * kg), 1)
    ones_half = ((lane_v % LANE) >= HEAD_DIM).astype(F32)
    bv_ref[0] = (_dot(pkv[:, kg:].astype(BF16), dupv_ref[...]) + ones_half).astype(BF16)
    o += 2 * kg
    cu = _dot(h, w_ref[:, o:o + 3 * HY_WIDTH])
    cu_ref[0] = cu
    cul_ref[0] = cu
    o += 3 * HY_WIDTH
    g_ref[0] = jax.nn.sigmoid(_dot(h, w_ref[:, o:]) + bg_ref[...]).astype(BF16)


def _norm_inproj(xc, mod, norm_w, w_in_bf, b_gate, q_norm_w, k_norm_w, cos_t, sin_t, lc):
    b, t, d = xc.shape
    tm = 256
    na = NA_HEADS * HEAD_DIM
    qg = GQA_Q_HEADS * HEAD_DIM
    kg = GQA_KV_HEADS * HEAD_DIM
    ng = N_BRANCH * d
    nb = b
    bdq = jnp.asarray(np.kron(np.eye(GQA_Q_HEADS), np.ones((HEAD_DIM, HEAD_DIM))), BF16)
    bdk = jnp.asarray(np.kron(np.eye(GQA_KV_HEADS), np.ones((HEAD_DIM, HEAD_DIM))), BF16)
    dup_np = np.zeros((kg, 2 * kg), np.float32)
    dupv_np = np.zeros((kg, 2 * kg), np.float32)
    for g in range(GQA_KV_HEADS):
        for r in range(2):
            dup_np[g * HEAD_DIM:(g + 1) * HEAD_DIM, (2 * g + r) * HEAD_DIM:(2 * g + r + 1) * HEAD_DIM] = np.eye(HEAD_DIM)
        dupv_np[g * HEAD_DIM:(g + 1) * HEAD_DIM, 2 * g * HEAD_DIM:(2 * g + 1) * HEAD_DIM] = np.eye(HEAD_DIM)
    dup = jnp.asarray(dup_np, BF16)
    dupv = jnp.asarray(dupv_np, BF16)
    qw = jnp.tile(q_norm_w.astype(F32), GQA_Q_HEADS).reshape(1, qg)
    kw = jnp.tile(k_norm_w.astype(F32), GQA_KV_HEADS).reshape(1, kg)
    ctx_tiles = lc // tm
    full = lambda shape: pl.BlockSpec(shape, lambda bi, ti: (0,) * len(shape))
    tok = lambda width: pl.BlockSpec((1, tm, width), lambda bi, ti: (bi, ti, 0))
    lat_tok = lambda width: pl.BlockSpec((1, tm, width), lambda bi, ti: (bi, jnp.maximum(ti - ctx_tiles, 0), 0))
    cw = 3 * HY_WIDTH
    outs = [(na, BF16, t, tok), (na, BF16, t, tok), (na, BF16, t, tok), (qg, BF16, t, tok), (qg, BF16, t - lc, lat_tok),
            (2 * kg, BF16, t, tok), (2 * kg, BF16, t, tok), (cw, F32, t, tok), (cw, F32, t - lc, lat_tok), (ng, BF16, t, tok)]
    return pl.pallas_call(
        _inproj_body,
        grid=(b, t // tm),
        in_specs=[tok(d),
                  pl.BlockSpec((1, N_MOD, d), lambda bi, ti: (jnp.where(ti < ctx_tiles, nb, bi), 0, 0)),
                  full((1, d)), full(w_in_bf.shape), full((1, ng)), full((1, qg)), full((1, kg)),
                  pl.BlockSpec((tm, LANE), lambda bi, ti: (ti, 0)),
                  pl.BlockSpec((tm, LANE), lambda bi, ti: (ti, 0)),
                  full(bdq.shape), full(bdk.shape), full(dup.shape), full(dupv.shape)],
        out_specs=[spec(w) for w, _, _, spec in outs],
        out_shape=[jax.ShapeDtypeStruct((b, rows, w), dt) for w, dt, rows, _ in outs],
        compiler_params=_cp(("arbitrary", "arbitrary")),
        name="norm_inproj",
    )(xc, mod, norm_w.reshape(1, d), w_in_bf, b_gate.reshape(1, ng), qw, kw, cos_t, sin_t, bdq, bdk, dup, dupv)


def _rope_tables(lc, n):
    tpos = jnp.arange(n, dtype=I32)
    rows = (tpos // GRID_W).astype(F32)
    cols = (tpos % GRID_W).astype(F32)
    d_axis = HEAD_DIM // 2
    inv = ROPE_THETA ** (-jnp.arange(0, d_axis, 2, dtype=F32) / d_axis)
    ang = jnp.concatenate([rows[:, None] * inv, cols[:, None] * inv], axis=-1)
    cos, sin = jnp.cos(ang), jnp.sin(ang)
    cos_h = jnp.concatenate([cos, cos], axis=-1)
    sin_h = jnp.concatenate([-sin, sin], axis=-1)
    cos_h = jnp.concatenate([jnp.ones((lc, HEAD_DIM), F32), cos_h], axis=0)
    sin_h = jnp.concatenate([jnp.zeros((lc, HEAD_DIM), F32), sin_h], axis=0)
    return jnp.concatenate([cos_h, cos_h], axis=-1), jnp.concatenate([sin_h, sin_h], axis=-1)


NA_ROWS_PER_STEP = 4


def _na_bias_table(rpb):
    qc = np.arange(GRID_W)[:, None]
    kc = np.arange(GRID_W)[None, :]
    win_c0 = np.clip(qc - NA_WIN_C // 2, 0, GRID_W - NA_WIN_C)
    col_ok = (kc >= win_c0) & (kc < win_c0 + NA_WIN_C)
    d_col = np.clip(kc - qc + NA_WIN_C - 1, 0, 2 * NA_WIN_C - 2)
    r = rpb.astype(F32)
    dc = jnp.asarray(d_col)[None, :, None, :]
    cols = jnp.zeros((NA_HEADS, GRID_W, 2 * NA_WIN_R - 1, GRID_W), F32)
    for c in range(2 * NA_WIN_C - 1):
        cols = jnp.where(dc == c, r[:, None, :, c, None], cols)
    cols = jnp.where(jnp.asarray(col_ok)[None, :, None, :], cols, -1e30) * LOG2E
    tab = jnp.stack([cols[:, :, v:v + NA_WIN_R, :] for v in range(NA_WIN_R)], axis=0)
    return tab.reshape(NA_WIN_R, NA_HEADS * GRID_W, NA_WIN_R * GRID_W)


def _na_body(q_ref, k_ref, v_ref, bias_ref, o_ref, *, lc, n_rows):
    i = pl.program_id(1)
    nq = NA_HEADS * GRID_W
    lane_q = lax.broadcasted_iota(I32, (GRID_W, NA_HEADS * HEAD_DIM), 1) // HEAD_DIM
    lane_o = lane_q
    kctx = k_ref[0, 0:lc, :]
    vctx = v_ref[0, 0:lc, :]
    for j in range(NA_ROWS_PER_STEP):
        r = i * NA_ROWS_PER_STEP + j
        kr0 = jnp.clip(r - NA_WIN_R // 2, 0, n_rows - NA_WIN_R)
        variant = kr0 - r + NA_WIN_R - 1
        kstart = pl.multiple_of(lc + kr0 * GRID_W, GRID_W)
        kwin = k_ref[0, pl.ds(kstart, NA_WIN_R * GRID_W), :]
        vwin = v_ref[0, pl.ds(kstart, NA_WIN_R * GRID_W), :]
        q = q_ref[0, j * GRID_W:(j + 1) * GRID_W, :]
        qm = jnp.concatenate([jnp.where(lane_q == h, q, jnp.zeros_like(q)) for h in range(NA_HEADS)], axis=0)
        s_loc = _dot_nt(qm, kwin) + bias_ref[variant]
        s_ctx = _dot_nt(qm, kctx)
        m = jnp.maximum(jnp.max(s_loc, axis=-1, keepdims=True), jnp.max(s_ctx, axis=-1, keepdims=True))
        p_loc = jnp.exp2(s_loc - m)
        p_ctx = jnp.exp2(s_ctx - m)
        l = jnp.sum(p_loc, axis=-1, keepdims=True) + jnp.sum(p_ctx, axis=-1, keepdims=True)
        o = (_dot(p_loc.astype(BF16), vwin) + _dot(p_ctx.astype(BF16), vctx)) / l
        out = jnp.zeros((GRID_W, NA_HEADS * HEAD_DIM), F32)
        for h in range(NA_HEADS):
            out = jnp.where(lane_o == h, o[h * GRID_W:(h + 1) * GRID_W], out)
        o_ref[0, j * GRID_W:(j + 1) * GRID_W, :] = out.astype(o_ref.dtype)
    del nq


def _na_attention(aq, ak, av, bias_tab, lc):
    b, t, w = aq.shape
    n = t - lc
    n_rows = n // GRID_W
    assert n % GRID_W == 0 and n_rows >= NA_WIN_R and n_rows % NA_ROWS_PER_STEP == 0
    tq = NA_ROWS_PER_STEP * GRID_W
    assert lc % tq == 0
    off = lc // tq
    return pl.pallas_call(
        functools.partial(_na_body, lc=lc, n_rows=n_rows),
        grid=(b, n_rows // NA_ROWS_PER_STEP),
        in_specs=[pl.BlockSpec((1, tq, w), lambda bi, i: (bi, i + off, 0)),
                  pl.BlockSpec((1, t, w), lambda bi, i: (bi, 0, 0)),
                  pl.BlockSpec((1, t, w), lambda bi, i: (bi, 0, 0)),
                  pl.BlockSpec(bias_tab.shape, lambda bi, i: (0, 0, 0))],
        out_specs=pl.BlockSpec((1, tq, w), lambda bi, i: (bi, i, 0)),
        out_shape=jax.ShapeDtypeStruct((b, n, w), BF16),
        compiler_params=_cp(("arbitrary", "arbitrary")),
        name="na_attention",
    )(aq, ak, av, bias_tab)


PAIR_KEY_CHUNK = 256
PAIR_Q_TILE = 512
PAIR_STREAMS = 2


def _pair_attn_body(q_ref, k_ref, v_ref, o_ref, *, kc, shared_kv):
    tq = q_ref.shape[1]
    tk = k_ref.shape[1]
    hs = tq // PAIR_STREAMS
    lane = lax.broadcasted_iota(I32, (hs, LANE), 1)
    lo = lane < HEAD_DIM
    qqs = []
    for si in range(PAIR_STREAMS):
        q = q_ref[0, si * hs:(si + 1) * hs, :]
        zero = jnp.zeros_like(q)
        qqs.append(jnp.concatenate([jnp.where(lo, q, zero), jnp.where(lo, zero, q)], axis=0))
    n_chunks = tk // kc

    def step(c, carry):
        start = 0 if n_chunks == 1 else pl.multiple_of(c * kc, kc)
        kk = k_ref[0, pl.ds(start, kc), :]
        vv = v_ref[0, pl.ds(start, kc), :]
        out = []
        for qq, (m, l, acc) in zip(qqs, carry):
            s = _dot_nt(qq, kk)
            m_new = jnp.maximum(m, jnp.max(s, axis=-1, keepdims=True))
            alpha = jnp.exp2(m - m_new)
            p = jnp.exp2(s - m_new)
            if not shared_kv:
                l = alpha * l + jnp.sum(p, axis=-1, keepdims=True)
            acc = alpha * acc + _dot(p.astype(BF16), vv)
            out.append((m_new, l, acc))
        return tuple(out)

    one = (jnp.full((2 * hs, 1), -jnp.inf, F32), jnp.zeros((2 * hs, 1), F32), jnp.zeros((2 * hs, LANE), F32))
    init = (one,) * PAIR_STREAMS
    res = step(0, init) if n_chunks == 1 else lax.fori_loop(0, n_chunks, step, init, unroll=True)
    for si, (m, l, acc) in enumerate(res):
        a, bb = acc[:hs], acc[hs:]
        if shared_kv:
            oa = a / a[:, HEAD_DIM:HEAD_DIM + 1]
            ob = pltpu.roll(bb / bb[:, HEAD_DIM:HEAD_DIM + 1], HEAD_DIM, 1)
        else:
            oa = a / l[:hs]
            ob = bb / l[hs:]
        o_ref[0, si * hs:(si + 1) * hs, :] = jnp.where(lo, oa, ob).astype(o_ref.dtype)


def _pair_attention(q, k, v, *, q_row0, n_q, kv_of_pair, n_pairs, shared_kv):
    b = q.shape[0]
    tk = k.shape[1]
    tq = PAIR_Q_TILE if (q_row0 % PAIR_Q_TILE == 0 and n_q % PAIR_Q_TILE == 0) else 256
    assert q_row0 % tq == 0 and n_q % tq == 0
    kc = PAIR_KEY_CHUNK if tk % PAIR_KEY_CHUNK == 0 else tk
    off = q_row0 // tq
    return pl.pallas_call(
        functools.partial(_pair_attn_body, kc=kc, shared_kv=shared_kv),
        grid=(b, n_pairs, n_q // tq),
        in_specs=[pl.BlockSpec((1, tq, LANE), lambda bi, p, i: (bi, i + off, p)),
                  pl.BlockSpec((1, tk, LANE), lambda bi, p, i: (bi, 0, kv_of_pair(p))),
                  pl.BlockSpec((1, tk, LANE), lambda bi, p, i: (bi, 0, kv_of_pair(p)))],
        out_specs=pl.BlockSpec((1, tq, LANE), lambda bi, p, i: (bi, i, p)),
        out_shape=jax.ShapeDtypeStruct((b, n_q, n_pairs * LANE), BF16),
        compiler_params=_cp(("arbitrary", "arbitrary", "arbitrary")),
        name="pair_attention",
    )(q, k, v)


def _filter_body(w1t_ref, w1c_ref, w1s_ref, b1_ref, w2_ref, b2_ref, w3_ref, fr_ref, o_ref, *, length, rows, transposed):
    g = pl.program_id(0)
    n_total = 2 * length
    bands = (HY_EMB_DIM - 1) // 2
    f_col = 1e-4 + lax.broadcasted_iota(I32, (bands, 1), 0).astype(F32) * ((bands - 1 - 1e-4) / (bands - 1))
    nch = HY_ORDER * HY_WIDTH
    d_lo = math.log(HY_DECAY_TARGET) / HY_SLOW_DECAY_PCT
    d_hi = math.log(HY_DECAY_TARGET) / HY_FAST_DECAY_PCT
    deltas = d_lo + lax.broadcasted_iota(I32, (1, nch), 1).astype(F32) * ((d_hi - d_lo) / (nch - 1))
    freq = fr_ref[...]
    position = lambda tt: jnp.where(tt < length, tt, n_total - tt).astype(F32)
    pos_row = position(g * rows + lax.broadcasted_iota(I32, (1, rows), 1))
    arg = f_col * (pos_row * (2.0 * math.pi / length))
    pre = (w1t_ref[...] * (pos_row * (1.0 / (length - 1))) + _dot(w1c_ref[...], jnp.cos(arg), HI)
           - _dot(w1s_ref[...], jnp.sin(arg), HI))
    h1 = jnp.sin(freq * (pre + b1_ref[...]))
    h2 = jnp.sin(freq * (_dot(w2_ref[...], h1, HI) + b2_ref[...])).T
    sub = DFT_N2 if transposed else rows
    for j in range(rows // sub):
        tt = g * rows + j * sub + lax.broadcasted_iota(I32, (sub, 1), 0)
        tn = position(tt) * (1.0 / (length - 1))
        filt = _dot(h2[j * sub:(j + 1) * sub], w3_ref[0], HI) * jnp.exp(-tn * jnp.abs(deltas))
        filt = jnp.where(tt == length, 0.0, filt)
        if transposed:
            o_ref[:, j, :] = filt
        else:
            o_ref[...] = filt


def _hyena_filter(length, w1, b1, w2, b2, w3, freq, *, transposed):
    hid = w1.shape[1]
    nch = HY_ORDER * HY_WIDTH
    n_total = 2 * length
    bands = (HY_EMB_DIM - 1) // 2
    rows = SUBLANE * DFT_N2 if transposed else length
    n_steps = n_total // rows
    assert n_total % rows == 0 and n_steps % 2 == 0
    w3d = w3.reshape(hid, 2, nch).transpose(1, 0, 2)
    full = lambda shape: pl.BlockSpec(shape, lambda g: (0,) * len(shape))
    if transposed:
        out_spec = pl.BlockSpec((DFT_N2, SUBLANE, nch), lambda g: (0, g, 0))
        out_shape = jax.ShapeDtypeStruct((DFT_N2, n_total // DFT_N2, nch), F32)
    else:
        out_spec = pl.BlockSpec((rows, nch), lambda g: (g, 0))
        out_shape = jax.ShapeDtypeStruct((n_total, nch), F32)
    return pl.pallas_call(
        functools.partial(_filter_body, length=length, rows=rows, transposed=transposed),
        grid=(n_steps,),
        in_specs=[full((hid, 1)), full((hid, bands)), full((hid, bands)), full((hid, 1)), full((hid, hid)), full((hid, 1)),
                  pl.BlockSpec((1, hid, nch), lambda g: (g // (n_steps // 2), 0, 0)), full((hid, 1))],
        out_specs=out_spec,
        out_shape=out_shape,
        compiler_params=_cp(("arbitrary",)),
        name="hyena_filter",
    )(w1[0:1].T, w1[1:1 + bands].T, w1[1 + bands:].T, b1.reshape(hid, 1), w2.T, b2.reshape(hid, 1), w3d, freq.reshape(hid, 1))


def _dft_tables(n1, n1_used):
    n = n1 * DFT_N2
    n2 = np.arange(DFT_N2)[:, None, None]
    k1 = np.arange(n1)[None, :, None]
    nn1 = np.arange(n1_used)[None, None, :]
    ang = 2.0 * np.pi * ((k1 * (DFT_N2 * nn1 + n2)) % n) / n
    g_fwd = np.concatenate([np.cos(ang), -np.sin(ang)], axis=1)
    g_inv = np.transpose(g_fwd, (0, 2, 1)) / n
    kk = np.arange(DFT_N2)
    a2 = 2.0 * np.pi * ((kk[:, None] * kk[None, :]) % DFT_N2) / DFT_N2
    fr, fi = np.cos(a2), -np.sin(a2)
    f2 = np.block([[fr, -fi], [fi, fr]])
    f2c = np.block([[fr, fi], [-fi, fr]])
    cast = lambda a: jnp.asarray(a.astype(np.float32)).astype(BF16)
    return cast(g_fwd), cast(g_inv), cast(f2), cast(f2c)


def _stage1_body(x_ref, g_ref, o_ref, *, n1):
    for j in range(SUBLANE):
        res = _dot(g_ref[j], x_ref[0, j].astype(BF16))
        o_ref[0, 0, :, j, :] = res[:n1]
        o_ref[0, 1, :, j, :] = res[n1:]


def _dft_stage1(x_t, g_fwd):
    bx, _, n1u, c = x_t.shape
    n1 = g_fwd.shape[1] // 2
    return pl.pallas_call(
        functools.partial(_stage1_body, n1=n1),
        grid=(bx, DFT_N2 // SUBLANE),
        in_specs=[pl.BlockSpec((1, SUBLANE, n1u, c), lambda bi, g: (bi, g, 0, 0)),
                  pl.BlockSpec((SUBLANE, 2 * n1, n1u), lambda bi, g: (g, 0, 0))],
        out_specs=pl.BlockSpec((1, 2, n1, SUBLANE, c), lambda bi, g: (bi, 0, 0, g, 0)),
        out_shape=jax.ShapeDtypeStruct((bx, 2, n1, DFT_N2, c), F32),
        compiler_params=_cp(("arbitrary", "arbitrary")),
        name="dft_stage1",
    )(x_t, g_fwd)


def _stage2_spec_body(a_ref, f2_ref, o_ref):
    for j in range(SUBLANE):
        slab = jnp.concatenate([a_ref[0, 0, j], a_ref[0, 1, j]], axis=0).astype(BF16)
        o_ref[j] = _dot(f2_ref[...], slab).astype(o_ref.dtype)


def _dft_stage2_spectrum(a, f2):
    _, _, n1, _, c = a.shape
    return pl.pallas_call(
        _stage2_spec_body,
        grid=(n1 // SUBLANE,),
        in_specs=[pl.BlockSpec((1, 2, SUBLANE, DFT_N2, c), lambda g: (0, 0, g, 0, 0)),
                  pl.BlockSpec(f2.shape, lambda g: (0, 0))],
        out_specs=pl.BlockSpec((SUBLANE, 2 * DFT_N2, c), lambda g: (g, 0, 0)),
        out_shape=jax.ShapeDtypeStruct((n1, 2 * DFT_N2, c), BF16),
        compiler_params=_cp(("arbitrary",)),
        name="dft_stage2_spectrum",
    )(a, f2)


def _stage2_conv_body(a_ref, k_ref, f2_ref, f2c_ref, o_ref):
    h = DFT_N2
    for j in range(SUBLANE):
        slab = jnp.concatenate([a_ref[0, 0, j], a_ref[0, 1, j]], axis=0).astype(BF16)
        x = _dot(f2_ref[...], slab)
        kf = k_ref[j].astype(F32)
        xr, xi, kr, ki = x[:h], x[h:], kf[:h], kf[h:]
        y = jnp.concatenate([xr * kr - xi * ki, xr * ki + xi * kr], axis=0).astype(BF16)
        bv = _dot(f2c_ref[...], y)
        o_ref[0, 0, :, j, :] = bv[:h]
        o_ref[0, 1, :, j, :] = bv[h:]


def _dft_stage2_conv(a, kspec, order, f2, f2c):
    bx, _, n1, _, c = a.shape
    return pl.pallas_call(
        _stage2_conv_body,
        grid=(bx, n1 // SUBLANE),
        in_specs=[pl.BlockSpec((1, 2, SUBLANE, DFT_N2, c), lambda bi, g: (bi, 0, g, 0, 0)),
                  pl.BlockSpec((SUBLANE, 2 * DFT_N2, c), lambda bi, g: (g, 0, order)),
                  pl.BlockSpec(f2.shape, lambda bi, g: (0, 0)),
                  pl.BlockSpec(f2c.shape, lambda bi, g: (0, 0))],
        out_specs=pl.BlockSpec((1, 2, DFT_N2, SUBLANE, c), lambda bi, g: (bi, 0, 0, g, 0)),
        out_shape=jax.ShapeDtypeStruct((bx, 2, DFT_N2, n1, c), F32),
        compiler_params=_cp(("arbitrary", "arbitrary")),
        name="dft_stage2_conv",
    )(a, kspec, f2, f2c)


def _inverse_body(b_ref, g_ref, u_ref, gate_ref, bias_ref, *rest, last, n1):
    if last:
        (o_ref,) = rest
    else:
        gf_ref, o_ref, a_ref = rest
    for j in range(SUBLANE):
        slab = jnp.concatenate([b_ref[0, 0, j], b_ref[0, 1, j]], axis=0).astype(BF16)
        y = _dot(g_ref[j], slab)
        out = gate_ref[0, j] * (y + u_ref[0, j] * bias_ref[...])
        if last:
            o_ref[0, :, j, :] = out
        else:
            o_ref[0, j] = out
            res = _dot(gf_ref[j], out.astype(BF16))
            a_ref[0, 0, :, j, :] = res[:n1]
            a_ref[0, 1, :, j, :] = res[n1:]


def _dft_inverse(bv, g_inv, u_t, gate_t, bias, g_fwd, *, last):
    bx, _, _, n1, c = bv.shape
    n1u = g_inv.shape[1]
    t_spec = pl.BlockSpec((1, SUBLANE, n1u, c), lambda bi, g: (bi, g, 0, 0))
    in_specs = [pl.BlockSpec((1, 2, SUBLANE, n1, c), lambda bi, g: (bi, 0, g, 0, 0)),
                pl.BlockSpec((SUBLANE, n1u, 2 * n1), lambda bi, g: (g, 0, 0)),
                t_spec, t_spec,
                pl.BlockSpec((1, c), lambda bi, g: (0, 0))]
    args = [bv, g_inv, u_t, gate_t, bias.reshape(1, c)]
    if last:
        out_specs = pl.BlockSpec((1, n1u, SUBLANE, c), lambda bi, g: (bi, 0, g, 0))
        out_shape = jax.ShapeDtypeStruct((bx, n1u, DFT_N2, c), F32)
    else:
        in_specs.append(pl.BlockSpec((SUBLANE, 2 * n1, n1u), lambda bi, g: (g, 0, 0)))
        args.append(g_fwd)
        out_specs = [t_spec, pl.BlockSpec((1, 2, n1, SUBLANE, c), lambda bi, g: (bi, 0, 0, g, 0))]
        out_shape = [jax.ShapeDtypeStruct((bx, DFT_N2, n1u, c), F32), jax.ShapeDtypeStruct((bx, 2, n1, DFT_N2, c), F32)]
    return pl.pallas_call(
        functools.partial(_inverse_body, last=last, n1=n1),
        grid=(bx, DFT_N2 // SUBLANE),
        in_specs=in_specs,
        out_specs=out_specs,
        out_shape=out_shape,
        compiler_params=_cp(("arbitrary", "arbitrary")),
        name="dft_inverse",
    )(*args)


def _shortconv_body(prev_ref, cur_ref, next_ref, w_ref, b_ref, v_ref, x1_ref, x2_ref, *, n_steps):
    i = pl.program_id(1)
    u = cur_ref[0]
    rows = u.shape[0]
    prev_row = jnp.where(i > 0, prev_ref[0, SUBLANE - 1:SUBLANE, :], 0.0)
    next_row = jnp.where(i < n_steps - 1, next_ref[0, 0:1, :], 0.0)
    ridx = lax.broadcasted_iota(I32, u.shape, 0)
    up = jnp.where(ridx == 0, prev_row, pltpu.roll(u, 1, 0))
    dn = jnp.where(ridx == rows - 1, next_row, pltpu.roll(u, rows - 1, 0))
    z = up * w_ref[0:1] + u * w_ref[1:2] + dn * w_ref[2:3] + b_ref[...]
    cw = HY_WIDTH
    for j in range(rows // DFT_N2):
        zj = z[j * DFT_N2:(j + 1) * DFT_N2]
        v_ref[0, :, j, :] = zj[:, :cw]
        x1_ref[0, :, j, :] = zj[:, cw:2 * cw]
        x2_ref[0, :, j, :] = zj[:, 2 * cw:]


def _short_conv_t(cu, conv_w, conv_b):
    b, length, c3 = cu.shape
    rows = SUBLANE * DFT_N2
    assert length % rows == 0
    n_steps = length // rows
    rb = rows // SUBLANE
    n1u = length // DFT_N2
    out_spec = pl.BlockSpec((1, DFT_N2, SUBLANE, HY_WIDTH), lambda bi, i: (bi, 0, i, 0))
    out_shape = jax.ShapeDtypeStruct((b, DFT_N2, n1u, HY_WIDTH), F32)
    return pl.pallas_call(
        functools.partial(_shortconv_body, n_steps=n_steps),
        grid=(b, n_steps),
        in_specs=[pl.BlockSpec((1, SUBLANE, c3), lambda bi, i: (bi, jnp.maximum(i * rb - 1, 0), 0)),
                  pl.BlockSpec((1, rows, c3), lambda bi, i: (bi, i, 0)),
                  pl.BlockSpec((1, SUBLANE, c3), lambda bi, i: (bi, jnp.minimum((i + 1) * rb, n_steps * rb - 1), 0)),
                  pl.BlockSpec((3, c3), lambda bi, i: (0, 0)),
                  pl.BlockSpec((1, c3), lambda bi, i: (0, 0))],
        out_specs=[out_spec] * 3,
        out_shape=[out_shape] * 3,
        compiler_params=_cp(("arbitrary", "arbitrary")),
        name="hyena_short_conv",
    )(cu, cu, cu, conv_w, conv_b.reshape(1, c3))


def _hyena_long(cu_lat, conv_w, conv_b, filt_params, bias_d):
    b, length, _ = cu_lat.shape
    n1u = length // DFT_N2
    n1 = 2 * n1u
    g_sig, g_inv, f2, f2c = _dft_tables(n1, n1u)
    g_full = _dft_tables(n1, n1)[0]
    filt_t = _hyena_filter(length, *filt_params, transposed=True)
    kspec = _dft_stage2_spectrum(_dft_stage1(filt_t[None], g_full), f2)
    v_t, x1_t, x2_t = _short_conv_t(cu_lat, conv_w, conv_b)
    y = v_t
    a = _dft_stage1(y, g_sig)
    for o, gate in enumerate((x1_t, x2_t)):
        bv = _dft_stage2_conv(a, kspec, o, f2, f2c)
        if o == HY_ORDER - 1:
            y = _dft_inverse(bv, g_inv, y, gate, bias_d[o], g_sig, last=True)
        else:
            y, a = _dft_inverse(bv, g_inv, y, gate, bias_d[o], g_sig, last=False)
    return y.reshape(b, length, HY_WIDTH)


def _hyena_small_body(cu_ref, w_ref, b_ref, ks_ref, ff_ref, fi_ref, bias_ref, o_ref):
    u = cu_ref[0]
    rows = u.shape[0]
    ridx = lax.broadcasted_iota(I32, u.shape, 0)
    up = jnp.where(ridx == 0, 0.0, pltpu.roll(u, 1, 0))
    dn = jnp.where(ridx == rows - 1, 0.0, pltpu.roll(u, rows - 1, 0))
    z = up * w_ref[0:1] + u * w_ref[1:2] + dn * w_ref[2:3] + b_ref[...]
    cw = HY_WIDTH
    nf = ff_ref.shape[0] // 2
    y = z[:, :cw]
    for o in range(HY_ORDER):
        gate = z[:, (o + 1) * cw:(o + 2) * cw]
        x = _dot(ff_ref[...], y.astype(BF16))
        kf = ks_ref[:, o * cw:(o + 1) * cw]
        xr, xi, kr, ki = x[:nf], x[nf:], kf[:nf], kf[nf:]
        prod = jnp.concatenate([xr * kr - xi * ki, xr * ki + xi * kr], axis=0).astype(BF16)
        conv = _dot(fi_ref[...], prod)
        y = gate * (conv + y * bias_ref[o:o + 1])
    o_ref[0] = y


def _spectrum_small_body(ff_ref, f_ref, o_ref):
    o_ref[...] = _dot(ff_ref[...], f_ref[...].astype(BF16))


def _hyena_small(cu_ctx, conv_w, conv_b, filt_params, bias_d):
    b, length, c3 = cu_ctx.shape
    n = 2 * length
    kt = np.arange(n)[:, None] * np.arange(n)[None, :]
    ang = 2.0 * np.pi * (kt % n) / n
    fwd = np.concatenate([np.cos(ang), -np.sin(ang)], axis=0)
    cast = lambda a: jnp.asarray(a.astype(np.float32)).astype(BF16)
    f_full = cast(fwd)
    f_sig = cast(fwd[:, :length])
    f_inv = cast(np.transpose(fwd[:, :length]) / n)
    filt = _hyena_filter(length, *filt_params, transposed=False)
    nch = filt.shape[1]
    kspec = pl.pallas_call(
        _spectrum_small_body,
        out_shape=jax.ShapeDtypeStruct((2 * n, nch), F32),
        compiler_params=_cp(None),
        name="hyena_small_spectrum",
    )(f_full, filt)
    full = lambda shape: pl.BlockSpec(shape, lambda bi: (0,) * len(shape))
    return pl.pallas_call(
        _hyena_small_body,
        grid=(b,),
        in_specs=[pl.BlockSpec((1, length, c3), lambda bi: (bi, 0, 0)), full((3, c3)), full((1, c3)), full(kspec.shape),
                  full(f_sig.shape), full(f_inv.shape), full(bias_d.shape)],
        out_specs=pl.BlockSpec((1, length, HY_WIDTH), lambda bi: (bi, 0, 0)),
        out_shape=jax.ShapeDtypeStruct((b, length, HY_WIDTH), F32),
        compiler_params=_cp(("arbitrary",)),
        name="hyena_small",
    )(cu_ctx, conv_w, conv_b.reshape(1, c3), kspec, f_sig, f_inv, bias_d)


def _merge_body(x_ref, mod_ref, yac_ref, yal_ref, ybc_ref, ybl_ref, ycc_ref, ycl_ref, g_ref, wa_ref, wb_ref, wc_ref, wo_ref,
                nw_ref, rwh_ref, rwl_ref, xo_ref, h_ref, lg_ref, *, ctx_tiles):
    d = x_ref.shape[2]
    is_ctx = pl.program_id(1) < ctx_tiles
    ya = jnp.where(is_ctx, yac_ref[0], yal_ref[0])
    yb = jnp.where(is_ctx, ybc_ref[0], ybl_ref[0])
    yc = jnp.where(is_ctx, ycc_ref[0], ycl_ref[0])
    g = g_ref[0].astype(F32)
    m = (g[:, :d] * _dot(ya, wa_ref[...]) + g[:, d:2 * d] * _dot(yb, wb_ref[...])
         + g[:, 2 * d:] * _dot(yc.astype(BF16), wc_ref[...]))
    y = _dot(m.astype(BF16), wo_ref[...])
    mod = mod_ref[0]
    x = x_ref[0] + mod[2:3] * y
    xo_ref[0] = x
    hn = x * lax.rsqrt(jnp.mean(x * x, axis=-1, keepdims=True) + RMS_EPS) * nw_ref[...]
    h = hn * (1.0 + mod[4:5]) + mod[3:4]
    h_ref[...] = h
    h_hi = h.astype(BF16)
    h_lo = (h - h_hi.astype(F32)).astype(BF16)
    lg_ref[...] = _dot(h_hi, rwh_ref[...]) + _dot(h_hi, rwl_ref[...]) + _dot(h_lo, rwh_ref[...])


def _merge(xc, mod, ya, yb, yc, g, w_a, w_b, w_c, w_o, norm2_w, router_w, lc):
    b, t, d = xc.shape
    tm = 256
    nb = b
    ctx_tiles = lc // tm
    nt = t // tm
    ne = router_w.shape[1]
    rw = jnp.pad(router_w.astype(F32), ((0, 0), (0, LANE - ne)))
    rw_hi = rw.astype(BF16)
    rw_lo = (rw - rw_hi.astype(F32)).astype(BF16)
    full = lambda shape: pl.BlockSpec(shape, lambda bi, ti: (0,) * len(shape))
    tok = lambda width: pl.BlockSpec((1, tm, width), lambda bi, ti: (bi, ti, 0))
    ctx_tok = lambda width: pl.BlockSpec((1, tm, width), lambda bi, ti: (bi, jnp.minimum(ti, ctx_tiles - 1), 0))
    lat_tok = lambda width: pl.BlockSpec((1, tm, width), lambda bi, ti: (bi, jnp.maximum(ti - ctx_tiles, 0), 0))
    branch_specs = []
    for pair in (ya, yb, yc):
        branch_specs += [ctx_tok(pair[0].shape[2]), lat_tok(pair[1].shape[2])]
    return pl.pallas_call(
        functools.partial(_merge_body, ctx_tiles=ctx_tiles),
        grid=(b, nt),
        in_specs=[tok(d),
                  pl.BlockSpec((1, N_MOD, d), lambda bi, ti: (jnp.where(ti < ctx_tiles, nb, bi), 0, 0))]
                 + branch_specs
                 + [tok(g.shape[2]), full(w_a.shape), full(w_b.shape), full(w_c.shape), full(w_o.shape), full((1, d)),
                    full(rw_hi.shape), full(rw_lo.shape)],
        out_specs=[tok(d),
                   pl.BlockSpec((tm, d), lambda bi, ti: (bi * nt + ti, 0)),
                   pl.BlockSpec((tm, LANE), lambda bi, ti: (bi * nt + ti, 0))],
        out_shape=[jax.ShapeDtypeStruct((b, t, d), F32),
                   jax.ShapeDtypeStruct((b * t, d), F32),
                   jax.ShapeDtypeStruct((b * t, LANE), F32)],
        compiler_params=_cp(("arbitrary", "arbitrary")),
        name="merge_outproj",
    )(xc, mod, ya[0], ya[1], yb[0], yb[1], yc[0], yc[1], g, w_a, w_b, w_c, w_o, norm2_w.reshape(1, d), rw_hi, rw_lo)


MOE_ROW_TILE = 512
MOE_TOKEN_TILE = 256
MOE_RUN_ALIGN = SUBLANE
MOE_LOCAL_ROWS = -(-(2 * MOE_TOKEN_TILE + N_EXPERTS * (MOE_RUN_ALIGN - 1)) // LANE) * LANE


def _top2_of4(a):
    m1 = jnp.maximum(jnp.maximum(a[0], a[1]), jnp.maximum(a[2], a[3]))
    i1 = jnp.where(a[0] == m1, 0, jnp.where(a[1] == m1, 1, jnp.where(a[2] == m1, 2, 3)))
    neg = jnp.full_like(m1, -jnp.inf)
    r = [jnp.where(i1 == j, neg, a[j]) for j in range(4)]
    m2 = jnp.maximum(jnp.maximum(r[0], r[1]), jnp.maximum(r[2], r[3]))
    i2 = jnp.where((r[0] == m2) & (i1 != 0), 0,
                   jnp.where((r[1] == m2) & (i1 != 1), 1, jnp.where((r[2] == m2) & (i1 != 2), 2, 3)))
    return m1, m2, i1, i2


def _route_body(lg_ref, rb_ref, tri_ref, e_ref, w_ref, rank_ref, cnt_ref):
    scores = jax.nn.sigmoid(lg_ref[...])
    sel = scores + rb_ref[...]
    rows = [sel[e:e + 1] for e in range(N_EXPERTS)]
    tops = [_top2_of4(rows[4 * g:4 * g + 4]) for g in range(N_GROUPS)]
    gs = [t[0] + t[1] for t in tops]
    gmax = jnp.maximum(jnp.maximum(gs[0], gs[1]), jnp.maximum(gs[2], gs[3]))
    gi = jnp.where(gs[0] == gmax, 0, jnp.where(gs[1] == gmax, 1, jnp.where(gs[2] == gmax, 2, 3)))
    pick = lambda k: jnp.where(gi == 0, tops[0][k], jnp.where(gi == 1, tops[1][k], jnp.where(gi == 2, tops[2][k], tops[3][k])))
    e0 = gi * EXPERTS_PER_GROUP + pick(2)
    e1 = gi * EXPERTS_PER_GROUP + pick(3)
    eid = lax.broadcasted_iota(I32, scores.shape, 0)
    oh0 = eid == e0
    oh1 = eid == e1
    s0 = jnp.sum(jnp.where(oh0, scores, 0.0), axis=0, keepdims=True)
    s1 = jnp.sum(jnp.where(oh1, scores, 0.0), axis=0, keepdims=True)
    tot = s0 + s1
    e_ref[...] = jnp.concatenate([e0, e1], axis=0)
    w_ref[...] = jnp.concatenate([s0 / tot, s1 / tot], axis=0)
    oh = (oh0 | oh1).astype(BF16)
    incl = _dot(oh, tri_ref[...])
    excl = incl - 1.0
    r0 = jnp.sum(jnp.where(oh0, excl, 0.0), axis=0, keepdims=True)
    r1 = jnp.sum(jnp.where(oh1, excl, 0.0), axis=0, keepdims=True)
    rank_ref[...] = jnp.concatenate([r0, r1], axis=0).astype(I32)
    tl = incl.shape[1]
    cnt_ref[0] = jnp.broadcast_to(incl[:, tl - 1:tl], cnt_ref.shape[1:]).astype(I32)


def _route(logits_t, router_b):
    ne, t = logits_t.shape
    tl = MOE_TOKEN_TILE
    assert t % tl == 0
    tri = jnp.asarray(np.triu(np.ones((tl, tl), np.float32)), BF16)
    two = lambda dt: jax.ShapeDtypeStruct((2, t), dt)
    return pl.pallas_call(
        _route_body,
        grid=(t // tl,),
        in_specs=[pl.BlockSpec((ne, tl), lambda i: (0, i)), pl.BlockSpec((ne, 1), lambda i: (0, 0)),
                  pl.BlockSpec((tl, tl), lambda i: (0, 0))],
        out_specs=[pl.BlockSpec((2, tl), lambda i: (0, i))] * 3 + [pl.BlockSpec((1, ne, LANE), lambda i: (i, 0, 0))],
        out_shape=[two(I32), two(F32), two(I32), jax.ShapeDtypeStruct((t // tl, ne, LANE), I32)],
        compiler_params=_cp(("arbitrary",)),
        name="moe_route",
    )(logits_t, router_b.reshape(ne, 1).astype(F32), tri)


RUN_PIECES = [1 << b for b in reversed(range(MOE_RUN_ALIGN.bit_length() - 1, MOE_ROW_TILE.bit_length() - 1))]


def _when_go(cp, cond, wait):
    @pl.when(cond)
    def _():
        cp.wait() if wait else cp.start()


def _run_copies(run_ref, local_ref, slots_ref, sem, *, to_slots, wait):
    for e in range(N_EXPERTS):
        srow = run_ref[0, 0, e]
        n = run_ref[0, 0, N_EXPERTS + e]
        lrow = run_ref[0, 0, 2 * N_EXPERTS + e]
        for s in RUN_PIECES:
            s_sl = slots_ref.at[pl.ds(pl.multiple_of(srow, MOE_RUN_ALIGN), s), :]
            l_sl = local_ref.at[pl.ds(pl.multiple_of(lrow, MOE_RUN_ALIGN), s), :]
            cp = pltpu.make_async_copy(l_sl, s_sl, sem) if to_slots else pltpu.make_async_copy(s_sl, l_sl, sem)
            _when_go(cp, (n & s) != 0, wait)
            srow = srow + (n & s)
            lrow = lrow + (n & s)


def _dispatch_body(run_ref, prev_ref, pad_ref, loc_ref, h_ref, xs_ref, sem, pbuf, zbuf, zsem, *, n_tail):
    tm = h_ref.shape[0]
    step = pl.program_id(0)
    buf = step % 2

    def zero_fill(wait):
        for e in range(N_EXPERTS):
            cur = pad_ref[0, e]
            n_pad = pad_ref[0, N_EXPERTS + e]
            for s in RUN_PIECES:
                dst = xs_ref.at[pl.ds(pl.multiple_of(cur, MOE_RUN_ALIGN), s), :]
                _when_go(pltpu.make_async_copy(zbuf.at[pl.ds(0, s), :], dst, zsem), (n_pad & s) != 0, wait)
                cur = cur + (n_pad & s)
        zrows = zbuf.shape[0]
        tail = pad_ref[0, 2 * N_EXPERTS]
        for k in range(n_tail):
            row = pl.multiple_of(tail + k * zrows, zrows)
            _when_go(pltpu.make_async_copy(zbuf, xs_ref.at[pl.ds(row, zrows), :], zsem), row < xs_ref.shape[0], wait)

    @pl.when(step == 0)
    def _():
        zbuf[...] = jnp.zeros_like(zbuf)
        zero_fill(False)
        zero_fill(True)

    loc = loc_ref[...]
    rid = lax.broadcasted_iota(I32, (pbuf.shape[1], tm), 0)
    onehot = ((rid == loc[0:1]) | (rid == loc[1:2])).astype(BF16)
    pbuf[buf] = _dot(onehot, h_ref[...].astype(BF16))
    _run_copies(run_ref, pbuf.at[buf], xs_ref, sem.at[buf], to_slots=True, wait=False)

    @pl.when(step > 0)
    def _():
        _run_copies(prev_ref, pbuf.at[1 - buf], xs_ref, sem.at[1 - buf], to_slots=True, wait=True)

    @pl.when(step == pl.num_programs(0) - 1)
    def _():
        _run_copies(run_ref, pbuf.at[buf], xs_ref, sem.at[buf], to_slots=True, wait=True)


def _dispatch(h2, run_info, pad_info, loc, s_pad):
    t, d = h2.shape
    tm = MOE_TOKEN_TILE
    zrows = MOE_ROW_TILE // 2
    return pl.pallas_call(
        functools.partial(_dispatch_body, n_tail=(s_pad - 2 * t) // zrows),
        grid=(t // tm,),
        in_specs=[pl.BlockSpec((1, 1, run_info.shape[2]), lambda i: (i, 0, 0), memory_space=pltpu.SMEM),
                  pl.BlockSpec((1, 1, run_info.shape[2]), lambda i: (jnp.maximum(i - 1, 0), 0, 0), memory_space=pltpu.SMEM),
                  pl.BlockSpec(pad_info.shape, lambda i: (0, 0), memory_space=pltpu.SMEM),
                  pl.BlockSpec((2, tm), lambda i: (0, i)),
                  pl.BlockSpec((tm, d), lambda i: (i, 0))],
        out_specs=pl.BlockSpec(memory_space=pl.ANY),
        out_shape=jax.ShapeDtypeStruct((s_pad, d), F32),
        scratch_shapes=[pltpu.SemaphoreType.DMA((2,)), pltpu.VMEM((2, MOE_LOCAL_ROWS, d), F32), pltpu.VMEM((zrows, d), F32),
                        pltpu.SemaphoreType.DMA(())],
        compiler_params=pltpu.CompilerParams(dimension_semantics=("arbitrary",), has_side_effects=True,
                                             vmem_limit_bytes=VMEM_LIMIT),
        name="moe_dispatch",
    )(run_info, run_info, pad_info, loc, h2)


def _ffn_body(te_ref, nu_ref, x_ref, wg_ref, wu_ref, wd_ref, o_ref):
    i = pl.program_id(0)

    @pl.when(i < nu_ref[0])
    def _():
        x = x_ref[...].astype(BF16)
        hmid = (_silu(_dot_nt(x, wg_ref[0, 0].astype(BF16))) * _dot_nt(x, wu_ref[0, 0].astype(BF16))).astype(BF16)
        o_ref[...] = _dot(hmid, wd_ref[0, 0].astype(BF16))

    @pl.when(i >= nu_ref[0])
    def _():
        o_ref[...] = jnp.zeros_like(o_ref)


def _grouped_ffn(xs, tile_expert, n_used, wg, wu, wd, layer):
    s_pad, d = xs.shape
    tr = MOE_ROW_TILE
    n_tiles = s_pad // tr
    ff = wg.shape[2]
    grid_spec = pltpu.PrefetchScalarGridSpec(
        num_scalar_prefetch=2,
        grid=(n_tiles,),
        in_specs=[pl.BlockSpec((tr, d), lambda i, te, nu: (jnp.minimum(i, nu[0] - 1), 0)),
                  pl.BlockSpec((1, 1, ff, d), lambda i, te, nu: (layer, te[i], 0, 0)),
                  pl.BlockSpec((1, 1, ff, d), lambda i, te, nu: (layer, te[i], 0, 0)),
                  pl.BlockSpec((1, 1, ff, d), lambda i, te, nu: (layer, te[i], 0, 0))],
        out_specs=pl.BlockSpec((tr, d), lambda i, te, nu: (i, 0)),
    )
    return pl.pallas_call(
        _ffn_body,
        grid_spec=grid_spec,
        out_shape=jax.ShapeDtypeStruct((s_pad, d), F32),
        compiler_params=_cp(("arbitrary",)),
        name="moe_grouped_ffn",
    )(tile_expert, n_used, xs, wg, wu, wd)


def _combine_body(run_ref, next_ref, x_ref, mod_ref, w_ref, loc_ref, nw_ref, ys_ref, o_ref, ybuf, sem, *, final):
    tm = x_ref.shape[1]
    step = pl.program_id(0) * pl.num_programs(1) + pl.program_id(1)
    n_steps = pl.num_programs(0) * pl.num_programs(1)
    buf = step % 2

    @pl.when(step == 0)
    def _():
        ybuf[...] = jnp.zeros_like(ybuf)
        _run_copies(run_ref, ybuf.at[0], ys_ref, sem.at[0], to_slots=False, wait=False)

    @pl.when(step + 1 < n_steps)
    def _():
        _run_copies(next_ref, ybuf.at[1 - buf], ys_ref, sem.at[1 - buf], to_slots=False, wait=False)

    _run_copies(run_ref, ybuf.at[buf], ys_ref, sem.at[buf], to_slots=False, wait=True)
    y16 = ybuf[buf].astype(BF16)
    cid = lax.broadcasted_iota(I32, (tm, ybuf.shape[1]), 1)
    pick0 = (cid == loc_ref[:, 0:1]).astype(BF16)
    pick1 = (cid == loc_ref[:, 1:2]).astype(BF16)
    y = w_ref[:, 0:1] * _dot(pick0, y16) + w_ref[:, 1:2] * _dot(pick1, y16)
    x = x_ref[0] + mod_ref[0][5:6] * y
    if final:
        x = x * lax.rsqrt(jnp.mean(x * x, axis=-1, keepdims=True) + RMS_EPS) * nw_ref[...]
    o_ref[0] = x


def _moe_combine(xc, mod, ys, run_info, w_t, loc_t, final_w, lc, *, final):
    b, t, d = xc.shape
    tm = MOE_TOKEN_TILE
    nb = b
    ctx_tiles = lc // tm
    nt = t // tm
    row0 = ctx_tiles if final else 0
    n_out = t - row0 * tm
    steps_per_b = nt - row0

    def next_tile(bi, ti):
        wrap = ti == steps_per_b - 1
        nb_i = jnp.minimum(jnp.where(wrap, bi + 1, bi), b - 1)
        nt_i = jnp.where(wrap, 0, ti + 1)
        return (nb_i * nt + nt_i + row0, 0, 0)

    return pl.pallas_call(
        functools.partial(_combine_body, final=final),
        grid=(b, nt - row0),
        in_specs=[pl.BlockSpec((1, 1, run_info.shape[2]), lambda bi, ti: (bi * nt + ti + row0, 0, 0), memory_space=pltpu.SMEM),
                  pl.BlockSpec((1, 1, run_info.shape[2]), next_tile, memory_space=pltpu.SMEM),
                  pl.BlockSpec((1, tm, d), lambda bi, ti: (bi, ti + row0, 0)),
                  pl.BlockSpec((1, N_MOD, d), lambda bi, ti: (jnp.where(ti + row0 < ctx_tiles, nb, bi), 0, 0)),
                  pl.BlockSpec((tm, 2), lambda bi, ti: (bi * nt + ti + row0, 0)),
                  pl.BlockSpec((tm, 2), lambda bi, ti: (bi * nt + ti + row0, 0)),
                  pl.BlockSpec((1, d), lambda bi, ti: (0, 0)),
                  pl.BlockSpec(memory_space=pl.ANY)],
        out_specs=pl.BlockSpec((1, tm, d), lambda bi, ti: (bi, ti, 0)),
        out_shape=jax.ShapeDtypeStruct((b, n_out, d), F32),
        scratch_shapes=[pltpu.VMEM((2, MOE_LOCAL_ROWS, d), F32), pltpu.SemaphoreType.DMA((2,))],
        compiler_params=_cp(("arbitrary", "arbitrary")),
        name="moe_combine",
    )(run_info, run_info, xc, mod, w_t, loc_t, final_w.reshape(1, d), ys)


def _moe_experts(h2, logits_t, router_b, wg, wu, wd, layer):
    t = h2.shape[0]
    tr = MOE_ROW_TILE
    tm = MOE_TOKEN_TILE
    al = MOE_RUN_ALIGN
    nt = t // tm
    e_idx, w_tok, lrank, cnt = _route(logits_t, router_b)
    cnt = cnt[:, :, 0]
    run_len = (cnt + al - 1) // al * al
    total = jnp.sum(run_len, axis=0)
    padded = (total + tr - 1) // tr * tr
    ends = jnp.cumsum(padded)
    starts = ends - padded
    run_start = starts[None, :] + jnp.cumsum(run_len, axis=0) - run_len
    loc_start = jnp.cumsum(run_len, axis=1) - run_len
    run_info = jnp.concatenate([run_start, run_len, loc_start], axis=1).astype(I32).reshape(nt, 1, 3 * N_EXPERTS)
    e3 = e_idx.reshape(2, nt, tm)
    loc = jnp.zeros_like(e3)
    for e in range(N_EXPERTS):
        loc = jnp.where(e3 == e, loc_start[None, :, e, None], loc)
    loc = (loc + lrank.reshape(2, nt, tm)).reshape(2, t).astype(I32)
    s_pad = (-(-(2 * t + nt * N_EXPERTS * (al - 1)) // tr) + N_EXPERTS) * tr
    n_tiles = s_pad // tr
    tile_start = jnp.arange(n_tiles, dtype=I32) * tr
    n_used = (ends[-1] // tr).astype(I32)
    tile_expert = jnp.sum((ends[None, :] <= tile_start[:, None]).astype(I32), axis=1)
    last_used = jnp.sum((ends <= (n_used - 1) * tr).astype(I32))
    tile_expert = jnp.minimum(jnp.where(jnp.arange(n_tiles) < n_used, tile_expert, last_used), N_EXPERTS - 1).astype(I32)
    pad_info = jnp.concatenate([starts + total, padded - total, ends[-1:]]).astype(I32).reshape(1, 2 * N_EXPERTS + 1)
    xs = _dispatch(h2, run_info, pad_info, loc, s_pad)
    ys = _grouped_ffn(xs, tile_expert, n_used.reshape(1), wg, wu, wd, layer)
    return ys, run_info, w_tok.T, loc.T


def kernel(x, c, ctx, c_ctx, ada_w, ada_b, norm1_w, norm2_w, w_in, b_gate, na_rpb, q_norm_w, k_norm_w, hy_conv_w, hy_conv_b,
           hy_f_w1, hy_f_b1, hy_f_w2, hy_f_b2, hy_f_w3, hy_sin_freq, hy_bias_d, w_br_a, w_br_b, w_br_c, w_out, router_w,
           router_b, exp_w_gate, exp_w_up, exp_w_down, final_norm_w):
    b, n, d = x.shape
    lc = ctx.shape[1]
    depth = ada_w.shape[0]
    t = lc + n
    cos_t, sin_t = _rope_tables(lc, n)
    xc = jnp.concatenate([ctx, x], axis=1)
    cvec = jnp.concatenate([c, c_ctx[None], jnp.zeros((SUBLANE - b - 1, d), F32)], axis=0)
    na_w = NA_HEADS * HEAD_DIM
    exp_w_gate_t = jnp.swapaxes(exp_w_gate, 2, 3)
    exp_w_up_t = jnp.swapaxes(exp_w_up, 2, 3)
    out = None
    for layer in range(depth):
        mod = _ada_mod(cvec, ada_w, ada_b, layer)[:b + 1].reshape(b + 1, N_MOD, d)
        aq, ak, av, bq, bq_lat, bk, bv, cu, cu_lat, g = _norm_inproj(xc, mod, norm1_w[layer], w_in[layer].astype(BF16),
                                                                     b_gate[layer], q_norm_w[layer], k_norm_w[layer],
                                                                     cos_t, sin_t, lc)
        bq_ctx = bq[:, :lc]
        cu_ctx = cu[:, :lc]
        filt_params = (hy_f_w1[layer], hy_f_b1[layer], hy_f_w2[layer], hy_f_b2[layer], hy_f_w3[layer], hy_sin_freq[layer])
        ya_lat = _na_attention(aq, ak, av, _na_bias_table(na_rpb[layer]), lc)
        ya_ctx = _pair_attention(aq, ak[:, :lc], av[:, :lc], q_row0=0, n_q=lc, kv_of_pair=lambda p: p, n_pairs=na_w // LANE,
                                 shared_kv=False)
        gq = GQA_Q_HEADS * HEAD_DIM // LANE
        yb_lat = _pair_attention(bq_lat, bk, bv, q_row0=0, n_q=n, kv_of_pair=lambda p: p // 2, n_pairs=gq, shared_kv=True)
        yb_ctx = _pair_attention(bq_ctx, bk[:, :lc], bv[:, :lc], q_row0=0, n_q=lc, kv_of_pair=lambda p: p // 2, n_pairs=gq,
                                 shared_kv=True)
        yc_lat = _hyena_long(cu_lat, hy_conv_w[layer], hy_conv_b[layer], filt_params, hy_bias_d[layer])
        yc_ctx = _hyena_small(cu_ctx, hy_conv_w[layer], hy_conv_b[layer], filt_params, hy_bias_d[layer])
        xc, h2, logits = _merge(xc, mod, (ya_ctx, ya_lat), (yb_ctx, yb_lat), (yc_ctx, yc_lat), g, w_br_a[layer].astype(BF16),
                                w_br_b[layer].astype(BF16), w_br_c[layer].astype(BF16), w_out[layer].astype(BF16),
                                norm2_w[layer], router_w, lc)
        logits_t = logits[:, :N_EXPERTS].T
        ys, run_info, w_t, loc_t = _moe_experts(h2, logits_t, router_b, exp_w_gate_t, exp_w_up_t, exp_w_down, layer)
        final = layer == depth - 1
        res = _moe_combine(xc, mod, ys, run_info, w_t, loc_t, final_norm_w, lc, final=final)
        if final:
            out = res
        else:
            xc = res
    return out
```

```python
import functools
import math

import jax
import jax.numpy as jnp
import numpy as np
from jax import lax
from jax.experimental import pallas as pl
from jax.experimental.pallas import tpu as pltpu

F32 = jnp.float32
BF16 = jnp.bfloat16
I32 = jnp.int32
HI = lax.Precision.HIGHEST

GRID_W = 64
HEAD_DIM = 64
RMS_EPS = 1e-6
N_MOD = 6
NA_HEADS = 4
NA_WIN_R = 8
NA_WIN_C = 16
GQA_Q_HEADS = 8
GQA_KV_HEADS = 2
ROPE_THETA = 10000.0
HY_WIDTH = 256
HY_ORDER = 2
HY_EMB_DIM = 33
HY_DECAY_TARGET = 1e-2
HY_FAST_DECAY_PCT = 0.3
HY_SLOW_DECAY_PCT = 1.5
N_BRANCH = 3
N_EXPERTS = 16
N_GROUPS = 4
EXPERTS_PER_GROUP = 4
LANE = 128
SUBLANE = 8
DFT_N2 = 128
LOG2E = math.log2(math.e)
VMEM_LIMIT = 56 * 1024 * 1024


def _cp(sem, vmem=VMEM_LIMIT, flags=None):
    return pltpu.CompilerParams(dimension_semantics=sem, vmem_limit_bytes=vmem, flags=flags)


def _dot(a, b, prec=None):
    return jnp.dot(a, b, preferred_element_type=F32, precision=prec)


def _dot_nt(a, b, prec=None):
    return lax.dot_general(a, b, (((1,), (1,)), ((), ())), preferred_element_type=F32, precision=prec)


def _silu(x):
    return x * jax.nn.sigmoid(x)


def _ada_body(c_ref, w_ref, b_ref, o_ref):
    o_ref[...] = _dot(_silu(c_ref[...]), w_ref[0], HI) + b_ref[0]


def _ada_mod(cvec, ada_w, ada_b, layer):
    rows, d = cvec.shape
    depth, _, cols = ada_w.shape
    tn = 1536
    return pl.pallas_call(
        _ada_body,
        grid=(cols // tn,),
        in_specs=[pl.BlockSpec((rows, d), lambda j: (0, 0)),
                  pl.BlockSpec((1, d, tn), lambda j: (layer, 0, j)),
                  pl.BlockSpec((1, 1, tn), lambda j: (layer, 0, j))],
        out_specs=pl.BlockSpec((rows, tn), lambda j: (0, j)),
        out_shape=jax.ShapeDtypeStruct((rows, cols), F32),
        compiler_params=_cp(("arbitrary",)),
        name="ada_mod",
    )(cvec, ada_w, ada_b.reshape(depth, 1, cols))


def _head_rms(x, ones_bd, w_row):
    x2 = x * x
    hi = x2.astype(BF16)
    lo = (x2 - hi.astype(F32)).astype(BF16)
    ss = _dot(hi, ones_bd) + _dot(lo, ones_bd)
    return x * lax.rsqrt(ss * (1.0 / HEAD_DIM) + RMS_EPS) * w_row


def _rope(x, cos_t, sin_t):
    n = x.shape[1]
    half = HEAD_DIM // 2
    lane = lax.broadcasted_iota(I32, x.shape, 1)
    first = (lane % HEAD_DIM) < half
    swapped = jnp.where(first, pltpu.roll(x, n - half, 1), pltpu.roll(x, half, 1))
    return x * cos_t + swapped * sin_t


def _inproj_body(xctx_ref, xlat_ref, mod_ref, nw_ref, w_ref, bg_ref, qw_ref, kw_ref, cos_ref, sin_ref, bdq_ref, bdk_ref, dup_ref,
                 dupv_ref, aq_ref, ak_ref, av_ref, bq_ref, bql_ref, bk_ref, bv_ref, cu_ref, cul_ref, g_ref, *, ctx_tiles):
    x = jnp.where(pl.program_id(1) < ctx_tiles, xctx_ref[0], xlat_ref[0])
    mod = mod_ref[0]
    y = x * lax.rsqrt(jnp.mean(x * x, axis=-1, keepdims=True) + RMS_EPS) * nw_ref[...]
    h = (y * (1.0 + mod[1:2]) + mod[0:1]).astype(BF16)
    scale = HEAD_DIM ** -0.5 * LOG2E
    na = NA_HEADS * HEAD_DIM
    qg = GQA_Q_HEADS * HEAD_DIM
    kg = GQA_KV_HEADS * HEAD_DIM
    o = 0
    pa = _dot(h, w_ref[:, o:o + 3 * na])
    aq_ref[0] = (pa[:, :na] * scale).astype(BF16)
    ak_ref[0] = pa[:, na:2 * na].astype(BF16)
    av_ref[0] = pa[:, 2 * na:].astype(BF16)
    o += 3 * na
    cos2 = cos_ref[...]
    sin2 = sin_ref[...]
    pq = _dot(h, w_ref[:, o:o + qg])
    qn = _head_rms(pq, bdq_ref[...], qw_ref[...])
    reps = qg // LANE
    qr = _rope(qn, jnp.concatenate([cos2] * reps, axis=1), jnp.concatenate([sin2] * reps, axis=1))
    bq = (qr * scale).astype(BF16)
    bq_ref[0] = bq
    bql_ref[0] = bq
    o += qg
    pkv = _dot(h, w_ref[:, o:o + 2 * kg])
    kn = _head_rms(pkv[:, :kg], bdk_ref[...], kw_ref[...])
    kr = _rope(kn, cos2, sin2).astype(BF16)
    bk_ref[0] = _dot(kr, dup_ref[...]).astype(BF16)
    lane_v = lax.broadcasted_iota(I32, (1, 2 * kg), 1)
    ones_half = ((lane_v % LANE) >= HEAD_DIM).astype(F32)
    bv_ref[0] = (_dot(pkv[:, kg:].astype(BF16), dupv_ref[...]) + ones_half).astype(BF16)
    o += 2 * kg
    cu = _dot(h, w_ref[:, o:o + 3 * HY_WIDTH])
    cu_ref[0] = cu
    cul_ref[0] = cu
    o += 3 * HY_WIDTH
    g_ref[0] = jax.nn.sigmoid(_dot(h, w_ref[:, o:]) + bg_ref[...]).astype(BF16)


def _stream_inputs(stream, tm, lc):
    ctx_tiles = lc // tm
    ctx_spec = lambda d: pl.BlockSpec((1, tm, d), lambda bi, ti: (bi, jnp.minimum(ti, ctx_tiles - 1), 0))
    if isinstance(stream, tuple):
        ctx, lat = stream
        b, n, d = lat.shape
        cur_spec = pl.BlockSpec((1, tm, d), lambda bi, ti: (bi, jnp.maximum(ti - ctx_tiles, 0), 0))
        return [ctx_spec(d), cur_spec], [ctx, lat], (b, lc + n, d)
    b, t, d = stream.shape
    return [ctx_spec(d), pl.BlockSpec((1, tm, d), lambda bi, ti: (bi, ti, 0))], [stream, stream], (b, t, d)


def _norm_inproj(xc, mod, norm_w, w_in_bf, b_gate, q_norm_w, k_norm_w, cos_t, sin_t, lc):
    tm = 256
    x_specs, x_args, (b, t, d) = _stream_inputs(xc, tm, lc)
    na = NA_HEADS * HEAD_DIM
    qg = GQA_Q_HEADS * HEAD_DIM
    kg = GQA_KV_HEADS * HEAD_DIM
    ng = N_BRANCH * d
    nb = b
    bdq = jnp.asarray(np.kron(np.eye(GQA_Q_HEADS), np.ones((HEAD_DIM, HEAD_DIM))), BF16)
    bdk = jnp.asarray(np.kron(np.eye(GQA_KV_HEADS), np.ones((HEAD_DIM, HEAD_DIM))), BF16)
    dup_np = np.zeros((kg, 2 * kg), np.float32)
    dupv_np = np.zeros((kg, 2 * kg), np.float32)
    for g in range(GQA_KV_HEADS):
        for r in range(2):
            dup_np[g * HEAD_DIM:(g + 1) * HEAD_DIM, (2 * g + r) * HEAD_DIM:(2 * g + r + 1) * HEAD_DIM] = np.eye(HEAD_DIM)
        dupv_np[g * HEAD_DIM:(g + 1) * HEAD_DIM, 2 * g * HEAD_DIM:(2 * g + 1) * HEAD_DIM] = np.eye(HEAD_DIM)
    dup = jnp.asarray(dup_np, BF16)
    dupv = jnp.asarray(dupv_np, BF16)
    qw = jnp.tile(q_norm_w.astype(F32), GQA_Q_HEADS).reshape(1, qg)
    kw = jnp.tile(k_norm_w.astype(F32), GQA_KV_HEADS).reshape(1, kg)
    ctx_tiles = lc // tm
    full = lambda shape: pl.BlockSpec(shape, lambda bi, ti: (0,) * len(shape))
    tok = lambda width: pl.BlockSpec((1, tm, width), lambda bi, ti: (bi, ti, 0))
    lat_tok = lambda width: pl.BlockSpec((1, tm, width), lambda bi, ti: (bi, jnp.maximum(ti - ctx_tiles, 0), 0))
    cw = 3 * HY_WIDTH
    outs = [(na, BF16, t, tok), (na, BF16, t, tok), (na, BF16, t, tok), (qg, BF16, t, tok), (qg, BF16, t - lc, lat_tok),
            (2 * kg, BF16, t, tok), (2 * kg, BF16, t, tok), (cw, F32, t, tok), (cw, F32, t - lc, lat_tok), (ng, BF16, t, tok)]
    return pl.pallas_call(
        functools.partial(_inproj_body, ctx_tiles=ctx_tiles),
        grid=(b, t // tm),
        in_specs=x_specs
                 + [pl.BlockSpec((1, N_MOD, d), lambda bi, ti: (jnp.where(ti < ctx_tiles, nb, bi), 0, 0)),
                  full((1, d)), full(w_in_bf.shape), full((1, ng)), full((1, qg)), full((1, kg)),
                  pl.BlockSpec((tm, LANE), lambda bi, ti: (ti, 0)),
                  pl.BlockSpec((tm, LANE), lambda bi, ti: (ti, 0)),
                  full(bdq.shape), full(bdk.shape), full(dup.shape), full(dupv.shape)],
        out_specs=[spec(w) for w, _, _, spec in outs],
        out_shape=[jax.ShapeDtypeStruct((b, rows, w), dt) for w, dt, rows, _ in outs],
        compiler_params=_cp(("arbitrary", "arbitrary")),
        name="norm_inproj",
    )(*x_args, mod, norm_w.reshape(1, d), w_in_bf, b_gate.reshape(1, ng), qw, kw, cos_t, sin_t, bdq, bdk, dup, dupv)


def _rope_tables(lc, n):
    tpos = jnp.arange(n, dtype=I32)
    rows = (tpos // GRID_W).astype(F32)
    cols = (tpos % GRID_W).astype(F32)
    d_axis = HEAD_DIM // 2
    inv = ROPE_THETA ** (-jnp.arange(0, d_axis, 2, dtype=F32) / d_axis)
    ang = jnp.concatenate([rows[:, None] * inv, cols[:, None] * inv], axis=-1)
    cos, sin = jnp.cos(ang), jnp.sin(ang)
    cos_h = jnp.concatenate([cos, cos], axis=-1)
    sin_h = jnp.concatenate([-sin, sin], axis=-1)
    cos_h = jnp.concatenate([jnp.ones((lc, HEAD_DIM), F32), cos_h], axis=0)
    sin_h = jnp.concatenate([jnp.zeros((lc, HEAD_DIM), F32), sin_h], axis=0)
    return jnp.concatenate([cos_h, cos_h], axis=-1), jnp.concatenate([sin_h, sin_h], axis=-1)


NA_ROWS_PER_STEP = 4


def _na_bias_table(rpb):
    qc = np.arange(GRID_W)[:, None]
    kc = np.arange(GRID_W)[None, :]
    win_c0 = np.clip(qc - NA_WIN_C // 2, 0, GRID_W - NA_WIN_C)
    col_ok = (kc >= win_c0) & (kc < win_c0 + NA_WIN_C)
    d_col = np.clip(kc - qc + NA_WIN_C - 1, 0, 2 * NA_WIN_C - 2)
    r = rpb.astype(F32)
    dc = jnp.asarray(d_col)[None, :, None, :]
    cols = jnp.zeros((NA_HEADS, GRID_W, 2 * NA_WIN_R - 1, GRID_W), F32)
    for c in range(2 * NA_WIN_C - 1):
        cols = jnp.where(dc == c, r[:, None, :, c, None], cols)
    cols = jnp.where(jnp.asarray(col_ok)[None, :, None, :], cols, -1e30) * LOG2E
    tab = jnp.stack([cols[:, :, v:v + NA_WIN_R, :] for v in range(NA_WIN_R)], axis=0)
    return tab.reshape(NA_WIN_R, NA_HEADS * GRID_W, NA_WIN_R * GRID_W)


def _na_body(q_ref, k_ref, v_ref, bias_ref, o_ref, *, lc, n_rows):
    i = pl.program_id(1)
    nq = NA_HEADS * GRID_W
    lane_q = lax.broadcasted_iota(I32, (GRID_W, NA_HEADS * HEAD_DIM), 1) // HEAD_DIM
    lane_o = lane_q
    kctx = k_ref[0, 0:lc, :]
    vctx = v_ref[0, 0:lc, :]
    for j in range(NA_ROWS_PER_STEP):
        r = i * NA_ROWS_PER_STEP + j
        kr0 = jnp.clip(r - NA_WIN_R // 2, 0, n_rows - NA_WIN_R)
        variant = kr0 - r + NA_WIN_R - 1
        kstart = pl.multiple_of(lc + kr0 * GRID_W, GRID_W)
        kwin = k_ref[0, pl.ds(kstart, NA_WIN_R * GRID_W), :]
        vwin = v_ref[0, pl.ds(kstart, NA_WIN_R * GRID_W), :]
        q = q_ref[0, j * GRID_W:(j + 1) * GRID_W, :]
        qm = jnp.concatenate([jnp.where(lane_q == h, q, jnp.zeros_like(q)) for h in range(NA_HEADS)], axis=0)
        s_loc = _dot_nt(qm, kwin) + bias_ref[variant]
        s_ctx = _dot_nt(qm, kctx)
        m = jnp.maximum(jnp.max(s_loc, axis=-1, keepdims=True), jnp.max(s_ctx, axis=-1, keepdims=True))
        p_loc = jnp.exp2(s_loc - m)
        p_ctx = jnp.exp2(s_ctx - m)
        l = jnp.sum(p_loc, axis=-1, keepdims=True) + jnp.sum(p_ctx, axis=-1, keepdims=True)
        o = (_dot(p_loc.astype(BF16), vwin) + _dot(p_ctx.astype(BF16), vctx)) / l
        out = jnp.zeros((GRID_W, NA_HEADS * HEAD_DIM), F32)
        for h in range(NA_HEADS):
            out = jnp.where(lane_o == h, o[h * GRID_W:(h + 1) * GRID_W], out)
        o_ref[0, j * GRID_W:(j + 1) * GRID_W, :] = out.astype(o_ref.dtype)
    del nq


def _na_attention(aq, ak, av, bias_tab, lc):
    b, t, w = aq.shape
    n = t - lc
    n_rows = n // GRID_W
    assert n % GRID_W == 0 and n_rows >= NA_WIN_R and n_rows % NA_ROWS_PER_STEP == 0
    tq = NA_ROWS_PER_STEP * GRID_W
    assert lc % tq == 0
    off = lc // tq
    return pl.pallas_call(
        functools.partial(_na_body, lc=lc, n_rows=n_rows),
        grid=(b, n_rows // NA_ROWS_PER_STEP),
        in_specs=[pl.BlockSpec((1, tq, w), lambda bi, i: (bi, i + off, 0)),
                  pl.BlockSpec((1, t, w), lambda bi, i: (bi, 0, 0)),
                  pl.BlockSpec((1, t, w), lambda bi, i: (bi, 0, 0)),
                  pl.BlockSpec(bias_tab.shape, lambda bi, i: (0, 0, 0))],
        out_specs=pl.BlockSpec((1, tq, w), lambda bi, i: (bi, i, 0)),
        out_shape=jax.ShapeDtypeStruct((b, n, w), BF16),
        compiler_params=_cp(("arbitrary", "arbitrary")),
        name="na_attention",
    )(aq, ak, av, bias_tab)


PAIR_KEY_CHUNK = 256
PAIR_Q_TILE = 512
PAIR_STREAMS = 2


def _pair_attn_body(q_ref, k_ref, v_ref, o_ref, *, kc, shared_kv):
    tq = q_ref.shape[1]
    tk = k_ref.shape[1]
    hs = tq // PAIR_STREAMS
    lane = lax.broadcasted_iota(I32, (hs, LANE), 1)
    lo = lane < HEAD_DIM
    qqs = []
    for si in range(PAIR_STREAMS):
        q = q_ref[0, si * hs:(si + 1) * hs, :]
        zero = jnp.zeros_like(q)
        qqs.append(jnp.concatenate([jnp.where(lo, q, zero), jnp.where(lo, zero, q)], axis=0))
    n_chunks = tk // kc

    def step(c, carry):
        start = 0 if n_chunks == 1 else pl.multiple_of(c * kc, kc)
        kk = k_ref[0, pl.ds(start, kc), :]
        vv = v_ref[0, pl.ds(start, kc), :]
        out = []
        for qq, (m, l, acc) in zip(qqs, carry):
            s = _dot_nt(qq, kk)
            m_new = jnp.maximum(m, jnp.max(s, axis=-1, keepdims=True))
            alpha = jnp.exp2(m - m_new)
            p = jnp.exp2(s - m_new)
            if not shared_kv:
                l = alpha * l + jnp.sum(p, axis=-1, keepdims=True)
            acc = alpha * acc + _dot(p.astype(BF16), vv)
            out.append((m_new, l, acc))
        return tuple(out)

    one = (jnp.full((2 * hs, 1), -jnp.inf, F32), jnp.zeros((2 * hs, 1), F32), jnp.zeros((2 * hs, LANE), F32))
    init = (one,) * PAIR_STREAMS
    res = step(0, init) if n_chunks == 1 else lax.fori_loop(0, n_chunks, step, init, unroll=True)
    for si, (m, l, acc) in enumerate(res):
        a, bb = acc[:hs], acc[hs:]
        if shared_kv:
            oa = a / a[:, HEAD_DIM:HEAD_DIM + 1]
            ob = pltpu.roll(bb / bb[:, HEAD_DIM:HEAD_DIM + 1], HEAD_DIM, 1)
        else:
            oa = a / l[:hs]
            ob = bb / l[hs:]
        o_ref[0, si * hs:(si + 1) * hs, :] = jnp.where(lo, oa, ob).astype(o_ref.dtype)


def _pair_attention(q, k, v, *, q_row0, n_q, kv_of_pair, n_pairs, shared_kv):
    b = q.shape[0]
    tk = k.shape[1]
    tq = PAIR_Q_TILE if (q_row0 % PAIR_Q_TILE == 0 and n_q % PAIR_Q_TILE == 0) else 256
    assert q_row0 % tq == 0 and n_q % tq == 0
    kc = PAIR_KEY_CHUNK if tk % PAIR_KEY_CHUNK == 0 else tk
    off = q_row0 // tq
    return pl.pallas_call(
        functools.partial(_pair_attn_body, kc=kc, shared_kv=shared_kv),
        grid=(b, n_pairs, n_q // tq),
        in_specs=[pl.BlockSpec((1, tq, LANE), lambda bi, p, i: (bi, i + off, p)),
                  pl.BlockSpec((1, tk, LANE), lambda bi, p, i: (bi, 0, kv_of_pair(p))),
                  pl.BlockSpec((1, tk, LANE), lambda bi, p, i: (bi, 0, kv_of_pair(p)))],
        out_specs=pl.BlockSpec((1, tq, LANE), lambda bi, p, i: (bi, i, p)),
        out_shape=jax.ShapeDtypeStruct((b, n_q, n_pairs * LANE), BF16),
        compiler_params=_cp(("arbitrary", "arbitrary", "arbitrary")),
        name="pair_attention",
    )(q, k, v)


def _filter_body(w1t_ref, w1c_ref, w1s_ref, b1_ref, w2_ref, b2_ref, w3_ref, fr_ref, o_ref, *, length, rows, transposed):
    g = pl.program_id(0)
    n_total = 2 * length
    bands = (HY_EMB_DIM - 1) // 2
    f_col = 1e-4 + lax.broadcasted_iota(I32, (bands, 1), 0).astype(F32) * ((bands - 1 - 1e-4) / (bands - 1))
    nch = HY_ORDER * HY_WIDTH
    d_lo = math.log(HY_DECAY_TARGET) / HY_SLOW_DECAY_PCT
    d_hi = math.log(HY_DECAY_TARGET) / HY_FAST_DECAY_PCT
    deltas = d_lo + lax.broadcasted_iota(I32, (1, nch), 1).astype(F32) * ((d_hi - d_lo) / (nch - 1))
    freq = fr_ref[...]
    position = lambda tt: jnp.where(tt < length, tt, n_total - tt).astype(F32)
    pos_row = position(g * rows + lax.broadcasted_iota(I32, (1, rows), 1))
    arg = f_col * (pos_row * (2.0 * math.pi / length))
    pre = (w1t_ref[...] * (pos_row * (1.0 / (length - 1))) + _dot(w1c_ref[...], jnp.cos(arg), HI)
           - _dot(w1s_ref[...], jnp.sin(arg), HI))
    h1 = jnp.sin(freq * (pre + b1_ref[...]))
    h2 = jnp.sin(freq * (_dot(w2_ref[...], h1, HI) + b2_ref[...])).T
    sub = DFT_N2 if transposed else rows
    for j in range(rows // sub):
        tt = g * rows + j * sub + lax.broadcasted_iota(I32, (sub, 1), 0)
        tn = position(tt) * (1.0 / (length - 1))
        filt = _dot(h2[j * sub:(j + 1) * sub], w3_ref[0], HI) * jnp.exp(-tn * jnp.abs(deltas))
        filt = jnp.where(tt == length, 0.0, filt)
        if transposed:
            o_ref[:, j, :] = filt
        else:
            o_ref[...] = filt


def _hyena_filter(length, w1, b1, w2, b2, w3, freq, *, transposed):
    hid = w1.shape[1]
    nch = HY_ORDER * HY_WIDTH
    n_total = 2 * length
    bands = (HY_EMB_DIM - 1) // 2
    rows = SUBLANE * DFT_N2 if transposed else length
    n_steps = n_total // rows
    assert n_total % rows == 0 and n_steps % 2 == 0
    w3d = w3.reshape(hid, 2, nch).transpose(1, 0, 2)
    full = lambda shape: pl.BlockSpec(shape, lambda g: (0,) * len(shape))
    if transposed:
        out_spec = pl.BlockSpec((DFT_N2, SUBLANE, nch), lambda g: (0, g, 0))
        out_shape = jax.ShapeDtypeStruct((DFT_N2, n_total // DFT_N2, nch), F32)
    else:
        out_spec = pl.BlockSpec((rows, nch), lambda g: (g, 0))
        out_shape = jax.ShapeDtypeStruct((n_total, nch), F32)
    return pl.pallas_call(
        functools.partial(_filter_body, length=length, rows=rows, transposed=transposed),
        grid=(n_steps,),
        in_specs=[full((hid, 1)), full((hid, bands)), full((hid, bands)), full((hid, 1)), full((hid, hid)), full((hid, 1)),
                  pl.BlockSpec((1, hid, nch), lambda g: (g // (n_steps // 2), 0, 0)), full((hid, 1))],
        out_specs=out_spec,
        out_shape=out_shape,
        compiler_params=_cp(("arbitrary",)),
        name="hyena_filter",
    )(w1[0:1].T, w1[1:1 + bands].T, w1[1 + bands:].T, b1.reshape(hid, 1), w2.T, b2.reshape(hid, 1), w3d, freq.reshape(hid, 1))


def _dft_tables(n1, n1_used):
    n = n1 * DFT_N2
    n2 = np.arange(DFT_N2)[:, None, None]
    k1 = np.arange(n1)[None, :, None]
    nn1 = np.arange(n1_used)[None, None, :]
    ang = 2.0 * np.pi * ((k1 * (DFT_N2 * nn1 + n2)) % n) / n
    g_fwd = np.concatenate([np.cos(ang), -np.sin(ang)], axis=1)
    g_inv = np.transpose(g_fwd, (0, 2, 1)) / n
    kk = np.arange(DFT_N2)
    a2 = 2.0 * np.pi * ((kk[:, None] * kk[None, :]) % DFT_N2) / DFT_N2
    fr, fi = np.cos(a2), -np.sin(a2)
    f2 = np.block([[fr, -fi], [fi, fr]])
    f2c = np.block([[fr, fi], [-fi, fr]])
    cast = lambda a: jnp.asarray(a.astype(np.float32)).astype(BF16)
    return cast(g_fwd), cast(g_inv), cast(f2), cast(f2c)


def _stage1_body(x_ref, g_ref, o_ref, *, n1):
    for j in range(SUBLANE):
        res = _dot(g_ref[j], x_ref[0, j].astype(BF16))
        o_ref[0, 0, :, j, :] = res[:n1]
        o_ref[0, 1, :, j, :] = res[n1:]


def _dft_stage1(x_t, g_fwd):
    bx, _, n1u, c = x_t.shape
    n1 = g_fwd.shape[1] // 2
    return pl.pallas_call(
        functools.partial(_stage1_body, n1=n1),
        grid=(bx, DFT_N2 // SUBLANE),
        in_specs=[pl.BlockSpec((1, SUBLANE, n1u, c), lambda bi, g: (bi, g, 0, 0)),
                  pl.BlockSpec((SUBLANE, 2 * n1, n1u), lambda bi, g: (g, 0, 0))],
        out_specs=pl.BlockSpec((1, 2, n1, SUBLANE, c), lambda bi, g: (bi, 0, 0, g, 0)),
        out_shape=jax.ShapeDtypeStruct((bx, 2, n1, DFT_N2, c), F32),
        compiler_params=_cp(("arbitrary", "arbitrary")),
        name="dft_stage1",
    )(x_t, g_fwd)


def _stage2_spec_body(a_ref, f2_ref, o_ref):
    for j in range(SUBLANE):
        slab = jnp.concatenate([a_ref[0, 0, j], a_ref[0, 1, j]], axis=0).astype(BF16)
        o_ref[j] = _dot(f2_ref[...], slab).astype(o_ref.dtype)


def _dft_stage2_spectrum(a, f2):
    _, _, n1, _, c = a.shape
    return pl.pallas_call(
        _stage2_spec_body,
        grid=(n1 // SUBLANE,),
        in_specs=[pl.BlockSpec((1, 2, SUBLANE, DFT_N2, c), lambda g: (0, 0, g, 0, 0)),
                  pl.BlockSpec(f2.shape, lambda g: (0, 0))],
        out_specs=pl.BlockSpec((SUBLANE, 2 * DFT_N2, c), lambda g: (g, 0, 0)),
        out_shape=jax.ShapeDtypeStruct((n1, 2 * DFT_N2, c), BF16),
        compiler_params=_cp(("arbitrary",)),
        name="dft_stage2_spectrum",
    )(a, f2)


def _stage2_conv_body(a_ref, k_ref, f2_ref, f2c_ref, o_ref):
    h = DFT_N2
    for j in range(SUBLANE):
        slab = jnp.concatenate([a_ref[0, 0, j], a_ref[0, 1, j]], axis=0).astype(BF16)
        x = _dot(f2_ref[...], slab)
        kf = k_ref[j].astype(F32)
        xr, xi, kr, ki = x[:h], x[h:], kf[:h], kf[h:]
        y = jnp.concatenate([xr * kr - xi * ki, xr * ki + xi * kr], axis=0).astype(BF16)
        bv = _dot(f2c_ref[...], y)
        o_ref[0, 0, :, j, :] = bv[:h]
        o_ref[0, 1, :, j, :] = bv[h:]


def _dft_stage2_conv(a, kspec, order, f2, f2c):
    bx, _, n1, _, c = a.shape
    return pl.pallas_call(
        _stage2_conv_body,
        grid=(bx, n1 // SUBLANE),
        in_specs=[pl.BlockSpec((1, 2, SUBLANE, DFT_N2, c), lambda bi, g: (bi, 0, g, 0, 0)),
                  pl.BlockSpec((SUBLANE, 2 * DFT_N2, c), lambda bi, g: (g, 0, order)),
                  pl.BlockSpec(f2.shape, lambda bi, g: (0, 0)),
                  pl.BlockSpec(f2c.shape, lambda bi, g: (0, 0))],
        out_specs=pl.BlockSpec((1, 2, DFT_N2, SUBLANE, c), lambda bi, g: (bi, 0, 0, g, 0)),
        out_shape=jax.ShapeDtypeStruct((bx, 2, DFT_N2, n1, c), F32),
        compiler_params=_cp(("arbitrary", "arbitrary")),
        name="dft_stage2_conv",
    )(a, kspec, f2, f2c)


def _inverse_body(b_ref, g_ref, u_ref, gate_ref, bias_ref, *rest, last, n1):
    if last:
        (o_ref,) = rest
    else:
        gf_ref, o_ref, a_ref = rest
    for j in range(SUBLANE):
        slab = jnp.concatenate([b_ref[0, 0, j], b_ref[0, 1, j]], axis=0).astype(BF16)
        y = _dot(g_ref[j], slab)
        out = gate_ref[0, j] * (y + u_ref[0, j] * bias_ref[...])
        if last:
            o_ref[0, :, j, :] = out
        else:
            o_ref[0, j] = out
            res = _dot(gf_ref[j], out.astype(BF16))
            a_ref[0, 0, :, j, :] = res[:n1]
            a_ref[0, 1, :, j, :] = res[n1:]


def _dft_inverse(bv, g_inv, u_t, gate_t, bias, g_fwd, *, last):
    bx, _, _, n1, c = bv.shape
    n1u = g_inv.shape[1]
    t_spec = pl.BlockSpec((1, SUBLANE, n1u, c), lambda bi, g: (bi, g, 0, 0))
    in_specs = [pl.BlockSpec((1, 2, SUBLANE, n1, c), lambda bi, g: (bi, 0, g, 0, 0)),
                pl.BlockSpec((SUBLANE, n1u, 2 * n1), lambda bi, g: (g, 0, 0)),
                t_spec, t_spec,
                pl.BlockSpec((1, c), lambda bi, g: (0, 0))]
    args = [bv, g_inv, u_t, gate_t, bias.reshape(1, c)]
    if last:
        out_specs = pl.BlockSpec((1, n1u, SUBLANE, c), lambda bi, g: (bi, 0, g, 0))
        out_shape = jax.ShapeDtypeStruct((bx, n1u, DFT_N2, c), F32)
    else:
        in_specs.append(pl.BlockSpec((SUBLANE, 2 * n1, n1u), lambda bi, g: (g, 0, 0)))
        args.append(g_fwd)
        out_specs = [t_spec, pl.BlockSpec((1, 2, n1, SUBLANE, c), lambda bi, g: (bi, 0, 0, g, 0))]
        out_shape = [jax.ShapeDtypeStruct((bx, DFT_N2, n1u, c), F32), jax.ShapeDtypeStruct((bx, 2, n1, DFT_N2, c), F32)]
    return pl.pallas_call(
        functools.partial(_inverse_body, last=last, n1=n1),
        grid=(bx, DFT_N2 // SUBLANE),
        in_specs=in_specs,
        out_specs=out_specs,
        out_shape=out_shape,
        compiler_params=_cp(("arbitrary", "arbitrary")),
        name="dft_inverse",
    )(*args)


def _shortconv_body(prev_ref, cur_ref, next_ref, w_ref, b_ref, v_ref, x1_ref, x2_ref, *, n_steps):
    i = pl.program_id(1)
    u = cur_ref[0]
    rows = u.shape[0]
    prev_row = jnp.where(i > 0, prev_ref[0, SUBLANE - 1:SUBLANE, :], 0.0)
    next_row = jnp.where(i < n_steps - 1, next_ref[0, 0:1, :], 0.0)
    ridx = lax.broadcasted_iota(I32, u.shape, 0)
    up = jnp.where(ridx == 0, prev_row, pltpu.roll(u, 1, 0))
    dn = jnp.where(ridx == rows - 1, next_row, pltpu.roll(u, rows - 1, 0))
    z = up * w_ref[0:1] + u * w_ref[1:2] + dn * w_ref[2:3] + b_ref[...]
    cw = HY_WIDTH
    for j in range(rows // DFT_N2):
        zj = z[j * DFT_N2:(j + 1) * DFT_N2]
        v_ref[0, :, j, :] = zj[:, :cw]
        x1_ref[0, :, j, :] = zj[:, cw:2 * cw]
        x2_ref[0, :, j, :] = zj[:, 2 * cw:]


def _short_conv_t(cu, conv_w, conv_b):
    b, length, c3 = cu.shape
    rows = SUBLANE * DFT_N2
    assert length % rows == 0
    n_steps = length // rows
    rb = rows // SUBLANE
    n1u = length // DFT_N2
    out_spec = pl.BlockSpec((1, DFT_N2, SUBLANE, HY_WIDTH), lambda bi, i: (bi, 0, i, 0))
    out_shape = jax.ShapeDtypeStruct((b, DFT_N2, n1u, HY_WIDTH), F32)
    return pl.pallas_call(
        functools.partial(_shortconv_body, n_steps=n_steps),
        grid=(b, n_steps),
        in_specs=[pl.BlockSpec((1, SUBLANE, c3), lambda bi, i: (bi, jnp.maximum(i * rb - 1, 0), 0)),
                  pl.BlockSpec((1, rows, c3), lambda bi, i: (bi, i, 0)),
                  pl.BlockSpec((1, SUBLANE, c3), lambda bi, i: (bi, jnp.minimum((i + 1) * rb, n_steps * rb - 1), 0)),
                  pl.BlockSpec((3, c3), lambda bi, i: (0, 0)),
                  pl.BlockSpec((1, c3), lambda bi, i: (0, 0))],
        out_specs=[out_spec] * 3,
        out_shape=[out_shape] * 3,
        compiler_params=_cp(("arbitrary", "arbitrary")),
        name="hyena_short_conv",
    )(cu, cu, cu, conv_w, conv_b.reshape(1, c3))


def _hyena_long(cu_lat, conv_w, conv_b, filt_params, bias_d):
    b, length, _ = cu_lat.shape
    n1u = length // DFT_N2
    n1 = 2 * n1u
    g_sig, g_inv, f2, f2c = _dft_tables(n1, n1u)
    g_full = _dft_tables(n1, n1)[0]
    filt_t = _hyena_filter(length, *filt_params, transposed=True)
    kspec = _dft_stage2_spectrum(_dft_stage1(filt_t[None], g_full), f2)
    v_t, x1_t, x2_t = _short_conv_t(cu_lat, conv_w, conv_b)
    y = v_t
    a = _dft_stage1(y, g_sig)
    for o, gate in enumerate((x1_t, x2_t)):
        bv = _dft_stage2_conv(a, kspec, o, f2, f2c)
        if o == HY_ORDER - 1:
            y = _dft_inverse(bv, g_inv, y, gate, bias_d[o], g_sig, last=True)
        else:
            y, a = _dft_inverse(bv, g_inv, y, gate, bias_d[o], g_sig, last=False)
    return y.reshape(b, length, HY_WIDTH)


def _hyena_small_body(cu_ref, w_ref, b_ref, ks_ref, ff_ref, fi_ref, bias_ref, o_ref):
    u = cu_ref[0]
    rows = u.shape[0]
    ridx = lax.broadcasted_iota(I32, u.shape, 0)
    up = jnp.where(ridx == 0, 0.0, pltpu.roll(u, 1, 0))
    dn = jnp.where(ridx == rows - 1, 0.0, pltpu.roll(u, rows - 1, 0))
    z = up * w_ref[0:1] + u * w_ref[1:2] + dn * w_ref[2:3] + b_ref[...]
    cw = HY_WIDTH
    nf = ff_ref.shape[0] // 2
    y = z[:, :cw]
    for o in range(HY_ORDER):
        gate = z[:, (o + 1) * cw:(o + 2) * cw]
        x = _dot(ff_ref[...], y.astype(BF16))
        kf = ks_ref[:, o * cw:(o + 1) * cw]
        xr, xi, kr, ki = x[:nf], x[nf:], kf[:nf], kf[nf:]
        prod = jnp.concatenate([xr * kr - xi * ki, xr * ki + xi * kr], axis=0).astype(BF16)
        conv = _dot(fi_ref[...], prod)
        y = gate * (conv + y * bias_ref[o:o + 1])
    o_ref[0] = y


def _spectrum_small_body(ff_ref, f_ref, o_ref):
    o_ref[...] = _dot(ff_ref[...], f_ref[...].astype(BF16))


def _hyena_small(cu_ctx, conv_w, conv_b, filt_params, bias_d):
    b, length, c3 = cu_ctx.shape
    n = 2 * length
    kt = np.arange(n)[:, None] * np.arange(n)[None, :]
    ang = 2.0 * np.pi * (kt % n) / n
    fwd = np.concatenate([np.cos(ang), -np.sin(ang)], axis=0)
    cast = lambda a: jnp.asarray(a.astype(np.float32)).astype(BF16)
    f_full = cast(fwd)
    f_sig = cast(fwd[:, :length])
    f_inv = cast(np.transpose(fwd[:, :length]) / n)
    filt = _hyena_filter(length, *filt_params, transposed=False)
    nch = filt.shape[1]
    kspec = pl.pallas_call(
        _spectrum_small_body,
        out_shape=jax.ShapeDtypeStruct((2 * n, nch), F32),
        compiler_params=_cp(None),
        name="hyena_small_spectrum",
    )(f_full, filt)
    full = lambda shape: pl.BlockSpec(shape, lambda bi: (0,) * len(shape))
    return pl.pallas_call(
        _hyena_small_body,
        grid=(b,),
        in_specs=[pl.BlockSpec((1, length, c3), lambda bi: (bi, 0, 0)), full((3, c3)), full((1, c3)), full(kspec.shape),
                  full(f_sig.shape), full(f_inv.shape), full(bias_d.shape)],
        out_specs=pl.BlockSpec((1, length, HY_WIDTH), lambda bi: (bi, 0, 0)),
        out_shape=jax.ShapeDtypeStruct((b, length, HY_WIDTH), F32),
        compiler_params=_cp(("arbitrary",)),
        name="hyena_small",
    )(cu_ctx, conv_w, conv_b.reshape(1, c3), kspec, f_sig, f_inv, bias_d)


def _merge_body(xctx_ref, xcur_ref, mod_ref, yac_ref, yal_ref, ybc_ref, ybl_ref, ycc_ref, ycl_ref, g_ref, wa_ref, wb_ref,
                wc_ref, wo_ref, nw_ref, rwh_ref, rwl_ref, xo_ref, h_ref, lg_ref, *, ctx_tiles):
    d = xcur_ref.shape[2]
    is_ctx = pl.program_id(1) < ctx_tiles
    ya = jnp.where(is_ctx, yac_ref[0], yal_ref[0])
    yb = jnp.where(is_ctx, ybc_ref[0], ybl_ref[0])
    yc = jnp.where(is_ctx, ycc_ref[0], ycl_ref[0])
    g = g_ref[0].astype(F32)
    m = (g[:, :d] * _dot(ya, wa_ref[...]) + g[:, d:2 * d] * _dot(yb, wb_ref[...])
         + g[:, 2 * d:] * _dot(yc.astype(BF16), wc_ref[...]))
    y = _dot(m.astype(BF16), wo_ref[...])
    mod = mod_ref[0]
    x = jnp.where(is_ctx, xctx_ref[0], xcur_ref[0]) + mod[2:3] * y
    xo_ref[0] = x
    hn = x * lax.rsqrt(jnp.mean(x * x, axis=-1, keepdims=True) + RMS_EPS) * nw_ref[...]
    h = hn * (1.0 + mod[4:5]) + mod[3:4]
    h_ref[...] = h
    h_hi = h.astype(BF16)
    h_lo = (h - h_hi.astype(F32)).astype(BF16)
    lg_ref[...] = _dot(h_hi, rwh_ref[...]) + _dot(h_hi, rwl_ref[...]) + _dot(h_lo, rwh_ref[...])


def _merge(xc, mod, ya, yb, yc, g, w_a, w_b, w_c, w_o, norm2_w, router_w, lc):
    tm = 256
    x_specs, x_args, (b, t, d) = _stream_inputs(xc, tm, lc)
    nb = b
    ctx_tiles = lc // tm
    nt = t // tm
    ne = router_w.shape[1]
    rw = jnp.pad(router_w.astype(F32), ((0, 0), (0, LANE - ne)))
    rw_hi = rw.astype(BF16)
    rw_lo = (rw - rw_hi.astype(F32)).astype(BF16)
    full = lambda shape: pl.BlockSpec(shape, lambda bi, ti: (0,) * len(shape))
    tok = lambda width: pl.BlockSpec((1, tm, width), lambda bi, ti: (bi, ti, 0))
    ctx_tok = lambda width: pl.BlockSpec((1, tm, width), lambda bi, ti: (bi, jnp.minimum(ti, ctx_tiles - 1), 0))
    lat_tok = lambda width: pl.BlockSpec((1, tm, width), lambda bi, ti: (bi, jnp.maximum(ti - ctx_tiles, 0), 0))
    branch_specs = []
    for pair in (ya, yb, yc):
        branch_specs += [ctx_tok(pair[0].shape[2]), lat_tok(pair[1].shape[2])]
    return pl.pallas_call(
        functools.partial(_merge_body, ctx_tiles=ctx_tiles),
        grid=(b, nt),
        in_specs=x_specs
                 + [pl.BlockSpec((1, N_MOD, d), lambda bi, ti: (jnp.where(ti < ctx_tiles, nb, bi), 0, 0))]
                 + branch_specs
                 + [tok(g.shape[2]), full(w_a.shape), full(w_b.shape), full(w_c.shape), full(w_o.shape), full((1, d)),
                    full(rw_hi.shape), full(rw_lo.shape)],
        out_specs=[tok(d),
                   pl.BlockSpec((tm, d), lambda bi, ti: (bi * nt + ti, 0)),
                   pl.BlockSpec((tm, LANE), lambda bi, ti: (bi * nt + ti, 0))],
        out_shape=[jax.ShapeDtypeStruct((b, t, d), F32),
                   jax.ShapeDtypeStruct((b * t, d), F32),
                   jax.ShapeDtypeStruct((b * t, LANE), F32)],
        compiler_params=_cp(("arbitrary", "arbitrary")),
        name="merge_outproj",
    )(*x_args, mod, ya[0], ya[1], yb[0], yb[1], yc[0], yc[1], g, w_a, w_b, w_c, w_o, norm2_w.reshape(1, d), rw_hi, rw_lo)


MOE_ROW_TILE = 512
MOE_TOKEN_TILE = 256
MOE_RUN_ALIGN = SUBLANE
MOE_LOCAL_ROWS = -(-(2 * MOE_TOKEN_TILE + N_EXPERTS * (MOE_RUN_ALIGN - 1)) // LANE) * LANE


def _top2_of4(a):
    m1 = jnp.maximum(jnp.maximum(a[0], a[1]), jnp.maximum(a[2], a[3]))
    i1 = jnp.where(a[0] == m1, 0, jnp.where(a[1] == m1, 1, jnp.where(a[2] == m1, 2, 3)))
    neg = jnp.full_like(m1, -jnp.inf)
    r = [jnp.where(i1 == j, neg, a[j]) for j in range(4)]
    m2 = jnp.maximum(jnp.maximum(r[0], r[1]), jnp.maximum(r[2], r[3]))
    i2 = jnp.where((r[0] == m2) & (i1 != 0), 0,
                   jnp.where((r[1] == m2) & (i1 != 1), 1, jnp.where((r[2] == m2) & (i1 != 2), 2, 3)))
    return m1, m2, i1, i2


def _route_body(lg_ref, rb_ref, tri_ref, e_ref, w_ref, rank_ref, cnt_ref):
    scores = jax.nn.sigmoid(lg_ref[...])
    sel = scores + rb_ref[...]
    rows = [sel[e:e + 1] for e in range(N_EXPERTS)]
    tops = [_top2_of4(rows[4 * g:4 * g + 4]) for g in range(N_GROUPS)]
    gs = [t[0] + t[1] for t in tops]
    gmax = jnp.maximum(jnp.maximum(gs[0], gs[1]), jnp.maximum(gs[2], gs[3]))
    gi = jnp.where(gs[0] == gmax, 0, jnp.where(gs[1] == gmax, 1, jnp.where(gs[2] == gmax, 2, 3)))
    pick = lambda k: jnp.where(gi == 0, tops[0][k], jnp.where(gi == 1, tops[1][k], jnp.where(gi == 2, tops[2][k], tops[3][k])))
    e0 = gi * EXPERTS_PER_GROUP + pick(2)
    e1 = gi * EXPERTS_PER_GROUP + pick(3)
    eid = lax.broadcasted_iota(I32, scores.shape, 0)
    oh0 = eid == e0
    oh1 = eid == e1
    s0 = jnp.sum(jnp.where(oh0, scores, 0.0), axis=0, keepdims=True)
    s1 = jnp.sum(jnp.where(oh1, scores, 0.0), axis=0, keepdims=True)
    tot = s0 + s1
    e_ref[...] = jnp.concatenate([e0, e1], axis=0)
    w_ref[...] = jnp.concatenate([s0 / tot, s1 / tot], axis=0)
    oh = (oh0 | oh1).astype(BF16)
    incl = _dot(oh, tri_ref[...])
    excl = incl - 1.0
    r0 = jnp.sum(jnp.where(oh0, excl, 0.0), axis=0, keepdims=True)
    r1 = jnp.sum(jnp.where(oh1, excl, 0.0), axis=0, keepdims=True)
    rank_ref[...] = jnp.concatenate([r0, r1], axis=0).astype(I32)
    tl = incl.shape[1]
    cnt_ref[0] = jnp.broadcast_to(incl[:, tl - 1:tl], cnt_ref.shape[1:]).astype(I32)


def _route(logits_t, router_b):
    ne, t = logits_t.shape
    tl = MOE_TOKEN_TILE
    assert t % tl == 0
    tri = jnp.asarray(np.triu(np.ones((tl, tl), np.float32)), BF16)
    two = lambda dt: jax.ShapeDtypeStruct((2, t), dt)
    return pl.pallas_call(
        _route_body,
        grid=(t // tl,),
        in_specs=[pl.BlockSpec((ne, tl), lambda i: (0, i)), pl.BlockSpec((ne, 1), lambda i: (0, 0)),
                  pl.BlockSpec((tl, tl), lambda i: (0, 0))],
        out_specs=[pl.BlockSpec((2, tl), lambda i: (0, i))] * 3 + [pl.BlockSpec((1, ne, LANE), lambda i: (i, 0, 0))],
        out_shape=[two(I32), two(F32), two(I32), jax.ShapeDtypeStruct((t // tl, ne, LANE), I32)],
        compiler_params=_cp(("arbitrary",)),
        name="moe_route",
    )(logits_t, router_b.reshape(ne, 1).astype(F32), tri)


RUN_PIECES = [1 << b for b in reversed(range(MOE_RUN_ALIGN.bit_length() - 1, MOE_ROW_TILE.bit_length() - 1))]


def _when_go(cp, cond, wait):
    @pl.when(cond)
    def _():
        cp.wait() if wait else cp.start()


def _run_copies(run_ref, local_ref, slots_ref, sem, *, to_slots, wait):
    for e in range(N_EXPERTS):
        srow = run_ref[0, 0, e]
        n = run_ref[0, 0, N_EXPERTS + e]
        lrow = run_ref[0, 0, 2 * N_EXPERTS + e]
        for s in RUN_PIECES:
            s_sl = slots_ref.at[pl.ds(pl.multiple_of(srow, MOE_RUN_ALIGN), s), :]
            l_sl = local_ref.at[pl.ds(pl.multiple_of(lrow, MOE_RUN_ALIGN), s), :]
            cp = pltpu.make_async_copy(l_sl, s_sl, sem) if to_slots else pltpu.make_async_copy(s_sl, l_sl, sem)
            _when_go(cp, (n & s) != 0, wait)
            srow = srow + (n & s)
            lrow = lrow + (n & s)


def _dispatch_body(run_ref, prev_ref, pad_ref, loc_ref, h_ref, xs_ref, sem, pbuf, zbuf, zsem, *, n_tail):
    tm = h_ref.shape[0]
    step = pl.program_id(0)
    buf = step % 2

    def zero_fill(wait):
        for e in range(N_EXPERTS):
            cur = pad_ref[0, e]
            n_pad = pad_ref[0, N_EXPERTS + e]
            for s in RUN_PIECES:
                dst = xs_ref.at[pl.ds(pl.multiple_of(cur, MOE_RUN_ALIGN), s), :]
                _when_go(pltpu.make_async_copy(zbuf.at[pl.ds(0, s), :], dst, zsem), (n_pad & s) != 0, wait)
                cur = cur + (n_pad & s)
        zrows = zbuf.shape[0]
        tail = pad_ref[0, 2 * N_EXPERTS]
        for k in range(n_tail):
            row = pl.multiple_of(tail + k * zrows, zrows)
            _when_go(pltpu.make_async_copy(zbuf, xs_ref.at[pl.ds(row, zrows), :], zsem), row < xs_ref.shape[0], wait)

    @pl.when(step == 0)
    def _():
        zbuf[...] = jnp.zeros_like(zbuf)
        zero_fill(False)
        zero_fill(True)

    loc = loc_ref[...]
    rid = lax.broadcasted_iota(I32, (pbuf.shape[1], tm), 0)
    onehot = ((rid == loc[0:1]) | (rid == loc[1:2])).astype(BF16)
    pbuf[buf] = _dot(onehot, h_ref[...].astype(BF16))
    _run_copies(run_ref, pbuf.at[buf], xs_ref, sem.at[buf], to_slots=True, wait=False)

    @pl.when(step > 0)
    def _():
        _run_copies(prev_ref, pbuf.at[1 - buf], xs_ref, sem.at[1 - buf], to_slots=True, wait=True)

    @pl.when(step == pl.num_programs(0) - 1)
    def _():
        _run_copies(run_ref, pbuf.at[buf], xs_ref, sem.at[buf], to_slots=True, wait=True)


def _dispatch(h2, run_info, pad_info, loc, s_pad):
    t, d = h2.shape
    tm = MOE_TOKEN_TILE
    zrows = MOE_ROW_TILE // 2
    return pl.pallas_call(
        functools.partial(_dispatch_body, n_tail=(s_pad - 2 * t) // zrows),
        grid=(t // tm,),
        in_specs=[pl.BlockSpec((1, 1, run_info.shape[2]), lambda i: (i, 0, 0), memory_space=pltpu.SMEM),
                  pl.BlockSpec((1, 1, run_info.shape[2]), lambda i: (jnp.maximum(i - 1, 0), 0, 0), memory_space=pltpu.SMEM),
                  pl.BlockSpec(pad_info.shape, lambda i: (0, 0), memory_space=pltpu.SMEM),
                  pl.BlockSpec((2, tm), lambda i: (0, i)),
                  pl.BlockSpec((tm, d), lambda i: (i, 0))],
        out_specs=pl.BlockSpec(memory_space=pl.ANY),
        out_shape=jax.ShapeDtypeStruct((s_pad, d), F32),
        scratch_shapes=[pltpu.SemaphoreType.DMA((2,)), pltpu.VMEM((2, MOE_LOCAL_ROWS, d), F32), pltpu.VMEM((zrows, d), F32),
                        pltpu.SemaphoreType.DMA(())],
        compiler_params=pltpu.CompilerParams(dimension_semantics=("arbitrary",), has_side_effects=True,
                                             vmem_limit_bytes=VMEM_LIMIT),
        name="moe_dispatch",
    )(run_info, run_info, pad_info, loc, h2)


def _ffn_body(te_ref, nu_ref, x_ref, wg_ref, wu_ref, wd_ref, o_ref):
    i = pl.program_id(0)

    @pl.when(i < nu_ref[0])
    def _():
        x = x_ref[...].astype(BF16)
        hmid = (_silu(_dot_nt(x, wg_ref[0, 0].astype(BF16))) * _dot_nt(x, wu_ref[0, 0].astype(BF16))).astype(BF16)
        o_ref[...] = _dot(hmid, wd_ref[0, 0].astype(BF16))

    @pl.when(i >= nu_ref[0])
    def _():
        o_ref[...] = jnp.zeros_like(o_ref)


def _grouped_ffn(xs, tile_expert, n_used, wg, wu, wd, layer):
    s_pad, d = xs.shape
    tr = MOE_ROW_TILE
    n_tiles = s_pad // tr
    ff = wg.shape[2]
    grid_spec = pltpu.PrefetchScalarGridSpec(
        num_scalar_prefetch=2,
        grid=(n_tiles,),
        in_specs=[pl.BlockSpec((tr, d), lambda i, te, nu: (jnp.minimum(i, nu[0] - 1), 0)),
                  pl.BlockSpec((1, 1, ff, d), lambda i, te, nu: (layer, te[i], 0, 0)),
                  pl.BlockSpec((1, 1, ff, d), lambda i, te, nu: (layer, te[i], 0, 0)),
                  pl.BlockSpec((1, 1, ff, d), lambda i, te, nu: (layer, te[i], 0, 0))],
        out_specs=pl.BlockSpec((tr, d), lambda i, te, nu: (i, 0)),
    )
    return pl.pallas_call(
        _ffn_body,
        grid_spec=grid_spec,
        out_shape=jax.ShapeDtypeStruct((s_pad, d), F32),
        compiler_params=_cp(("arbitrary",)),
        name="moe_grouped_ffn",
    )(tile_expert, n_used, xs, wg, wu, wd)


def _combine_body(run_ref, next_ref, x_ref, mod_ref, w_ref, loc_ref, nw_ref, ys_ref, o_ref, ybuf, sem, *, final):
    tm = x_ref.shape[1]
    step = pl.program_id(0) * pl.num_programs(1) + pl.program_id(1)
    n_steps = pl.num_programs(0) * pl.num_programs(1)
    buf = step % 2

    @pl.when(step == 0)
    def _():
        ybuf[...] = jnp.zeros_like(ybuf)
        _run_copies(run_ref, ybuf.at[0], ys_ref, sem.at[0], to_slots=False, wait=False)

    @pl.when(step + 1 < n_steps)
    def _():
        _run_copies(next_ref, ybuf.at[1 - buf], ys_ref, sem.at[1 - buf], to_slots=False, wait=False)

    _run_copies(run_ref, ybuf.at[buf], ys_ref, sem.at[buf], to_slots=False, wait=True)
    y16 = ybuf[buf].astype(BF16)
    cid = lax.broadcasted_iota(I32, (tm, ybuf.shape[1]), 1)
    pick0 = (cid == loc_ref[:, 0:1]).astype(BF16)
    pick1 = (cid == loc_ref[:, 1:2]).astype(BF16)
    y = w_ref[:, 0:1] * _dot(pick0, y16) + w_ref[:, 1:2] * _dot(pick1, y16)
    x = x_ref[0] + mod_ref[0][5:6] * y
    if final:
        x = x * lax.rsqrt(jnp.mean(x * x, axis=-1, keepdims=True) + RMS_EPS) * nw_ref[...]
    o_ref[0] = x


def _moe_combine(xc, mod, ys, run_info, w_t, loc_t, final_w, lc, *, final):
    b, t, d = xc.shape
    tm = MOE_TOKEN_TILE
    nb = b
    ctx_tiles = lc // tm
    nt = t // tm
    row0 = ctx_tiles if final else 0
    n_out = t - row0 * tm
    steps_per_b = nt - row0

    def next_tile(bi, ti):
        wrap = ti == steps_per_b - 1
        nb_i = jnp.minimum(jnp.where(wrap, bi + 1, bi), b - 1)
        nt_i = jnp.where(wrap, 0, ti + 1)
        return (nb_i * nt + nt_i + row0, 0, 0)

    return pl.pallas_call(
        functools.partial(_combine_body, final=final),
        grid=(b, nt - row0),
        in_specs=[pl.BlockSpec((1, 1, run_info.shape[2]), lambda bi, ti: (bi * nt + ti + row0, 0, 0), memory_space=pltpu.SMEM),
                  pl.BlockSpec((1, 1, run_info.shape[2]), next_tile, memory_space=pltpu.SMEM),
                  pl.BlockSpec((1, tm, d), lambda bi, ti: (bi, ti + row0, 0)),
                  pl.BlockSpec((1, N_MOD, d), lambda bi, ti: (jnp.where(ti + row0 < ctx_tiles, nb, bi), 0, 0)),
                  pl.BlockSpec((tm, 2), lambda bi, ti: (bi * nt + ti + row0, 0)),
                  pl.BlockSpec((tm, 2), lambda bi, ti: (bi * nt + ti + row0, 0)),
                  pl.BlockSpec((1, d), lambda bi, ti: (0, 0)),
                  pl.BlockSpec(memory_space=pl.ANY)],
        out_specs=pl.BlockSpec((1, tm, d), lambda bi, ti: (bi, ti, 0)),
        out_shape=jax.ShapeDtypeStruct((b, n_out, d), F32),
        scratch_shapes=[pltpu.VMEM((2, MOE_LOCAL_ROWS, d), F32), pltpu.SemaphoreType.DMA((2,))],
        compiler_params=_cp(("arbitrary", "arbitrary")),
        name="moe_combine",
    )(run_info, run_info, xc, mod, w_t, loc_t, final_w.reshape(1, d), ys)


def _moe_experts(h2, logits_t, router_b, wg, wu, wd, layer):
    t = h2.shape[0]
    tr = MOE_ROW_TILE
    tm = MOE_TOKEN_TILE
    al = MOE_RUN_ALIGN
    nt = t // tm
    e_idx, w_tok, lrank, cnt = _route(logits_t, router_b)
    cnt = cnt[:, :, 0]
    run_len = (cnt + al - 1) // al * al
    total = jnp.sum(run_len, axis=0)
    padded = (total + tr - 1) // tr * tr
    ends = jnp.cumsum(padded)
    starts = ends - padded
    run_start = starts[None, :] + jnp.cumsum(run_len, axis=0) - run_len
    loc_start = jnp.cumsum(run_len, axis=1) - run_len
    run_info = jnp.concatenate([run_start, run_len, loc_start], axis=1).astype(I32).reshape(nt, 1, 3 * N_EXPERTS)
    e3 = e_idx.reshape(2, nt, tm)
    loc = jnp.zeros_like(e3)
    for e in range(N_EXPERTS):
        loc = jnp.where(e3 == e, loc_start[None, :, e, None], loc)
    loc = (loc + lrank.reshape(2, nt, tm)).reshape(2, t).astype(I32)
    s_pad = (-(-(2 * t + nt * N_EXPERTS * (al - 1)) // tr) + N_EXPERTS) * tr
    n_tiles = s_pad // tr
    tile_start = jnp.arange(n_tiles, dtype=I32) * tr
    n_used = (ends[-1] // tr).astype(I32)
    tile_expert = jnp.sum((ends[None, :] <= tile_start[:, None]).astype(I32), axis=1)
    last_used = jnp.sum((ends <= (n_used - 1) * tr).astype(I32))
    tile_expert = jnp.minimum(jnp.where(jnp.arange(n_tiles) < n_used, tile_expert, last_used), N_EXPERTS - 1).astype(I32)
    pad_info = jnp.concatenate([starts + total, padded - total, ends[-1:]]).astype(I32).reshape(1, 2 * N_EXPERTS + 1)
    xs = _dispatch(h2, run_info, pad_info, loc, s_pad)
    ys = _grouped_ffn(xs, tile_expert, n_used.reshape(1), wg, wu, wd, layer)
    return ys, run_info, w_tok.T, loc.T


def kernel(x, c, ctx, c_ctx, ada_w, ada_b, norm1_w, norm2_w, w_in, b_gate, na_rpb, q_norm_w, k_norm_w, hy_conv_w, hy_conv_b,
           hy_f_w1, hy_f_b1, hy_f_w2, hy_f_b2, hy_f_w3, hy_sin_freq, hy_bias_d, w_br_a, w_br_b, w_br_c, w_out, router_w,
           router_b, exp_w_gate, exp_w_up, exp_w_down, final_norm_w):
    b, n, d = x.shape
    lc = ctx.shape[1]
    depth = ada_w.shape[0]
    t = lc + n
    cos_t, sin_t = _rope_tables(lc, n)
    xc = (ctx, x)
    cvec = jnp.concatenate([c, c_ctx[None], jnp.zeros((SUBLANE - b - 1, d), F32)], axis=0)
    na_w = NA_HEADS * HEAD_DIM
    exp_w_gate_t = jnp.swapaxes(exp_w_gate, 2, 3)
    exp_w_up_t = jnp.swapaxes(exp_w_up, 2, 3)
    out = None
    for layer in range(depth):
        mod = _ada_mod(cvec, ada_w, ada_b, layer)[:b + 1].reshape(b + 1, N_MOD, d)
        aq, ak, av, bq, bq_lat, bk, bv, cu, cu_lat, g = _norm_inproj(xc, mod, norm1_w[layer], w_in[layer].astype(BF16),
                                                                     b_gate[layer], q_norm_w[layer], k_norm_w[layer],
                                                                     cos_t, sin_t, lc)
        bq_ctx = bq[:, :lc]
        cu_ctx = cu[:, :lc]
        filt_params = (hy_f_w1[layer], hy_f_b1[layer], hy_f_w2[layer], hy_f_b2[layer], hy_f_w3[layer], hy_sin_freq[layer])
        ya_lat = _na_attention(aq, ak, av, _na_bias_table(na_rpb[layer]), lc)
        ya_ctx = _pair_attention(aq, ak[:, :lc], av[:, :lc], q_row0=0, n_q=lc, kv_of_pair=lambda p: p, n_pairs=na_w // LANE,
                                 shared_kv=False)
        gq = GQA_Q_HEADS * HEAD_DIM // LANE
        yb_lat = _pair_attention(bq_lat, bk, bv, q_row0=0, n_q=n, kv_of_pair=lambda p: p // 2, n_pairs=gq, shared_kv=True)
        yb_ctx = _pair_attention(bq_ctx, bk[:, :lc], bv[:, :lc], q_row0=0, n_q=lc, kv_of_pair=lambda p: p // 2, n_pairs=gq,
                                 shared_kv=True)
        yc_lat = _hyena_long(cu_lat, hy_conv_w[layer], hy_conv_b[layer], filt_params, hy_bias_d[layer])
        yc_ctx = _hyena_small(cu_ctx, hy_conv_w[layer], hy_conv_b[layer], filt_params, hy_bias_d[layer])
        xc, h2, logits = _merge(xc, mod, (ya_ctx, ya_lat), (yb_ctx, yb_lat), (yc_ctx, yc_lat), g, w_br_a[layer].astype(BF16),
                                w_br_b[layer].astype(BF16), w_br_c[layer].astype(BF16), w_out[layer].astype(BF16),
                                norm2_w[layer], router_w, lc)
        logits_t = logits[:, :N_EXPERTS].T
        ys, run_info, w_t, loc_t = _moe_experts(h2, logits_t, router_b, exp_w_gate_t, exp_w_up_t, exp_w_down, layer)
        final = layer == depth - 1
        res = _moe_combine(xc, mod, ys, run_info, w_t, loc_t, final_norm_w, lc, final=final)
        if final:
            out = res
        else:
            xc = res
    return out
```

```python
import functools
import math

import jax
import jax.numpy as jnp
import numpy as np
from jax import lax
from jax.experimental import pallas as pl
from jax.experimental.pallas import tpu as pltpu

F32 = jnp.float32
BF16 = jnp.bfloat16
I32 = jnp.int32
HI = lax.Precision.HIGHEST

GRID_W = 64
HEAD_DIM = 64
RMS_EPS = 1e-6
N_MOD = 6
NA_HEADS = 4
NA_WIN_R = 8
NA_WIN_C = 16
GQA_Q_HEADS = 8
GQA_KV_HEADS = 2
ROPE_THETA = 10000.0
HY_WIDTH = 256
HY_ORDER = 2
HY_EMB_DIM = 33
HY_DECAY_TARGET = 1e-2
HY_FAST_DECAY_PCT = 0.3
HY_SLOW_DECAY_PCT = 1.5
N_BRANCH = 3
N_EXPERTS = 16
N_GROUPS = 4
EXPERTS_PER_GROUP = 4
LANE = 128
SUBLANE = 8
DFT_N2 = 128
LOG2E = math.log2(math.e)
VMEM_LIMIT = 56 * 1024 * 1024


def _cp(sem, vmem=VMEM_LIMIT, flags=None):
    return pltpu.CompilerParams(dimension_semantics=sem, vmem_limit_bytes=vmem, flags=flags)


def _dot(a, b, prec=None):
    return jnp.dot(a, b, preferred_element_type=F32, precision=prec)


def _dot_nt(a, b, prec=None):
    return lax.dot_general(a, b, (((1,), (1,)), ((), ())), preferred_element_type=F32, precision=prec)


def _silu(x):
    return x * jax.nn.sigmoid(x)


def _ada_body(c_ref, w_ref, b_ref, o_ref):
    o_ref[...] = _dot(_silu(c_ref[...]), w_ref[0], HI) + b_ref[0]


def _ada_mod(cvec, ada_w, ada_b, layer):
    rows, d = cvec.shape
    depth, _, cols = ada_w.shape
    tn = 1536
    return pl.pallas_call(
        _ada_body,
        grid=(cols // tn,),
        in_specs=[pl.BlockSpec((rows, d), lambda j: (0, 0)),
                  pl.BlockSpec((1, d, tn), lambda j: (layer, 0, j)),
                  pl.BlockSpec((1, 1, tn), lambda j: (layer, 0, j))],
        out_specs=pl.BlockSpec((rows, tn), lambda j: (0, j)),
        out_shape=jax.ShapeDtypeStruct((rows, cols), F32),
        compiler_params=_cp(("arbitrary",)),
        name="ada_mod",
    )(cvec, ada_w, ada_b.reshape(depth, 1, cols))


def _head_rms(x, ones_bd, w_row):
    x2 = x * x
    hi = x2.astype(BF16)
    lo = (x2 - hi.astype(F32)).astype(BF16)
    ss = _dot(hi, ones_bd) + _dot(lo, ones_bd)
    return x * lax.rsqrt(ss * (1.0 / HEAD_DIM) + RMS_EPS) * w_row


def _rope(x, cos_t, sin_t):
    n = x.shape[1]
    half = HEAD_DIM // 2
    lane = lax.broadcasted_iota(I32, x.shape, 1)
    first = (lane % HEAD_DIM) < half
    swapped = jnp.where(first, pltpu.roll(x, n - half, 1), pltpu.roll(x, half, 1))
    return x * cos_t + swapped * sin_t


def _inproj_body(xctx_ref, xlat_ref, mod_ref, nw_ref, w_ref, bg_ref, qw_ref, kw_ref, cos_ref, sin_ref, bdq_ref, bdk_ref, dup_ref,
                 dupv_ref, aq_ref, ak_ref, av_ref, bq_ref, bql_ref, bk_ref, bv_ref, cu_ref, cul_ref, g_ref, *, ctx_tiles):
    x = jnp.where(pl.program_id(1) < ctx_tiles, xctx_ref[0], xlat_ref[0])
    mod = mod_ref[0]
    y = x * lax.rsqrt(jnp.mean(x * x, axis=-1, keepdims=True) + RMS_EPS) * nw_ref[...]
    h = (y * (1.0 + mod[1:2]) + mod[0:1]).astype(BF16)
    scale = HEAD_DIM ** -0.5 * LOG2E
    na = NA_HEADS * HEAD_DIM
    qg = GQA_Q_HEADS * HEAD_DIM
    kg = GQA_KV_HEADS * HEAD_DIM
    o = 0
    pa = _dot(h, w_ref[:, o:o + 3 * na])
    aq_ref[0] = (pa[:, :na] * scale).astype(BF16)
    ak_ref[0] = pa[:, na:2 * na].astype(BF16)
    av_ref[0] = pa[:, 2 * na:].astype(BF16)
    o += 3 * na
    cos2 = cos_ref[...]
    sin2 = sin_ref[...]
    pq = _dot(h, w_ref[:, o:o + qg])
    qn = _head_rms(pq, bdq_ref[...], qw_ref[...])
    reps = qg // LANE
    qr = _rope(qn, jnp.concatenate([cos2] * reps, axis=1), jnp.concatenate([sin2] * reps, axis=1))
    bq = (qr * scale).astype(BF16)
    bq_ref[0] = bq
    bql_ref[0] = bq
    o += qg
    pkv = _dot(h, w_ref[:, o:o + 2 * kg])
    kn = _head_rms(pkv[:, :kg], bdk_ref[...], kw_ref[...])
    kr = _rope(kn, cos2, sin2).astype(BF16)
    bk_ref[0] = _dot(kr, dup_ref[...]).astype(BF16)
    lane_v = lax.broadcasted_iota(I32, (1, 2 * kg), 1)
    ones_half = ((lane_v % LANE) >= HEAD_DIM).astype(F32)
    bv_ref[0] = (_dot(pkv[:, kg:].astype(BF16), dupv_ref[...]) + ones_half).astype(BF16)
    o += 2 * kg
    cu = _dot(h, w_ref[:, o:o + 3 * HY_WIDTH])
    cu_ref[0] = cu
    cul_ref[0] = cu
    o += 3 * HY_WIDTH
    g_ref[0] = jax.nn.sigmoid(_dot(h, w_ref[:, o:]) + bg_ref[...]).astype(BF16)


def _stream_inputs(stream, tm, lc):
    ctx_tiles = lc // tm
    ctx_spec = lambda d: pl.BlockSpec((1, tm, d), lambda bi, ti: (bi, jnp.minimum(ti, ctx_tiles - 1), 0))
    if isinstance(stream, tuple):
        ctx, lat = stream
        b, n, d = lat.shape
        cur_spec = pl.BlockSpec((1, tm, d), lambda bi, ti: (bi, jnp.maximum(ti - ctx_tiles, 0), 0))
        return [ctx_spec(d), cur_spec], [ctx, lat], (b, lc + n, d)
    b, t, d = stream.shape
    return [ctx_spec(d), pl.BlockSpec((1, tm, d), lambda bi, ti: (bi, ti, 0))], [stream, stream], (b, t, d)


def _norm_inproj(xc, mod, norm_w, w_in_bf, b_gate, q_norm_w, k_norm_w, cos_t, sin_t, lc):
    tm = 256
    x_specs, x_args, (b, t, d) = _stream_inputs(xc, tm, lc)
    na = NA_HEADS * HEAD_DIM
    qg = GQA_Q_HEADS * HEAD_DIM
    kg = GQA_KV_HEADS * HEAD_DIM
    ng = N_BRANCH * d
    nb = b
    bdq = jnp.asarray(np.kron(np.eye(GQA_Q_HEADS), np.ones((HEAD_DIM, HEAD_DIM))), BF16)
    bdk = jnp.asarray(np.kron(np.eye(GQA_KV_HEADS), np.ones((HEAD_DIM, HEAD_DIM))), BF16)
    dup_np = np.zeros((kg, 2 * kg), np.float32)
    dupv_np = np.zeros((kg, 2 * kg), np.float32)
    for g in range(GQA_KV_HEADS):
        for r in range(2):
            dup_np[g * HEAD_DIM:(g + 1) * HEAD_DIM, (2 * g + r) * HEAD_DIM:(2 * g + r + 1) * HEAD_DIM] = np.eye(HEAD_DIM)
        dupv_np[g * HEAD_DIM:(g + 1) * HEAD_DIM, 2 * g * HEAD_DIM:(2 * g + 1) * HEAD_DIM] = np.eye(HEAD_DIM)
    dup = jnp.asarray(dup_np, BF16)
    dupv = jnp.asarray(dupv_np, BF16)
    qw = jnp.tile(q_norm_w.astype(F32), GQA_Q_HEADS).reshape(1, qg)
    kw = jnp.tile(k_norm_w.astype(F32), GQA_KV_HEADS).reshape(1, kg)
    ctx_tiles = lc // tm
    full = lambda shape: pl.BlockSpec(shape, lambda bi, ti: (0,) * len(shape))
    tok = lambda width: pl.BlockSpec((1, tm, width), lambda bi, ti: (bi, ti, 0))
    lat_tok = lambda width: pl.BlockSpec((1, tm, width), lambda bi, ti: (bi, jnp.maximum(ti - ctx_tiles, 0), 0))
    cw = 3 * HY_WIDTH
    outs = [(na, BF16, t, tok), (na, BF16, t, tok), (na, BF16, t, tok), (qg, BF16, t, tok), (qg, BF16, t - lc, lat_tok),
            (2 * kg, BF16, t, tok), (2 * kg, BF16, t, tok), (cw, F32, t, tok), (cw, F32, t - lc, lat_tok), (ng, BF16, t, tok)]
    return pl.pallas_call(
        functools.partial(_inproj_body, ctx_tiles=ctx_tiles),
        grid=(b, t // tm),
        in_specs=x_specs
                 + [pl.BlockSpec((1, N_MOD, d), lambda bi, ti: (jnp.where(ti < ctx_tiles, nb, bi), 0, 0)),
                  full((1, d)), full(w_in_bf.shape), full((1, ng)), full((1, qg)), full((1, kg)),
                  pl.BlockSpec((tm, LANE), lambda bi, ti: (ti, 0)),
                  pl.BlockSpec((tm, LANE), lambda bi, ti: (ti, 0)),
                  full(bdq.shape), full(bdk.shape), full(dup.shape), full(dupv.shape)],
        out_specs=[spec(w) for w, _, _, spec in outs],
        out_shape=[jax.ShapeDtypeStruct((b, rows, w), dt) for w, dt, rows, _ in outs],
        compiler_params=_cp(("arbitrary", "arbitrary")),
        name="norm_inproj",
    )(*x_args, mod, norm_w.reshape(1, d), w_in_bf, b_gate.reshape(1, ng), qw, kw, cos_t, sin_t, bdq, bdk, dup, dupv)


def _rope_tables(lc, n):
    tpos = jnp.arange(n, dtype=I32)
    rows = (tpos // GRID_W).astype(F32)
    cols = (tpos % GRID_W).astype(F32)
    d_axis = HEAD_DIM // 2
    inv = ROPE_THETA ** (-jnp.arange(0, d_axis, 2, dtype=F32) / d_axis)
    ang = jnp.concatenate([rows[:, None] * inv, cols[:, None] * inv], axis=-1)
    cos, sin = jnp.cos(ang), jnp.sin(ang)
    cos_h = jnp.concatenate([cos, cos], axis=-1)
    sin_h = jnp.concatenate([-sin, sin], axis=-1)
    cos_h = jnp.concatenate([jnp.ones((lc, HEAD_DIM), F32), cos_h], axis=0)
    sin_h = jnp.concatenate([jnp.zeros((lc, HEAD_DIM), F32), sin_h], axis=0)
    return jnp.concatenate([cos_h, cos_h], axis=-1), jnp.concatenate([sin_h, sin_h], axis=-1)


NA_ROWS_PER_STEP = 4


def _na_bias_table(rpb):
    qc = np.arange(GRID_W)[:, None]
    kc = np.arange(GRID_W)[None, :]
    win_c0 = np.clip(qc - NA_WIN_C // 2, 0, GRID_W - NA_WIN_C)
    col_ok = (kc >= win_c0) & (kc < win_c0 + NA_WIN_C)
    d_col = np.clip(kc - qc + NA_WIN_C - 1, 0, 2 * NA_WIN_C - 2)
    r = rpb.astype(F32)
    dc = jnp.asarray(d_col)[None, :, None, :]
    cols = jnp.zeros((NA_HEADS, GRID_W, 2 * NA_WIN_R - 1, GRID_W), F32)
    for c in range(2 * NA_WIN_C - 1):
        cols = jnp.where(dc == c, r[:, None, :, c, None], cols)
    cols = jnp.where(jnp.asarray(col_ok)[None, :, None, :], cols, -1e30) * LOG2E
    tab = jnp.stack([cols[:, :, v:v + NA_WIN_R, :] for v in range(NA_WIN_R)], axis=0)
    return tab.reshape(NA_WIN_R, NA_HEADS * GRID_W, NA_WIN_R * GRID_W)


def _na_body(q_ref, k_ref, v_ref, bias_ref, o_ref, *, lc, n_rows):
    i = pl.program_id(1)
    nq = NA_HEADS * GRID_W
    lane_q = lax.broadcasted_iota(I32, (GRID_W, NA_HEADS * HEAD_DIM), 1) // HEAD_DIM
    lane_o = lane_q
    kctx = k_ref[0, 0:lc, :]
    vctx = v_ref[0, 0:lc, :]
    for j in range(NA_ROWS_PER_STEP):
        r = i * NA_ROWS_PER_STEP + j
        kr0 = jnp.clip(r - NA_WIN_R // 2, 0, n_rows - NA_WIN_R)
        variant = kr0 - r + NA_WIN_R - 1
        kstart = pl.multiple_of(lc + kr0 * GRID_W, GRID_W)
        kwin = k_ref[0, pl.ds(kstart, NA_WIN_R * GRID_W), :]
        vwin = v_ref[0, pl.ds(kstart, NA_WIN_R * GRID_W), :]
        q = q_ref[0, j * GRID_W:(j + 1) * GRID_W, :]
        qm = jnp.concatenate([jnp.where(lane_q == h, q, jnp.zeros_like(q)) for h in range(NA_HEADS)], axis=0)
        s_loc = _dot_nt(qm, kwin) + bias_ref[variant]
        s_ctx = _dot_nt(qm, kctx)
        m = jnp.maximum(jnp.max(s_loc, axis=-1, keepdims=True), jnp.max(s_ctx, axis=-1, keepdims=True))
        p_loc = jnp.exp2(s_loc - m)
        p_ctx = jnp.exp2(s_ctx - m)
        l = jnp.sum(p_loc, axis=-1, keepdims=True) + jnp.sum(p_ctx, axis=-1, keepdims=True)
        o = (_dot(p_loc.astype(BF16), vwin) + _dot(p_ctx.astype(BF16), vctx)) / l
        out = jnp.zeros((GRID_W, NA_HEADS * HEAD_DIM), F32)
        for h in range(NA_HEADS):
            out = jnp.where(lane_o == h, o[h * GRID_W:(h + 1) * GRID_W], out)
        o_ref[0, j * GRID_W:(j + 1) * GRID_W, :] = out.astype(o_ref.dtype)
    del nq


def _na_attention(aq, ak, av, bias_tab, lc):
    b, t, w = aq.shape
    n = t - lc
    n_rows = n // GRID_W
    assert n % GRID_W == 0 and n_rows >= NA_WIN_R and n_rows % NA_ROWS_PER_STEP == 0
    tq = NA_ROWS_PER_STEP * GRID_W
    assert lc % tq == 0
    off = lc // tq
    return pl.pallas_call(
        functools.partial(_na_body, lc=lc, n_rows=n_rows),
        grid=(b, n_rows // NA_ROWS_PER_STEP),
        in_specs=[pl.BlockSpec((1, tq, w), lambda bi, i: (bi, i + off, 0)),
                  pl.BlockSpec((1, t, w), lambda bi, i: (bi, 0, 0)),
                  pl.BlockSpec((1, t, w), lambda bi, i: (bi, 0, 0)),
                  pl.BlockSpec(bias_tab.shape, lambda bi, i: (0, 0, 0))],
        out_specs=pl.BlockSpec((1, tq, w), lambda bi, i: (bi, i, 0)),
        out_shape=jax.ShapeDtypeStruct((b, n, w), BF16),
        compiler_params=_cp(("arbitrary", "arbitrary")),
        name="na_attention",
    )(aq, ak, av, bias_tab)


PAIR_KEY_CHUNK = 256
PAIR_Q_TILE = 512
PAIR_STREAMS = 2


def _pair_attn_body(q_ref, k_ref, v_ref, o_ref, *, kc, shared_kv):
    tq = q_ref.shape[1]
    tk = k_ref.shape[1]
    hs = tq // PAIR_STREAMS
    lane = lax.broadcasted_iota(I32, (hs, LANE), 1)
    lo = lane < HEAD_DIM
    qqs = []
    for si in range(PAIR_STREAMS):
        q = q_ref[0, si * hs:(si + 1) * hs, :]
        zero = jnp.zeros_like(q)
        qqs.append(jnp.concatenate([jnp.where(lo, q, zero), jnp.where(lo, zero, q)], axis=0))
    n_chunks = tk // kc

    def step(c, carry):
        start = 0 if n_chunks == 1 else pl.multiple_of(c * kc, kc)
        kk = k_ref[0, pl.ds(start, kc), :]
        vv = v_ref[0, pl.ds(start, kc), :]
        out = []
        for qq, (m, l, acc) in zip(qqs, carry):
            s = _dot_nt(qq, kk)
            m_new = jnp.maximum(m, jnp.max(s, axis=-1, keepdims=True))
            alpha = jnp.exp2(m - m_new)
            p = jnp.exp2(s - m_new)
            if not shared_kv:
                l = alpha * l + jnp.sum(p, axis=-1, keepdims=True)
            acc = alpha * acc + _dot(p.astype(BF16), vv)
            out.append((m_new, l, acc))
        return tuple(out)

    one = (jnp.full((2 * hs, 1), -jnp.inf, F32), jnp.zeros((2 * hs, 1), F32), jnp.zeros((2 * hs, LANE), F32))
    init = (one,) * PAIR_STREAMS
    res = step(0, init) if n_chunks == 1 else lax.fori_loop(0, n_chunks, step, init, unroll=True)
    for si, (m, l, acc) in enumerate(res):
        a, bb = acc[:hs], acc[hs:]
        if shared_kv:
            oa = a / a[:, HEAD_DIM:HEAD_DIM + 1]
            ob = pltpu.roll(bb / bb[:, HEAD_DIM:HEAD_DIM + 1], HEAD_DIM, 1)
        else:
            oa = a / l[:hs]
            ob = bb / l[hs:]
        o_ref[0, si * hs:(si + 1) * hs, :] = jnp.where(lo, oa, ob).astype(o_ref.dtype)


def _pair_attention(q, k, v, *, q_row0, n_q, kv_of_pair, n_pairs, shared_kv):
    b = q.shape[0]
    tk = k.shape[1]
    tq = PAIR_Q_TILE if (q_row0 % PAIR_Q_TILE == 0 and n_q % PAIR_Q_TILE == 0) else 256
    assert q_row0 % tq == 0 and n_q % tq == 0
    kc = PAIR_KEY_CHUNK if tk % PAIR_KEY_CHUNK == 0 else tk
    off = q_row0 // tq
    return pl.pallas_call(
        functools.partial(_pair_attn_body, kc=kc, shared_kv=shared_kv),
        grid=(b, n_pairs, n_q // tq),
        in_specs=[pl.BlockSpec((1, tq, LANE), lambda bi, p, i: (bi, i + off, p)),
                  pl.BlockSpec((1, tk, LANE), lambda bi, p, i: (bi, 0, kv_of_pair(p))),
                  pl.BlockSpec((1, tk, LANE), lambda bi, p, i: (bi, 0, kv_of_pair(p)))],
        out_specs=pl.BlockSpec((1, tq, LANE), lambda bi, p, i: (bi, i, p)),
        out_shape=jax.ShapeDtypeStruct((b, n_q, n_pairs * LANE), BF16),
        compiler_params=_cp(("arbitrary", "arbitrary", "arbitrary")),
        name="pair_attention",
    )(q, k, v)


def _filter_body(w1t_ref, w1c_ref, w1s_ref, b1_ref, w2_ref, b2_ref, w3_ref, fr_ref, o_ref, *, length, rows, transposed):
    g = pl.program_id(0)
    n_total = 2 * length
    bands = (HY_EMB_DIM - 1) // 2
    f_col = 1e-4 + lax.broadcasted_iota(I32, (bands, 1), 0).astype(F32) * ((bands - 1 - 1e-4) / (bands - 1))
    nch = HY_ORDER * HY_WIDTH
    d_lo = math.log(HY_DECAY_TARGET) / HY_SLOW_DECAY_PCT
    d_hi = math.log(HY_DECAY_TARGET) / HY_FAST_DECAY_PCT
    deltas = d_lo + lax.broadcasted_iota(I32, (1, nch), 1).astype(F32) * ((d_hi - d_lo) / (nch - 1))
    freq = fr_ref[...]
    position = lambda tt: jnp.where(tt < length, tt, n_total - tt).astype(F32)
    pos_row = position(g * rows + lax.broadcasted_iota(I32, (1, rows), 1))
    arg = f_col * (pos_row * (2.0 * math.pi / length))
    pre = (w1t_ref[...] * (pos_row * (1.0 / (length - 1))) + _dot(w1c_ref[...], jnp.cos(arg), HI)
           - _dot(w1s_ref[...], jnp.sin(arg), HI))
    h1 = jnp.sin(freq * (pre + b1_ref[...]))
    h2 = jnp.sin(freq * (_dot(w2_ref[...], h1, HI) + b2_ref[...])).T
    sub = DFT_N2 if transposed else rows
    for j in range(rows // sub):
        tt = g * rows + j * sub + lax.broadcasted_iota(I32, (sub, 1), 0)
        tn = position(tt) * (1.0 / (length - 1))
        filt = _dot(h2[j * sub:(j + 1) * sub], w3_ref[0], HI) * jnp.exp(-tn * jnp.abs(deltas))
        filt = jnp.where(tt == length, 0.0, filt)
        if transposed:
            o_ref[:, j, :] = filt
        else:
            o_ref[...] = filt


def _hyena_filter(length, w1, b1, w2, b2, w3, freq, *, transposed):
    hid = w1.shape[1]
    nch = HY_ORDER * HY_WIDTH
    n_total = 2 * length
    bands = (HY_EMB_DIM - 1) // 2
    rows = SUBLANE * DFT_N2 if transposed else length
    n_steps = n_total // rows
    assert n_total % rows == 0 and n_steps % 2 == 0
    w3d = w3.reshape(hid, 2, nch).transpose(1, 0, 2)
    full = lambda shape: pl.BlockSpec(shape, lambda g: (0,) * len(shape))
    if transposed:
        out_spec = pl.BlockSpec((DFT_N2, SUBLANE, nch), lambda g: (0, g, 0))
        out_shape = jax.ShapeDtypeStruct((DFT_N2, n_total // DFT_N2, nch), F32)
    else:
        out_spec = pl.BlockSpec((rows, nch), lambda g: (g, 0))
        out_shape = jax.ShapeDtypeStruct((n_total, nch), F32)
    return pl.pallas_call(
        functools.partial(_filter_body, length=length, rows=rows, transposed=transposed),
        grid=(n_steps,),
        in_specs=[full((hid, 1)), full((hid, bands)), full((hid, bands)), full((hid, 1)), full((hid, hid)), full((hid, 1)),
                  pl.BlockSpec((1, hid, nch), lambda g: (g // (n_steps // 2), 0, 0)), full((hid, 1))],
        out_specs=out_spec,
        out_shape=out_shape,
        compiler_params=_cp(("arbitrary",)),
        name="hyena_filter",
    )(w1[0:1].T, w1[1:1 + bands].T, w1[1 + bands:].T, b1.reshape(hid, 1), w2.T, b2.reshape(hid, 1), w3d, freq.reshape(hid, 1))


def _dft_tables(n1, n1_used):
    n = n1 * DFT_N2
    n2 = np.arange(DFT_N2)[:, None, None]
    k1 = np.arange(n1)[None, :, None]
    nn1 = np.arange(n1_used)[None, None, :]
    ang = 2.0 * np.pi * ((k1 * (DFT_N2 * nn1 + n2)) % n) / n
    g_fwd = np.concatenate([np.cos(ang), -np.sin(ang)], axis=1)
    g_inv = np.transpose(g_fwd, (0, 2, 1)) / n
    kk = np.arange(DFT_N2)
    a2 = 2.0 * np.pi * ((kk[:, None] * kk[None, :]) % DFT_N2) / DFT_N2
    fr, fi = np.cos(a2), -np.sin(a2)
    f2 = np.block([[fr, -fi], [fi, fr]])
    f2c = np.block([[fr, fi], [-fi, fr]])
    cast = lambda a: jnp.asarray(a.astype(np.float32)).astype(BF16)
    return cast(g_fwd), cast(g_inv), cast(f2), cast(f2c)


def _stage1_body(x_ref, g_ref, o_ref, *, n1):
    for j in range(SUBLANE):
        res = _dot(g_ref[j], x_ref[0, j].astype(BF16))
        o_ref[0, 0, :, j, :] = res[:n1]
        o_ref[0, 1, :, j, :] = res[n1:]


def _dft_stage1(x_t, g_fwd):
    bx, _, n1u, c = x_t.shape
    n1 = g_fwd.shape[1] // 2
    return pl.pallas_call(
        functools.partial(_stage1_body, n1=n1),
        grid=(bx, DFT_N2 // SUBLANE),
        in_specs=[pl.BlockSpec((1, SUBLANE, n1u, c), lambda bi, g: (bi, g, 0, 0)),
                  pl.BlockSpec((SUBLANE, 2 * n1, n1u), lambda bi, g: (g, 0, 0))],
        out_specs=pl.BlockSpec((1, 2, n1, SUBLANE, c), lambda bi, g: (bi, 0, 0, g, 0)),
        out_shape=jax.ShapeDtypeStruct((bx, 2, n1, DFT_N2, c), F32),
        compiler_params=_cp(("arbitrary", "arbitrary")),
        name="dft_stage1",
    )(x_t, g_fwd)


def _stage2_spec_body(a_ref, f2_ref, o_ref):
    for j in range(SUBLANE):
        slab = jnp.concatenate([a_ref[0, 0, j], a_ref[0, 1, j]], axis=0).astype(BF16)
        o_ref[j] = _dot(f2_ref[...], slab).astype(o_ref.dtype)


def _dft_stage2_spectrum(a, f2):
    _, _, n1, _, c = a.shape
    return pl.pallas_call(
        _stage2_spec_body,
        grid=(n1 // SUBLANE,),
        in_specs=[pl.BlockSpec((1, 2, SUBLANE, DFT_N2, c), lambda g: (0, 0, g, 0, 0)),
                  pl.BlockSpec(f2.shape, lambda g: (0, 0))],
        out_specs=pl.BlockSpec((SUBLANE, 2 * DFT_N2, c), lambda g: (g, 0, 0)),
        out_shape=jax.ShapeDtypeStruct((n1, 2 * DFT_N2, c), BF16),
        compiler_params=_cp(("arbitrary",)),
        name="dft_stage2_spectrum",
    )(a, f2)


def _stage2_conv_body(a_ref, k_ref, f2_ref, f2c_ref, o_ref):
    h = DFT_N2
    for j in range(SUBLANE):
        kf = k_ref[j].astype(F32)
        kr, ki = kf[:h], kf[h:]
        for bi in range(a_ref.shape[0]):
            slab = jnp.concatenate([a_ref[bi, 0, j], a_ref[bi, 1, j]], axis=0).astype(BF16)
            x = _dot(f2_ref[...], slab)
            xr, xi = x[:h], x[h:]
            y = jnp.concatenate([xr * kr - xi * ki, xr * ki + xi * kr], axis=0).astype(BF16)
            bv = _dot(f2c_ref[...], y)
            o_ref[bi, 0, :, j, :] = bv[:h]
            o_ref[bi, 1, :, j, :] = bv[h:]


def _dft_stage2_conv(a, kspec, order, f2, f2c):
    bx, _, n1, _, c = a.shape
    return pl.pallas_call(
        _stage2_conv_body,
        grid=(n1 // SUBLANE,),
        in_specs=[pl.BlockSpec((bx, 2, SUBLANE, DFT_N2, c), lambda g: (0, 0, g, 0, 0)),
                  pl.BlockSpec((SUBLANE, 2 * DFT_N2, c), lambda g: (g, 0, order)),
                  pl.BlockSpec(f2.shape, lambda g: (0, 0)),
                  pl.BlockSpec(f2c.shape, lambda g: (0, 0))],
        out_specs=pl.BlockSpec((bx, 2, DFT_N2, SUBLANE, c), lambda g: (0, 0, 0, g, 0)),
        out_shape=jax.ShapeDtypeStruct((bx, 2, DFT_N2, n1, c), F32),
        compiler_params=_cp(("arbitrary",)),
        name="dft_stage2_conv",
    )(a, kspec, f2, f2c)


def _inverse_body(b_ref, g_ref, u_ref, gate_ref, bias_ref, *rest, last, n1):
    if last:
        (o_ref,) = rest
    else:
        gf_ref, o_ref, a_ref = rest
    for j in range(SUBLANE):
        slab = jnp.concatenate([b_ref[0, 0, j], b_ref[0, 1, j]], axis=0).astype(BF16)
        y = _dot(g_ref[j], slab)
        out = gate_ref[0, j] * (y + u_ref[0, j] * bias_ref[...])
        if last:
            o_ref[0, :, j, :] = out
        else:
            o_ref[0, j] = out
            res = _dot(gf_ref[j], out.astype(BF16))
            a_ref[0, 0, :, j, :] = res[:n1]
            a_ref[0, 1, :, j, :] = res[n1:]


def _dft_inverse(bv, g_inv, u_t, gate_t, bias, g_fwd, *, last):
    bx, _, _, n1, c = bv.shape
    n1u = g_inv.shape[1]
    t_spec = pl.BlockSpec((1, SUBLANE, n1u, c), lambda bi, g: (bi, g, 0, 0))
    in_specs = [pl.BlockSpec((1, 2, SUBLANE, n1, c), lambda bi, g: (bi, 0, g, 0, 0)),
                pl.BlockSpec((SUBLANE, n1u, 2 * n1), lambda bi, g: (g, 0, 0)),
                t_spec, t_spec,
                pl.BlockSpec((1, c), lambda bi, g: (0, 0))]
    args = [bv, g_inv, u_t, gate_t, bias.reshape(1, c)]
    if last:
        out_specs = pl.BlockSpec((1, n1u, SUBLANE, c), lambda bi, g: (bi, 0, g, 0))
        out_shape = jax.ShapeDtypeStruct((bx, n1u, DFT_N2, c), F32)
    else:
        in_specs.append(pl.BlockSpec((SUBLANE, 2 * n1, n1u), lambda bi, g: (g, 0, 0)))
        args.append(g_fwd)
        out_specs = [t_spec, pl.BlockSpec((1, 2, n1, SUBLANE, c), lambda bi, g: (bi, 0, 0, g, 0))]
        out_shape = [jax.ShapeDtypeStruct((bx, DFT_N2, n1u, c), F32), jax.ShapeDtypeStruct((bx, 2, n1, DFT_N2, c), F32)]
    return pl.pallas_call(
        functools.partial(_inverse_body, last=last, n1=n1),
        grid=(bx, DFT_N2 // SUBLANE),
        in_specs=in_specs,
        out_specs=out_specs,
        out_shape=out_shape,
        compiler_params=_cp(("arbitrary", "arbitrary")),
        name="dft_inverse",
    )(*args)


def _shortconv_body(prev_ref, cur_ref, next_ref, w_ref, b_ref, v_ref, x1_ref, x2_ref, *, n_steps):
    i = pl.program_id(1)
    u = cur_ref[0]
    rows = u.shape[0]
    prev_row = jnp.where(i > 0, prev_ref[0, SUBLANE - 1:SUBLANE, :], 0.0)
    next_row = jnp.where(i < n_steps - 1, next_ref[0, 0:1, :], 0.0)
    ridx = lax.broadcasted_iota(I32, u.shape, 0)
    up = jnp.where(ridx == 0, prev_row, pltpu.roll(u, 1, 0))
    dn = jnp.where(ridx == rows - 1, next_row, pltpu.roll(u, rows - 1, 0))
    z = up * w_ref[0:1] + u * w_ref[1:2] + dn * w_ref[2:3] + b_ref[...]
    cw = HY_WIDTH
    for j in range(rows // DFT_N2):
        zj = z[j * DFT_N2:(j + 1) * DFT_N2]
        v_ref[0, :, j, :] = zj[:, :cw]
        x1_ref[0, :, j, :] = zj[:, cw:2 * cw]
        x2_ref[0, :, j, :] = zj[:, 2 * cw:]


def _short_conv_t(cu, conv_w, conv_b):
    b, length, c3 = cu.shape
    rows = SUBLANE * DFT_N2
    assert length % rows == 0
    n_steps = length // rows
    rb = rows // SUBLANE
    n1u = length // DFT_N2
    out_spec = pl.BlockSpec((1, DFT_N2, SUBLANE, HY_WIDTH), lambda bi, i: (bi, 0, i, 0))
    out_shape = jax.ShapeDtypeStruct((b, DFT_N2, n1u, HY_WIDTH), F32)
    return pl.pallas_call(
        functools.partial(_shortconv_body, n_steps=n_steps),
        grid=(b, n_steps),
        in_specs=[pl.BlockSpec((1, SUBLANE, c3), lambda bi, i: (bi, jnp.maximum(i * rb - 1, 0), 0)),
                  pl.BlockSpec((1, rows, c3), lambda bi, i: (bi, i, 0)),
                  pl.BlockSpec((1, SUBLANE, c3), lambda bi, i: (bi, jnp.minimum((i + 1) * rb, n_steps * rb - 1), 0)),
                  pl.BlockSpec((3, c3), lambda bi, i: (0, 0)),
                  pl.BlockSpec((1, c3), lambda bi, i: (0, 0))],
        out_specs=[out_spec] * 3,
        out_shape=[out_shape] * 3,
        compiler_params=_cp(("arbitrary", "arbitrary")),
        name="hyena_short_conv",
    )(cu, cu, cu, conv_w, conv_b.reshape(1, c3))


def _hyena_long(cu_lat, conv_w, conv_b, filt_params, bias_d):
    b, length, _ = cu_lat.shape
    n1u = length // DFT_N2
    n1 = 2 * n1u
    g_sig, g_inv, f2, f2c = _dft_tables(n1, n1u)
    g_full = _dft_tables(n1, n1)[0]
    filt_t = _hyena_filter(length, *filt_params, transposed=True)
    kspec = _dft_stage2_spectrum(_dft_stage1(filt_t[None], g_full), f2)
    v_t, x1_t, x2_t = _short_conv_t(cu_lat, conv_w, conv_b)
    y = v_t
    a = _dft_stage1(y, g_sig)
    for o, gate in enumerate((x1_t, x2_t)):
        bv = _dft_stage2_conv(a, kspec, o, f2, f2c)
        if o == HY_ORDER - 1:
            y = _dft_inverse(bv, g_inv, y, gate, bias_d[o], g_sig, last=True)
        else:
            y, a = _dft_inverse(bv, g_inv, y, gate, bias_d[o], g_sig, last=False)
    return y.reshape(b, length, HY_WIDTH)


def _hyena_small_body(cu_ref, w_ref, b_ref, ks_ref, ff_ref, fi_ref, bias_ref, o_ref):
    u = cu_ref[0]
    rows = u.shape[0]
    ridx = lax.broadcasted_iota(I32, u.shape, 0)
    up = jnp.where(ridx == 0, 0.0, pltpu.roll(u, 1, 0))
    dn = jnp.where(ridx == rows - 1, 0.0, pltpu.roll(u, rows - 1, 0))
    z = up * w_ref[0:1] + u * w_ref[1:2] + dn * w_ref[2:3] + b_ref[...]
    cw = HY_WIDTH
    nf = ff_ref.shape[0] // 2
    y = z[:, :cw]
    for o in range(HY_ORDER):
        gate = z[:, (o + 1) * cw:(o + 2) * cw]
        x = _dot(ff_ref[...], y.astype(BF16))
        kf = ks_ref[:, o * cw:(o + 1) * cw]
        xr, xi, kr, ki = x[:nf], x[nf:], kf[:nf], kf[nf:]
        prod = jnp.concatenate([xr * kr - xi * ki, xr * ki + xi * kr], axis=0).astype(BF16)
        conv = _dot(fi_ref[...], prod)
        y = gate * (conv + y * bias_ref[o:o + 1])
    o_ref[0] = y


def _spectrum_small_body(ff_ref, f_ref, o_ref):
    o_ref[...] = _dot(ff_ref[...], f_ref[...].astype(BF16))


def _hyena_small(cu_ctx, conv_w, conv_b, filt_params, bias_d):
    b, length, c3 = cu_ctx.shape
    n = 2 * length
    kt = np.arange(n)[:, None] * np.arange(n)[None, :]
    ang = 2.0 * np.pi * (kt % n) / n
    fwd = np.concatenate([np.cos(ang), -np.sin(ang)], axis=0)
    cast = lambda a: jnp.asarray(a.astype(np.float32)).astype(BF16)
    f_full = cast(fwd)
    f_sig = cast(fwd[:, :length])
    f_inv = cast(np.transpose(fwd[:, :length]) / n)
    filt = _hyena_filter(length, *filt_params, transposed=False)
    nch = filt.shape[1]
    kspec = pl.pallas_call(
        _spectrum_small_body,
        out_shape=jax.ShapeDtypeStruct((2 * n, nch), F32),
        compiler_params=_cp(None),
        name="hyena_small_spectrum",
    )(f_full, filt)
    full = lambda shape: pl.BlockSpec(shape, lambda bi: (0,) * len(shape))
    return pl.pallas_call(
        _hyena_small_body,
        grid=(b,),
        in_specs=[pl.BlockSpec((1, length, c3), lambda bi: (bi, 0, 0)), full((3, c3)), full((1, c3)), full(kspec.shape),
                  full(f_sig.shape), full(f_inv.shape), full(bias_d.shape)],
        out_specs=pl.BlockSpec((1, length, HY_WIDTH), lambda bi: (bi, 0, 0)),
        out_shape=jax.ShapeDtypeStruct((b, length, HY_WIDTH), F32),
        compiler_params=_cp(("arbitrary",)),
        name="hyena_small",
    )(cu_ctx, conv_w, conv_b.reshape(1, c3), kspec, f_sig, f_inv, bias_d)


def _merge_body(xctx_ref, xcur_ref, mod_ref, yac_ref, yal_ref, ybc_ref, ybl_ref, ycc_ref, ycl_ref, g_ref, wa_ref, wb_ref,
                wc_ref, wo_ref, nw_ref, rwh_ref, rwl_ref, xo_ref, h_ref, lg_ref, *, ctx_tiles):
    d = xcur_ref.shape[2]
    is_ctx = pl.program_id(1) < ctx_tiles
    ya = jnp.where(is_ctx, yac_ref[0], yal_ref[0])
    yb = jnp.where(is_ctx, ybc_ref[0], ybl_ref[0])
    yc = jnp.where(is_ctx, ycc_ref[0], ycl_ref[0])
    g = g_ref[0].astype(F32)
    m = (g[:, :d] * _dot(ya, wa_ref[...]) + g[:, d:2 * d] * _dot(yb, wb_ref[...])
         + g[:, 2 * d:] * _dot(yc.astype(BF16), wc_ref[...]))
    y = _dot(m.astype(BF16), wo_ref[...])
    mod = mod_ref[0]
    x = jnp.where(is_ctx, xctx_ref[0], xcur_ref[0]) + mod[2:3] * y
    xo_ref[0] = x
    hn = x * lax.rsqrt(jnp.mean(x * x, axis=-1, keepdims=True) + RMS_EPS) * nw_ref[...]
    h = hn * (1.0 + mod[4:5]) + mod[3:4]
    h_ref[...] = h
    h_hi = h.astype(BF16)
    h_lo = (h - h_hi.astype(F32)).astype(BF16)
    lg_ref[...] = _dot(h_hi, rwh_ref[...]) + _dot(h_hi, rwl_ref[...]) + _dot(h_lo, rwh_ref[...])


def _merge(xc, mod, ya, yb, yc, g, w_a, w_b, w_c, w_o, norm2_w, router_w, lc):
    tm = 256
    x_specs, x_args, (b, t, d) = _stream_inputs(xc, tm, lc)
    nb = b
    ctx_tiles = lc // tm
    nt = t // tm
    ne = router_w.shape[1]
    rw = jnp.pad(router_w.astype(F32), ((0, 0), (0, LANE - ne)))
    rw_hi = rw.astype(BF16)
    rw_lo = (rw - rw_hi.astype(F32)).astype(BF16)
    full = lambda shape: pl.BlockSpec(shape, lambda bi, ti: (0,) * len(shape))
    tok = lambda width: pl.BlockSpec((1, tm, width), lambda bi, ti: (bi, ti, 0))
    ctx_tok = lambda width: pl.BlockSpec((1, tm, width), lambda bi, ti: (bi, jnp.minimum(ti, ctx_tiles - 1), 0))
    lat_tok = lambda width: pl.BlockSpec((1, tm, width), lambda bi, ti: (bi, jnp.maximum(ti - ctx_tiles, 0), 0))
    branch_specs = []
    for pair in (ya, yb, yc):
        branch_specs += [ctx_tok(pair[0].shape[2]), lat_tok(pair[1].shape[2])]
    return pl.pallas_call(
        functools.partial(_merge_body, ctx_tiles=ctx_tiles),
        grid=(b, nt),
        in_specs=x_specs
                 + [pl.BlockSpec((1, N_MOD, d), lambda bi, ti: (jnp.where(ti < ctx_tiles, nb, bi), 0, 0))]
                 + branch_specs
                 + [tok(g.shape[2]), full(w_a.shape), full(w_b.shape), full(w_c.shape), full(w_o.shape), full((1, d)),
                    full(rw_hi.shape), full(rw_lo.shape)],
        out_specs=[tok(d),
                   pl.BlockSpec((tm, d), lambda bi, ti: (bi * nt + ti, 0)),
                   pl.BlockSpec((tm, LANE), lambda bi, ti: (bi * nt + ti, 0))],
        out_shape=[jax.ShapeDtypeStruct((b, t, d), F32),
                   jax.ShapeDtypeStruct((b * t, d), F32),
                   jax.ShapeDtypeStruct((b * t, LANE), F32)],
        compiler_params=_cp(("arbitrary", "arbitrary")),
        name="merge_outproj",
    )(*x_args, mod, ya[0], ya[1], yb[0], yb[1], yc[0], yc[1], g, w_a, w_b, w_c, w_o, norm2_w.reshape(1, d), rw_hi, rw_lo)


MOE_ROW_TILE = 512
MOE_TOKEN_TILE = 256
MOE_RUN_ALIGN = SUBLANE
MOE_LOCAL_ROWS = -(-(2 * MOE_TOKEN_TILE + N_EXPERTS * (MOE_RUN_ALIGN - 1)) // LANE) * LANE


def _top2_of4(a):
    m1 = jnp.maximum(jnp.maximum(a[0], a[1]), jnp.maximum(a[2], a[3]))
    i1 = jnp.where(a[0] == m1, 0, jnp.where(a[1] == m1, 1, jnp.where(a[2] == m1, 2, 3)))
    neg = jnp.full_like(m1, -jnp.inf)
    r = [jnp.where(i1 == j, neg, a[j]) for j in range(4)]
    m2 = jnp.maximum(jnp.maximum(r[0], r[1]), jnp.maximum(r[2], r[3]))
    i2 = jnp.where((r[0] == m2) & (i1 != 0), 0,
                   jnp.where((r[1] == m2) & (i1 != 1), 1, jnp.where((r[2] == m2) & (i1 != 2), 2, 3)))
    return m1, m2, i1, i2


def _route_body(lg_ref, rb_ref, tri_ref, e_ref, w_ref, rank_ref, cnt_ref):
    scores = jax.nn.sigmoid(lg_ref[...])
    sel = scores + rb_ref[...]
    rows = [sel[e:e + 1] for e in range(N_EXPERTS)]
    tops = [_top2_of4(rows[4 * g:4 * g + 4]) for g in range(N_GROUPS)]
    gs = [t[0] + t[1] for t in tops]
    gmax = jnp.maximum(jnp.maximum(gs[0], gs[1]), jnp.maximum(gs[2], gs[3]))
    gi = jnp.where(gs[0] == gmax, 0, jnp.where(gs[1] == gmax, 1, jnp.where(gs[2] == gmax, 2, 3)))
    pick = lambda k: jnp.where(gi == 0, tops[0][k], jnp.where(gi == 1, tops[1][k], jnp.where(gi == 2, tops[2][k], tops[3][k])))
    e0 = gi * EXPERTS_PER_GROUP + pick(2)
    e1 = gi * EXPERTS_PER_GROUP + pick(3)
    eid = lax.broadcasted_iota(I32, scores.shape, 0)
    oh0 = eid == e0
    oh1 = eid == e1
    s0 = jnp.sum(jnp.where(oh0, scores, 0.0), axis=0, keepdims=True)
    s1 = jnp.sum(jnp.where(oh1, scores, 0.0), axis=0, keepdims=True)
    tot = s0 + s1
    e_ref[...] = jnp.concatenate([e0, e1], axis=0)
    w_ref[...] = jnp.concatenate([s0 / tot, s1 / tot], axis=0)
    oh = (oh0 | oh1).astype(BF16)
    incl = _dot(oh, tri_ref[...])
    excl = incl - 1.0
    r0 = jnp.sum(jnp.where(oh0, excl, 0.0), axis=0, keepdims=True)
    r1 = jnp.sum(jnp.where(oh1, excl, 0.0), axis=0, keepdims=True)
    rank_ref[...] = jnp.concatenate([r0, r1], axis=0).astype(I32)
    tl = incl.shape[1]
    cnt_ref[0] = jnp.broadcast_to(incl[:, tl - 1:tl], cnt_ref.shape[1:]).astype(I32)


def _route(logits_t, router_b):
    ne, t = logits_t.shape
    tl = MOE_TOKEN_TILE
    assert t % tl == 0
    tri = jnp.asarray(np.triu(np.ones((tl, tl), np.float32)), BF16)
    two = lambda dt: jax.ShapeDtypeStruct((2, t), dt)
    return pl.pallas_call(
        _route_body,
        grid=(t // tl,),
        in_specs=[pl.BlockSpec((ne, tl), lambda i: (0, i)), pl.BlockSpec((ne, 1), lambda i: (0, 0)),
                  pl.BlockSpec((tl, tl), lambda i: (0, 0))],
        out_specs=[pl.BlockSpec((2, tl), lambda i: (0, i))] * 3 + [pl.BlockSpec((1, ne, LANE), lambda i: (i, 0, 0))],
        out_shape=[two(I32), two(F32), two(I32), jax.ShapeDtypeStruct((t // tl, ne, LANE), I32)],
        compiler_params=_cp(("arbitrary",)),
        name="moe_route",
    )(logits_t, router_b.reshape(ne, 1).astype(F32), tri)


RUN_PIECES = [1 << b for b in reversed(range(MOE_RUN_ALIGN.bit_length() - 1, MOE_ROW_TILE.bit_length() - 1))]


def _when_go(cp, cond, wait):
    @pl.when(cond)
    def _():
        cp.wait() if wait else cp.start()


def _run_copies(run_ref, local_ref, slots_ref, sem, *, to_slots, wait):
    for e in range(N_EXPERTS):
        srow = run_ref[0, 0, e]
        n = run_ref[0, 0, N_EXPERTS + e]
        lrow = run_ref[0, 0, 2 * N_EXPERTS + e]
        for s in RUN_PIECES:
            s_sl = slots_ref.at[pl.ds(pl.multiple_of(srow, MOE_RUN_ALIGN), s), :]
            l_sl = local_ref.at[pl.ds(pl.multiple_of(lrow, MOE_RUN_ALIGN), s), :]
            cp = pltpu.make_async_copy(l_sl, s_sl, sem) if to_slots else pltpu.make_async_copy(s_sl, l_sl, sem)
            _when_go(cp, (n & s) != 0, wait)
            srow = srow + (n & s)
            lrow = lrow + (n & s)


def _dispatch_body(run_ref, prev_ref, pad_ref, loc_ref, h_ref, xs_ref, sem, pbuf, zbuf, zsem, *, n_tail):
    tm = h_ref.shape[0]
    step = pl.program_id(0)
    buf = step % 2

    def zero_fill(wait):
        for e in range(N_EXPERTS):
            cur = pad_ref[0, e]
            n_pad = pad_ref[0, N_EXPERTS + e]
            for s in RUN_PIECES:
                dst = xs_ref.at[pl.ds(pl.multiple_of(cur, MOE_RUN_ALIGN), s), :]
                _when_go(pltpu.make_async_copy(zbuf.at[pl.ds(0, s), :], dst, zsem), (n_pad & s) != 0, wait)
                cur = cur + (n_pad & s)
        zrows = zbuf.shape[0]
        tail = pad_ref[0, 2 * N_EXPERTS]
        for k in range(n_tail):
            row = pl.multiple_of(tail + k * zrows, zrows)
            _when_go(pltpu.make_async_copy(zbuf, xs_ref.at[pl.ds(row, zrows), :], zsem), row < xs_ref.shape[0], wait)

    @pl.when(step == 0)
    def _():
        zbuf[...] = jnp.zeros_like(zbuf)
        zero_fill(False)
        zero_fill(True)

    loc = loc_ref[...]
    rid = lax.broadcasted_iota(I32, (pbuf.shape[1], tm), 0)
    onehot = ((rid == loc[0:1]) | (rid == loc[1:2])).astype(BF16)
    pbuf[buf] = _dot(onehot, h_ref[...].astype(BF16))
    _run_copies(run_ref, pbuf.at[buf], xs_ref, sem.at[buf], to_slots=True, wait=False)

    @pl.when(step > 0)
    def _():
        _run_copies(prev_ref, pbuf.at[1 - buf], xs_ref, sem.at[1 - buf], to_slots=True, wait=True)

    @pl.when(step == pl.num_programs(0) - 1)
    def _():
        _run_copies(run_ref, pbuf.at[buf], xs_ref, sem.at[buf], to_slots=True, wait=True)


def _dispatch(h2, run_info, pad_info, loc, s_pad):
    t, d = h2.shape
    tm = MOE_TOKEN_TILE
    zrows = MOE_ROW_TILE // 2
    return pl.pallas_call(
        functools.partial(_dispatch_body, n_tail=(s_pad - 2 * t) // zrows),
        grid=(t // tm,),
        in_specs=[pl.BlockSpec((1, 1, run_info.shape[2]), lambda i: (i, 0, 0), memory_space=pltpu.SMEM),
                  pl.BlockSpec((1, 1, run_info.shape[2]), lambda i: (jnp.maximum(i - 1, 0), 0, 0), memory_space=pltpu.SMEM),
                  pl.BlockSpec(pad_info.shape, lambda i: (0, 0), memory_space=pltpu.SMEM),
                  pl.BlockSpec((2, tm), lambda i: (0, i)),
                  pl.BlockSpec((tm, d), lambda i: (i, 0))],
        out_specs=pl.BlockSpec(memory_space=pl.ANY),
        out_shape=jax.ShapeDtypeStruct((s_pad, d), F32),
        scratch_shapes=[pltpu.SemaphoreType.DMA((2,)), pltpu.VMEM((2, MOE_LOCAL_ROWS, d), F32), pltpu.VMEM((zrows, d), F32),
                        pltpu.SemaphoreType.DMA(())],
        compiler_params=pltpu.CompilerParams(dimension_semantics=("arbitrary",), has_side_effects=True,
                                             vmem_limit_bytes=VMEM_LIMIT),
        name="moe_dispatch",
    )(run_info, run_info, pad_info, loc, h2)


def _ffn_body(te_ref, nu_ref, x_ref, wg_ref, wu_ref, wd_ref, o_ref):
    i = pl.program_id(0)

    @pl.when(i < nu_ref[0])
    def _():
        x = x_ref[...].astype(BF16)
        hmid = (_silu(_dot_nt(x, wg_ref[0, 0].astype(BF16))) * _dot_nt(x, wu_ref[0, 0].astype(BF16))).astype(BF16)
        o_ref[...] = _dot(hmid, wd_ref[0, 0].astype(BF16))

    @pl.when(i >= nu_ref[0])
    def _():
        o_ref[...] = jnp.zeros_like(o_ref)


def _grouped_ffn(xs, tile_expert, n_used, wg, wu, wd, layer):
    s_pad, d = xs.shape
    tr = MOE_ROW_TILE
    n_tiles = s_pad // tr
    ff = wg.shape[2]
    grid_spec = pltpu.PrefetchScalarGridSpec(
        num_scalar_prefetch=2,
        grid=(n_tiles,),
        in_specs=[pl.BlockSpec((tr, d), lambda i, te, nu: (jnp.minimum(i, nu[0] - 1), 0)),
                  pl.BlockSpec((1, 1, ff, d), lambda i, te, nu: (layer, te[i], 0, 0)),
                  pl.BlockSpec((1, 1, ff, d), lambda i, te, nu: (layer, te[i], 0, 0)),
                  pl.BlockSpec((1, 1, ff, d), lambda i, te, nu: (layer, te[i], 0, 0))],
        out_specs=pl.BlockSpec((tr, d), lambda i, te, nu: (i, 0)),
    )
    return pl.pallas_call(
        _ffn_body,
        grid_spec=grid_spec,
        out_shape=jax.ShapeDtypeStruct((s_pad, d), F32),
        compiler_params=_cp(("arbitrary",)),
        name="moe_grouped_ffn",
    )(tile_expert, n_used, xs, wg, wu, wd)


def _combine_body(run_ref, next_ref, x_ref, mod_ref, w_ref, loc_ref, nw_ref, ys_ref, o_ref, ybuf, sem, *, final):
    tm = x_ref.shape[1]
    step = pl.program_id(0) * pl.num_programs(1) + pl.program_id(1)
    n_steps = pl.num_programs(0) * pl.num_programs(1)
    buf = step % 2

    @pl.when(step == 0)
    def _():
        ybuf[...] = jnp.zeros_like(ybuf)
        _run_copies(run_ref, ybuf.at[0], ys_ref, sem.at[0], to_slots=False, wait=False)

    @pl.when(step + 1 < n_steps)
    def _():
        _run_copies(next_ref, ybuf.at[1 - buf], ys_ref, sem.at[1 - buf], to_slots=False, wait=False)

    _run_copies(run_ref, ybuf.at[buf], ys_ref, sem.at[buf], to_slots=False, wait=True)
    y16 = ybuf[buf].astype(BF16)
    cid = lax.broadcasted_iota(I32, (tm, ybuf.shape[1]), 1)
    pick0 = (cid == loc_ref[:, 0:1]).astype(BF16)
    pick1 = (cid == loc_ref[:, 1:2]).astype(BF16)
    y = w_ref[:, 0:1] * _dot(pick0, y16) + w_ref[:, 1:2] * _dot(pick1, y16)
    x = x_ref[0] + mod_ref[0][5:6] * y
    if final:
        x = x * lax.rsqrt(jnp.mean(x * x, axis=-1, keepdims=True) + RMS_EPS) * nw_ref[...]
    o_ref[0] = x


def _moe_combine(xc, mod, ys, run_info, w_t, loc_t, final_w, lc, *, final):
    b, t, d = xc.shape
    tm = MOE_TOKEN_TILE
    nb = b
    ctx_tiles = lc // tm
    nt = t // tm
    row0 = ctx_tiles if final else 0
    n_out = t - row0 * tm
    steps_per_b = nt - row0

    def next_tile(bi, ti):
        wrap = ti == steps_per_b - 1
        nb_i = jnp.minimum(jnp.where(wrap, bi + 1, bi), b - 1)
        nt_i = jnp.where(wrap, 0, ti + 1)
        return (nb_i * nt + nt_i + row0, 0, 0)

    return pl.pallas_call(
        functools.partial(_combine_body, final=final),
        grid=(b, nt - row0),
        in_specs=[pl.BlockSpec((1, 1, run_info.shape[2]), lambda bi, ti: (bi * nt + ti + row0, 0, 0), memory_space=pltpu.SMEM),
                  pl.BlockSpec((1, 1, run_info.shape[2]), next_tile, memory_space=pltpu.SMEM),
                  pl.BlockSpec((1, tm, d), lambda bi, ti: (bi, ti + row0, 0)),
                  pl.BlockSpec((1, N_MOD, d), lambda bi, ti: (jnp.where(ti + row0 < ctx_tiles, nb, bi), 0, 0)),
                  pl.BlockSpec((tm, 2), lambda bi, ti: (bi * nt + ti + row0, 0)),
                  pl.BlockSpec((tm, 2), lambda bi, ti: (bi * nt + ti + row0, 0)),
                  pl.BlockSpec((1, d), lambda bi, ti: (0, 0)),
                  pl.BlockSpec(memory_space=pl.ANY)],
        out_specs=pl.BlockSpec((1, tm, d), lambda bi, ti: (bi, ti, 0)),
        out_shape=jax.ShapeDtypeStruct((b, n_out, d), F32),
        scratch_shapes=[pltpu.VMEM((2, MOE_LOCAL_ROWS, d), F32), pltpu.SemaphoreType.DMA((2,))],
        compiler_params=_cp(("arbitrary", "arbitrary")),
        name="moe_combine",
    )(run_info, run_info, xc, mod, w_t, loc_t, final_w.reshape(1, d), ys)


def _moe_experts(h2, logits_t, router_b, wg, wu, wd, layer):
    t = h2.shape[0]
    tr = MOE_ROW_TILE
    tm = MOE_TOKEN_TILE
    al = MOE_RUN_ALIGN
    nt = t // tm
    e_idx, w_tok, lrank, cnt = _route(logits_t, router_b)
    cnt = cnt[:, :, 0]
    run_len = (cnt + al - 1) // al * al
    total = jnp.sum(run_len, axis=0)
    padded = (total + tr - 1) // tr * tr
    ends = jnp.cumsum(padded)
    starts = ends - padded
    run_start = starts[None, :] + jnp.cumsum(run_len, axis=0) - run_len
    loc_start = jnp.cumsum(run_len, axis=1) - run_len
    run_info = jnp.concatenate([run_start, run_len, loc_start], axis=1).astype(I32).reshape(nt, 1, 3 * N_EXPERTS)
    e3 = e_idx.reshape(2, nt, tm)
    loc = jnp.zeros_like(e3)
    for e in range(N_EXPERTS):
        loc = jnp.where(e3 == e, loc_start[None, :, e, None], loc)
    loc = (loc + lrank.reshape(2, nt, tm)).reshape(2, t).astype(I32)
    s_pad = (-(-(2 * t + nt * N_EXPERTS * (al - 1)) // tr) + N_EXPERTS) * tr
    n_tiles = s_pad // tr
    tile_start = jnp.arange(n_tiles, dtype=I32) * tr
    n_used = (ends[-1] // tr).astype(I32)
    tile_expert = jnp.sum((ends[None, :] <= tile_start[:, None]).astype(I32), axis=1)
    last_used = jnp.sum((ends <= (n_used - 1) * tr).astype(I32))
    tile_expert = jnp.minimum(jnp.where(jnp.arange(n_tiles) < n_used, tile_expert, last_used), N_EXPERTS - 1).astype(I32)
    pad_info = jnp.concatenate([starts + total, padded - total, ends[-1:]]).astype(I32).reshape(1, 2 * N_EXPERTS + 1)
    xs = _dispatch(h2, run_info, pad_info, loc, s_pad)
    ys = _grouped_ffn(xs, tile_expert, n_used.reshape(1), wg, wu, wd, layer)
    return ys, run_info, w_tok.T, loc.T


def kernel(x, c, ctx, c_ctx, ada_w, ada_b, norm1_w, norm2_w, w_in, b_gate, na_rpb, q_norm_w, k_norm_w, hy_conv_w, hy_conv_b,
           hy_f_w1, hy_f_b1, hy_f_w2, hy_f_b2, hy_f_w3, hy_sin_freq, hy_bias_d, w_br_a, w_br_b, w_br_c, w_out, router_w,
           router_b, exp_w_gate, exp_w_up, exp_w_down, final_norm_w):
    b, n, d = x.shape
    lc = ctx.shape[1]
    depth = ada_w.shape[0]
    t = lc + n
    cos_t, sin_t = _rope_tables(lc, n)
    xc = (ctx, x)
    cvec = jnp.concatenate([c, c_ctx[None], jnp.zeros((SUBLANE - b - 1, d), F32)], axis=0)
    na_w = NA_HEADS * HEAD_DIM
    exp_w_gate_t = jnp.swapaxes(exp_w_gate, 2, 3)
    exp_w_up_t = jnp.swapaxes(exp_w_up, 2, 3)
    out = None
    for layer in range(depth):
        mod = _ada_mod(cvec, ada_w, ada_b, layer)[:b + 1].reshape(b + 1, N_MOD, d)
        aq, ak, av, bq, bq_lat, bk, bv, cu, cu_lat, g = _norm_inproj(xc, mod, norm1_w[layer], w_in[layer].astype(BF16),
                                                                     b_gate[layer], q_norm_w[layer], k_norm_w[layer],
                                                                     cos_t, sin_t, lc)
        bq_ctx = bq[:, :lc]
        cu_ctx = cu[:, :lc]
        filt_params = (hy_f_w1[layer], hy_f_b1[layer], hy_f_w2[layer], hy_f_b2[layer], hy_f_w3[layer], hy_sin_freq[layer])
        ya_lat = _na_attention(aq, ak, av, _na_bias_table(na_rpb[layer]), lc)
        ya_ctx = _pair_attention(aq, ak[:, :lc], av[:, :lc], q_row0=0, n_q=lc, kv_of_pair=lambda p: p, n_pairs=na_w // LANE,
                                 shared_kv=False)
        gq = GQA_Q_HEADS * HEAD_DIM // LANE
        yb_lat = _pair_attention(bq_lat, bk, bv, q_row0=0, n_q=n, kv_of_pair=lambda p: p // 2, n_pairs=gq, shared_kv=True)
        yb_ctx = _pair_attention(bq_ctx, bk[:, :lc], bv[:, :lc], q_row0=0, n_q=lc, kv_of_pair=lambda p: p // 2, n_pairs=gq,
                                 shared_kv=True)
        yc_lat = _hyena_long(cu_lat, hy_conv_w[layer], hy_conv_b[layer], filt_params, hy_bias_d[layer])
        yc_ctx = _hyena_small(cu_ctx, hy_conv_w[layer], hy_conv_b[layer], filt_params, hy_bias_d[layer])
        xc, h2, logits = _merge(xc, mod, (ya_ctx, ya_lat), (yb_ctx, yb_lat), (yc_ctx, yc_lat), g, w_br_a[layer].astype(BF16),
                                w_br_b[layer].astype(BF16), w_br_c[layer].astype(BF16), w_out[layer].astype(BF16),
                                norm2_w[layer], router_w, lc)
        logits_t = logits[:, :N_EXPERTS].T
        ys, run_info, w_t, loc_t = _moe_experts(h2, logits_t, router_b, exp_w_gate_t, exp_w_up_t, exp_w_down, layer)
        final = layer == depth - 1
        res = _moe_combine(xc, mod, ys, run_info, w_t, loc_t, final_norm_w, lc, final=final)
        if final:
            out = res
        else:
            xc = res
    return out
```
